```python
import jax
import jax.numpy as jnp
from jax import lax
import numpy as np

D_MODEL = 1024
BATCH = 8
SEQ = 2048
DEPTH = 1
DEC_BATCH = 32
DEC_SEQ = 32
PAST_LEN = 1024

CHUNK = 64
HEAD_DIM = 64
ATT_WIDTH = D_MODEL // 2
N_HEADS = ATT_WIDTH // HEAD_DIM
N_KV_HEADS = N_HEADS // 2
GQA_GROUP = N_HEADS // N_KV_HEADS
ROT_DIM = HEAD_DIM // 4
ROPE_THETA = 500000.0
IDX_HEADS = 8
IDX_DIM = 64
TOPK_MAX = 256
RWKV_WIDTH = D_MODEL - ATT_WIDTH
RWKV_HEADS = RWKV_WIDTH // HEAD_DIM
W_LORA = 32
A_LORA = 32
G_LORA = 96
SHIFT_WIDTH = 3 * RWKV_WIDTH + W_LORA + A_LORA + G_LORA
PROJ_WIDTH = N_HEADS * HEAD_DIM + 2 * N_KV_HEADS * HEAD_DIM + IDX_HEADS * IDX_DIM + IDX_DIM + IDX_HEADS + SHIFT_WIDTH
D_FF = 4 * D_MODEL
NORM_EPS = 1e-6
GN_EPS = 64e-5
L2_EPS = 1e-12

kernel_name = 'hybrid_dsa_rwkv7_stream_step'


def _split(z, sizes):
    offs = np.cumsum(np.array(sizes))[:-1].tolist()
    return jnp.split(z, offs, axis=-1)


def rms_norm(x, g):
    xf = x.astype(jnp.float32)
    y = xf * lax.rsqrt(jnp.mean(xf * xf, axis=-1, keepdims=True) + NORM_EPS)
    return (y * g.astype(jnp.float32)).astype(x.dtype)


def layer_norm(x, g, b):
    xf = x.astype(jnp.float32)
    mu = jnp.mean(xf, axis=-1, keepdims=True)
    var = jnp.mean(jnp.square(xf - mu), axis=-1, keepdims=True)
    y = (xf - mu) * lax.rsqrt(var + NORM_EPS)
    return (y * g.astype(jnp.float32) + b.astype(jnp.float32)).astype(x.dtype)


def rope_partial(x, pos):
    half = ROT_DIM // 2
    inv = ROPE_THETA ** (-jnp.arange(0, ROT_DIM, 2, dtype=jnp.float32) / ROT_DIM)
    ang = pos.astype(jnp.float32)[:, None] * inv[None, :]
    cos = jnp.cos(ang)[:, None, :]
    sin = jnp.sin(ang)[:, None, :]
    xf = x.astype(jnp.float32)
    x1 = xf[..., :half]
    x2 = xf[..., half:ROT_DIM]
    out = jnp.concatenate([x1 * cos - x2 * sin, x2 * cos + x1 * sin, xf[..., ROT_DIM:]], axis=-1)
    return out.astype(x.dtype)


def dsa_block(q, qi, wi, qpos, k_all, v_all, ki_all, topk):
    B, Tq = q.shape[0], q.shape[1]
    L = k_all.shape[1]
    limit = (qpos // CHUNK + 1) * CHUNK
    admissible = jnp.arange(L)[None, :] < limit[:, None]
    dots = jnp.einsum('bthd,bsd->bths', qi.astype(jnp.float32), ki_all.astype(jnp.float32)) * (IDX_DIM ** -0.5)
    score = jnp.einsum('bths,bth->bts', jax.nn.relu(dots), wi.astype(jnp.float32))
    score = jnp.where(admissible[None], score, -jnp.inf)
    _, idx = lax.top_k(score, topk)
    valid = idx < limit[None, :, None]
    kg = jax.vmap(lambda kb, ib: kb[ib])(k_all, idx)
    vg = jax.vmap(lambda vb, ib: vb[ib])(v_all, idx)
    qg = q.reshape(B, Tq, N_KV_HEADS, GQA_GROUP, HEAD_DIM)
    s = jnp.einsum('btkgd,btskd->btkgs', qg.astype(jnp.float32), kg.astype(jnp.float32)) * (HEAD_DIM ** -0.5)
    s = jnp.where(valid[:, :, None, None, :], s, -jnp.inf)
    p = jax.nn.softmax(s, axis=-1)
    o = jnp.einsum('btkgs,btskd->btkgd', p.astype(vg.dtype), vg)
    return o.reshape(B, Tq, N_HEADS * HEAD_DIM).astype(q.dtype)


def dsa_attention(q, qi, wi, pos, k_all, v_all, ki_all):
    B, T = q.shape[0], q.shape[1]
    L = k_all.shape[1]
    topk = min(TOPK_MAX, L // 4)
    qblk = min(T, CHUNK)
    nblk = T // qblk

    def blocks(a):
        return jnp.swapaxes(a.reshape((B, nblk, qblk) + a.shape[2:]), 0, 1)

    def one_block(args):
        qb, qib, wib, pb = args
        return dsa_block(qb, qib, wib, pb, k_all, v_all, ki_all, topk)

    o = lax.map(one_block, (blocks(q), blocks(qi), blocks(wi), pos.reshape(nblk, qblk)))
    return jnp.swapaxes(o, 0, 1).reshape(B, T, N_HEADS * HEAD_DIM)


def rwkv7_time_mix(zs, shift0, wkv0, mu_shift, w0, w2, a0, a2, g2, k_k, k_a, r_k, ln_x_g, ln_x_b):
    B, T = zs.shape[0], zs.shape[1]
    f32 = jnp.float32
    prev = jnp.concatenate([shift0.astype(zs.dtype), zs[:, :-1]], axis=1)
    zm = zs + mu_shift * (prev - zs)
    r, k, v, zw, za, zg = _split(zm, (RWKV_WIDTH, RWKV_WIDTH, RWKV_WIDTH, W_LORA, A_LORA, G_LORA))
    log_w = -jax.nn.softplus(-(w0 + jnp.tanh(zw) @ w2).astype(f32)) - 0.5
    decay = jnp.exp(-jnp.exp(log_w))
    a = jax.nn.sigmoid((a0 + za @ a2).astype(f32))
    g = (jax.nn.sigmoid(zg) @ g2).astype(f32)

    def heads(t):
        return t.astype(f32).reshape(B, T, RWKV_HEADS, HEAD_DIM)

    r_h, k_h, v_h, a_h, w_h = heads(r), heads(k), heads(v), heads(a), heads(decay)
    kk = k_h * k_k.astype(f32).reshape(RWKV_HEADS, HEAD_DIM)
    kk = kk / jnp.maximum(jnp.sqrt(jnp.sum(kk * kk, axis=-1, keepdims=True)), L2_EPS)
    k_h = k_h * (1.0 + (a_h - 1.0) * k_a.astype(f32).reshape(RWKV_HEADS, HEAD_DIM))

    def step(S, inp):
        r_t, w_t, k_t, v_t, kk_t, b_t = inp
        s_kk = jnp.einsum('bhvk,bhk->bhv', S, kk_t)
        S = S * w_t[:, :, None, :] - s_kk[..., None] * b_t[:, :, None, :] + v_t[..., None] * k_t[:, :, None, :]
        return S, jnp.einsum('bhvk,bhk->bhv', S, r_t)

    def tm(t):
        return jnp.swapaxes(t, 0, 1)

    s_fin, y = lax.scan(step, wkv0.astype(f32), (tm(r_h), tm(w_h), tm(k_h), tm(v_h), tm(kk), tm(kk * a_h)))
    y = tm(y)
    mu = jnp.mean(y, axis=-1, keepdims=True)
    var = jnp.mean(jnp.square(y - mu), axis=-1, keepdims=True)
    y = ((y - mu) * lax.rsqrt(var + GN_EPS)).reshape(B, T, RWKV_WIDTH) * ln_x_g.astype(f32) + ln_x_b.astype(f32)
    bonus = jnp.sum(r_h * k_h * r_k.astype(f32), axis=-1, keepdims=True) * v_h
    o = (y + bonus.reshape(B, T, RWKV_WIDTH)) * g
    return o.astype(zs.dtype), s_fin


def hybrid_layer(x, pos, k_past, v_past, ki_past, wkv0, shift0,
                 norm_mix, w_in, q_gain, k_gain, kidx_ln_g, kidx_ln_b, mu_shift,
                 w0, w2, a0, a2, g2, k_k, k_a, r_k, ln_x_g, ln_x_b, w_out,
                 norm_ffn, w_ff1, w_ff2):
    B, T = x.shape[0], x.shape[1]
    dt = x.dtype
    xn = rms_norm(x, norm_mix)
    zq, zk, zv, ziq, zik, ziw, zs = _split(
        xn @ w_in,
        (N_HEADS * HEAD_DIM, N_KV_HEADS * HEAD_DIM, N_KV_HEADS * HEAD_DIM,
         IDX_HEADS * IDX_DIM, IDX_DIM, IDX_HEADS, SHIFT_WIDTH))
    q = rope_partial(rms_norm(zq.reshape(B, T, N_HEADS, HEAD_DIM), q_gain), pos)
    k = rope_partial(rms_norm(zk.reshape(B, T, N_KV_HEADS, HEAD_DIM), k_gain), pos)
    v = zv.reshape(B, T, N_KV_HEADS, HEAD_DIM)
    qi = rope_partial(ziq.reshape(B, T, IDX_HEADS, IDX_DIM), pos)
    ki = rope_partial(layer_norm(zik, kidx_ln_g, kidx_ln_b)[:, :, None, :], pos)[:, :, 0, :]
    wi = ziw * (IDX_HEADS ** -0.5)
    k_all = jnp.concatenate([k_past.astype(k.dtype), k], axis=1)
    v_all = jnp.concatenate([v_past.astype(v.dtype), v], axis=1)
    ki_all = jnp.concatenate([ki_past.astype(ki.dtype), ki], axis=1)
    o_att = dsa_attention(q, qi, wi, pos, k_all, v_all, ki_all)
    o_rwkv, wkv_new = rwkv7_time_mix(zs, shift0, wkv0, mu_shift, w0, w2, a0, a2, g2,
                                     k_k, k_a, r_k, ln_x_g, ln_x_b)
    h = x + (jnp.concatenate([o_att, o_rwkv.astype(o_att.dtype)], axis=-1) @ w_out).astype(dt)
    hn = rms_norm(h, norm_ffn)
    y = h + (jnp.square(jax.nn.relu(hn @ w_ff1)) @ w_ff2).astype(dt)
    return y, k, v, ki, wkv_new.astype(dt), zs[:, -1:]


def setup_inputs(seed: int = 0) -> dict:
    key = jax.random.key(seed)
    ks = iter(jax.random.split(key, 40))

    def nrm(shape, scale):
        return jax.random.normal(next(ks), shape, jnp.float32) * scale

    L = DEPTH
    return {
        'x_prompt': nrm((BATCH, SEQ, D_MODEL), 1.0),
        'x_sample': nrm((DEC_BATCH, DEC_SEQ, D_MODEL), 1.0),
        'cache_k': nrm((L, DEC_BATCH, PAST_LEN, N_KV_HEADS, HEAD_DIM), 1.0),
        'cache_v': nrm((L, DEC_BATCH, PAST_LEN, N_KV_HEADS, HEAD_DIM), 1.0),
        'cache_kidx': nrm((L, DEC_BATCH, PAST_LEN, IDX_DIM), 1.0),
        'state_wkv': nrm((L, DEC_BATCH, RWKV_HEADS, HEAD_DIM, HEAD_DIM), 0.3),
        'state_shift': nrm((L, DEC_BATCH, 1, SHIFT_WIDTH), 1.0),
        'norm_mix': 1.0 + nrm((L, D_MODEL), 0.02),
        'w_in': nrm((L, D_MODEL, PROJ_WIDTH), D_MODEL ** -0.5),
        'q_gain': 1.0 + nrm((L, HEAD_DIM), 0.02),
        'k_gain': 1.0 + nrm((L, HEAD_DIM), 0.02),
        'kidx_ln_g': 1.0 + nrm((L, IDX_DIM), 0.02),
        'kidx_ln_b': nrm((L, IDX_DIM), 0.02),
        'mu_shift': jax.random.uniform(next(ks), (L, SHIFT_WIDTH), jnp.float32),
        'w0': nrm((L, RWKV_WIDTH), 0.5),
        'w2': nrm((L, W_LORA, RWKV_WIDTH), 0.5 * W_LORA ** -0.5),
        'a0': nrm((L, RWKV_WIDTH), 0.1),
        'a2': nrm((L, A_LORA, RWKV_WIDTH), 0.5 * A_LORA ** -0.5),
        'g2': nrm((L, G_LORA, RWKV_WIDTH), G_LORA ** -0.5),
        'k_k': 0.85 + nrm((L, RWKV_WIDTH), 0.05),
        'k_a': 1.0 + nrm((L, RWKV_WIDTH), 0.05),
        'r_k': nrm((L, RWKV_HEADS, HEAD_DIM), 0.1),
        'ln_x_g': 1.0 + nrm((L, RWKV_WIDTH), 0.02),
        'ln_x_b': nrm((L, RWKV_WIDTH), 0.02),
        'w_out': nrm((L, D_MODEL, D_MODEL), D_MODEL ** -0.5),
        'norm_ffn': 1.0 + nrm((L, D_MODEL), 0.02),
        'w_ff1': nrm((L, D_MODEL, D_FF), D_MODEL ** -0.5),
        'w_ff2': nrm((L, D_FF, D_MODEL), D_FF ** -0.5),
    }


def reference(x_prompt, x_sample, cache_k, cache_v, cache_kidx, state_wkv, state_shift,
              norm_mix, w_in, q_gain, k_gain, kidx_ln_g, kidx_ln_b, mu_shift,
              w0, w2, a0, a2, g2, k_k, k_a, r_k, ln_x_g, ln_x_b, w_out,
              norm_ffn, w_ff1, w_ff2):
    B, T = x_prompt.shape[0], x_prompt.shape[1]
    dt = x_prompt.dtype
    pos_p = jnp.arange(T, dtype=jnp.int32)
    pos_s = cache_k.shape[2] + jnp.arange(x_sample.shape[1], dtype=jnp.int32)
    no_k = jnp.zeros((B, 0, N_KV_HEADS, HEAD_DIM), dt)
    no_ki = jnp.zeros((B, 0, IDX_DIM), dt)
    wkv_zero = jnp.zeros((B, RWKV_HEADS, HEAD_DIM, HEAD_DIM), jnp.float32)
    shift_zero = jnp.zeros((B, 1, SHIFT_WIDTH), dt)

    y_p, y_s = x_prompt, x_sample
    kp, vp, kip, sp, shp = [], [], [], [], []
    ks_, vs_, kis, ss, shs = [], [], [], [], []
    for l in range(DEPTH):
        wl = (norm_mix[l], w_in[l], q_gain[l], k_gain[l], kidx_ln_g[l], kidx_ln_b[l], mu_shift[l],
              w0[l], w2[l], a0[l], a2[l], g2[l], k_k[l], k_a[l], r_k[l], ln_x_g[l], ln_x_b[l],
              w_out[l], norm_ffn[l], w_ff1[l], w_ff2[l])
        y_p, k1, v1, ki1, s1, sh1 = hybrid_layer(y_p, pos_p, no_k, no_k, no_ki, wkv_zero, shift_zero, *wl)
        y_s, k2, v2, ki2, s2, sh2 = hybrid_layer(y_s, pos_s, cache_k[l], cache_v[l], cache_kidx[l],
                                                 state_wkv[l], state_shift[l], *wl)
        kp.append(k1); vp.append(v1); kip.append(ki1); sp.append(s1); shp.append(sh1)
        ks_.append(k2); vs_.append(v2); kis.append(ki2); ss.append(s2); shs.append(sh2)
    return (y_p, y_s,
            jnp.stack(kp), jnp.stack(vp), jnp.stack(kip), jnp.stack(sp), jnp.stack(shp),
            jnp.stack(ks_), jnp.stack(vs_), jnp.stack(kis), jnp.stack(ss), jnp.stack(shs))
```

```python
import functools

import numpy as np
import jax
import jax.numpy as jnp
from jax import lax
from jax.experimental import pallas as pl
from jax.experimental.pallas import tpu as pltpu

F32 = jnp.float32
BF16 = jnp.bfloat16

D_MODEL = 1024
CHUNK = 64
HEAD_DIM = 64
ATT_WIDTH = D_MODEL // 2
N_HEADS = ATT_WIDTH // HEAD_DIM
N_KV_HEADS = N_HEADS // 2
GQA_GROUP = N_HEADS // N_KV_HEADS
KV_WIDTH = N_KV_HEADS * HEAD_DIM
ROT_DIM = HEAD_DIM // 4
ROPE_THETA = 500000.0
IDX_HEADS = 8
IDX_DIM = 64
IDX_WIDTH = IDX_HEADS * IDX_DIM
TOPK_MAX = 256
RWKV_WIDTH = D_MODEL - ATT_WIDTH
RWKV_HEADS = RWKV_WIDTH // HEAD_DIM
W_LORA = 32
A_LORA = 32
G_LORA = 96
LORA_WIDTH = W_LORA + A_LORA + G_LORA
SHIFT_WIDTH = 3 * RWKV_WIDTH + LORA_WIDTH
D_FF = 4 * D_MODEL
NORM_EPS = 1e-6
GN_EPS = 64e-5
L2_EPS = 1e-12

LANES = 128
SHIFT_PAD = 3 * RWKV_WIDTH + 2 * LANES
LORA_PAD = SHIFT_PAD - 3 * RWKV_WIDTH
QKV_WIDTH = ATT_WIDTH + 2 * KV_WIDTH + IDX_WIDTH
INV_BLOCK = 16
INV_SHIFT = 4
CHUNK_SHIFT = 6
MASK_BIAS = -1e30
ROW_TILE = 256
BISECT_ITERS = 32
VMEM_LIMIT = 56 * 1024 * 1024

NN = (((1,), (0,)), ((), ()))
NT = (((1,), (1,)), ((), ()))
TN = (((0,), (0,)), ((), ()))


def _dot(a, b, dims=NN):
    return lax.dot_general(a.astype(BF16), b.astype(BF16), dims, preferred_element_type=F32)


def _group_sum(x, bd):
    w = x.shape[-1]
    return _dot(x, bd[:w, :w])


def _rope(x, cos, sa, sb):
    w = x.shape[-1]
    reps = w // LANES
    if reps > 1:
        cos, sa, sb = (jnp.tile(t, (1, reps)) for t in (cos, sa, sb))
    half = ROT_DIM // 2
    return x * cos + pltpu.roll(x, w - half, 1) * sa + pltpu.roll(x, half, 1) * sb


def _inproj_kernel(x_ref, g_ref, wa_ref, wb_ref, wc_ref, qg_ref, kg_ref, lng_ref, lnb_ref, bd_ref,
                   cos_ref, sa_ref, sb_ref,
                   q_ref, k_ref, v_ref, qi_ref, misc_ref, zs_ref, kbf_ref, vbf_ref, kibf_ref):
    x = x_ref[...]
    ms = jnp.mean(x * x, axis=-1, keepdims=True)
    xn = (x * lax.rsqrt(ms + NORM_EPS) * g_ref[...]).astype(BF16)
    za = jnp.dot(xn, wa_ref[...], preferred_element_type=F32)
    zb = jnp.dot(xn, wb_ref[...], preferred_element_type=F32)
    zs_ref[...] = jnp.dot(xn, wc_ref[...], preferred_element_type=F32)

    cos, sa, sb = cos_ref[...], sa_ref[...], sb_ref[...]
    bd = bd_ref[...]

    zq = za[:, :ATT_WIDTH]
    qn = zq * lax.rsqrt(_group_sum(zq * zq, bd) * (1.0 / HEAD_DIM) + NORM_EPS) * qg_ref[...]
    q_ref[...] = (_rope(qn, cos, sa, sb) * (HEAD_DIM ** -0.5)).astype(BF16)

    zk = za[:, ATT_WIDTH:ATT_WIDTH + KV_WIDTH]
    kn = zk * lax.rsqrt(_group_sum(zk * zk, bd) * (1.0 / HEAD_DIM) + NORM_EPS) * kg_ref[...]
    k = _rope(kn, cos, sa, sb)
    k_ref[...] = k
    kbf_ref[...] = k.astype(BF16)

    v = za[:, ATT_WIDTH + KV_WIDTH:ATT_WIDTH + 2 * KV_WIDTH]
    v_ref[...] = v
    vbf_ref[...] = v.astype(BF16)

    zqi = za[:, ATT_WIDTH + 2 * KV_WIDTH:]
    qi_ref[...] = (_rope(zqi, cos, sa, sb) * (IDX_DIM ** -0.5)).astype(BF16)

    lane = lax.broadcasted_iota(jnp.int32, zb.shape, 1)
    is_ki = lane < IDX_DIM
    mu = jnp.sum(jnp.where(is_ki, zb, 0.0), axis=-1, keepdims=True) * (1.0 / IDX_DIM)
    d = jnp.where(is_ki, zb - mu, 0.0)
    var = jnp.sum(d * d, axis=-1, keepdims=True) * (1.0 / IDX_DIM)
    kin = d * lax.rsqrt(var + NORM_EPS) * lng_ref[...] + lnb_ref[...]
    ki = _rope(kin, cos, sa, sb)
    misc_ref[...] = jnp.where(is_ki, ki, zb * (IDX_HEADS ** -0.5))
    kibf_ref[...] = ki[:, :IDX_DIM].astype(BF16)


def _rope_tables(pos):
    half = ROT_DIM // 2
    inv = ROPE_THETA ** (-jnp.arange(0, ROT_DIM, 2, dtype=F32) / ROT_DIM)
    ang = pos.astype(F32)[:, None] * inv[None, :]
    cos, sin = jnp.cos(ang), jnp.sin(ang)
    n = pos.shape[0]
    ones = jnp.ones((n, HEAD_DIM - ROT_DIM), F32)
    zeros_h = jnp.zeros((n, half), F32)
    zeros_r = jnp.zeros((n, HEAD_DIM - ROT_DIM), F32)
    c64 = jnp.concatenate([cos, cos, ones], axis=1)
    sa64 = jnp.concatenate([-sin, zeros_h, zeros_r], axis=1)
    sb64 = jnp.concatenate([zeros_h, sin, zeros_r], axis=1)
    rep = LANES // HEAD_DIM
    return tuple(jnp.tile(t, (1, rep)) for t in (c64, sa64, sb64))


def _inproj(x2d, seq_len, pos0, p, tm):
    n = x2d.shape[0]
    tab_rows = max(seq_len, tm)
    pos = pos0 + (jnp.arange(tab_rows, dtype=jnp.int32) % seq_len)
    cos, sa, sb = _rope_tables(pos)
    nblk_t = tab_rows // tm

    def row(i):
        return (i, 0)

    def const(i):
        return (0, 0)

    def tab(i):
        return (i % nblk_t, 0)

    full = lambda a: pl.BlockSpec(a.shape, const)
    outs = [
        (ATT_WIDTH, BF16), (KV_WIDTH, F32), (KV_WIDTH, F32), (IDX_WIDTH, BF16), (LANES, F32),
        (SHIFT_PAD, F32), (KV_WIDTH, BF16), (KV_WIDTH, BF16), (IDX_DIM, BF16),
    ]
    params = [p["norm_mix"], p["wa"], p["wb"], p["wc"], p["q_gain"], p["k_gain"], p["ln_g"], p["ln_b"], p["bd"]]
    return pl.pallas_call(
        _inproj_kernel,
        grid=(n // tm,),
        in_specs=[pl.BlockSpec((tm, D_MODEL), row)] + [full(a) for a in params]
        + [pl.BlockSpec((tm, LANES), tab)] * 3,
        out_specs=[pl.BlockSpec((tm, w), row) for w, _ in outs],
        out_shape=[jax.ShapeDtypeStruct((n, w), dt) for w, dt in outs],
        compiler_params=pltpu.CompilerParams(dimension_semantics=("parallel",), vmem_limit_bytes=VMEM_LIMIT),
        name="inproj",
    )(x2d, *params, cos, sa, sb)


def _unit_lower_inverse(low, eye, blk):
    c = low.shape[0]
    d = jnp.where(blk, low, 0.0)
    e = low - d
    d2 = _dot(d, d)
    d4 = _dot(d2, d2)
    d8 = _dot(d4, d4)
    td = _dot(_dot(_dot(eye - d, eye + d2), eye + d4), eye + d8)
    nb = c // INV_BLOCK
    if nb == 1:
        return td
    n = _dot(td, e)
    if nb == 2:
        return _dot(eye - n, td)
    assert nb == 4
    return _dot(_dot(eye - n, eye + _dot(n, n)), td)


def _rwkv_kernel(zs_ref, shift0_ref, wkv0_ref, mu_ref, w0_ref, w2_ref, a0_ref, a2_ref, g2_ref,
                 kk_ref, ka_ref, rk_ref, lng_ref, lnb_ref, bd_ref,
                 o_ref, sout_ref, s_scr, prev_scr, *, chunk):
    c = chunk
    ci = pl.program_id(1)
    w = RWKV_WIDTH

    @pl.when(ci == 0)
    def _():
        s_scr[...] = wkv0_ref[...]
        prev_scr[...] = shift0_ref[...]

    z = zs_ref[...]
    prev_row = prev_scr[...]
    mu = mu_ref[...]

    def mixed(lo, hi):
        zp = z[:, lo:hi]
        rolled = pltpu.roll(zp, 1, 0)
        first = lax.broadcasted_iota(jnp.int32, zp.shape, 0) == 0
        prv = jnp.where(first, prev_row[:, lo:hi], rolled)
        return zp + mu[:, lo:hi] * (prv - zp)

    r = mixed(0, w)
    k = mixed(w, 2 * w)
    v = mixed(2 * w, 3 * w)
    tail = mixed(3 * w, SHIFT_PAD)
    prev_scr[...] = z[c - 1:c, :]

    lw = w0_ref[...] + _dot(jnp.tanh(tail), w2_ref[...])
    u = -lw
    softplus = jnp.maximum(u, 0.0) + jnp.log(1.0 + jnp.exp(-jnp.abs(u)))
    logd = -jnp.exp(-softplus - 0.5)
    a = 1.0 / (1.0 + jnp.exp(-(a0_ref[...] + _dot(tail, a2_ref[...]))))
    gate = _dot(1.0 / (1.0 + jnp.exp(-tail)), g2_ref[...])

    bd = bd_ref[...]
    kk = k * kk_ref[...]
    kk = kk / jnp.maximum(jnp.sqrt(_group_sum(kk * kk, bd)), L2_EPS)
    k = k * (1.0 + (a - 1.0) * ka_ref[...])
    b = kk * a

    row = lax.broadcasted_iota(jnp.int32, (c, c), 0)
    col = lax.broadcasted_iota(jnp.int32, (c, c), 1)
    tri_incl = row >= col
    tri_strict = row > col
    eye = jnp.where(row == col, 1.0, 0.0)
    blk = jnp.right_shift(row, INV_SHIFT) == jnp.right_shift(col, INV_SHIFT)

    tri = jnp.where(tri_incl, 1.0, 0.0).astype(BF16)
    hi = logd.astype(BF16)
    rem = logd - hi.astype(F32)
    mid = rem.astype(BF16)
    lo = (rem - mid.astype(F32)).astype(BF16)
    lg = (jnp.dot(tri, hi, preferred_element_type=F32) + jnp.dot(tri, mid, preferred_element_type=F32)
          + jnp.dot(tri, lo, preferred_element_type=F32))
    g = jnp.exp(lg)
    ginv = jnp.exp(-lg)
    gprev = jnp.exp(lg - logd)

    a_m = gprev * kk
    b_m = b * ginv
    k_m = k * ginv
    p_m = g * r
    bonus_rk = r * k * rk_ref[...]

    ys, bonus = [], []
    for h in range(RWKV_HEADS):
        sl = slice(h * HEAD_DIM, (h + 1) * HEAD_DIM)
        ar = jnp.concatenate([a_m[:, sl], p_m[:, sl]], axis=0).astype(BF16)
        kb = jnp.concatenate([k_m[:, sl], b_m[:, sl]], axis=0).astype(BF16)
        s_h = s_scr[h]
        vh = v[:, sl]
        gm = _dot(ar, kb, NT)
        a_s = _dot(ar, s_h, NT)
        m_ak = jnp.where(tri_strict, gm[:c, :c], 0.0)
        l_ab = jnp.where(tri_strict, gm[:c, c:], 0.0)
        n_pk = jnp.where(tri_incl, gm[c:, :c], 0.0)
        n_pb = jnp.where(tri_incl, gm[c:, c:], 0.0)
        rhs = a_s[:c] + _dot(m_ak, vh)
        u_h = _dot(_unit_lower_inverse(l_ab, eye, blk), rhs)
        y = a_s[c:] + _dot(jnp.concatenate([n_pk, -n_pb], axis=1), jnp.concatenate([vh, u_h], axis=0))
        s_new = (s_h + _dot(jnp.concatenate([vh, -u_h], axis=0), kb, TN)) * g[c - 1:c, sl]
        s_scr[h] = s_new
        m = jnp.mean(y, axis=-1, keepdims=True)
        yc = y - m
        var = jnp.mean(yc * yc, axis=-1, keepdims=True)
        ys.append(yc * lax.rsqrt(var + GN_EPS))
        bonus.append(jnp.sum(bonus_rk[:, sl], axis=-1, keepdims=True) * vh)

    yn = jnp.concatenate(ys, axis=1) * lng_ref[...] + lnb_ref[...]
    o_ref[...] = ((yn + jnp.concatenate(bonus, axis=1)) * gate).astype(o_ref.dtype)

    @pl.when(ci == pl.num_programs(1) - 1)
    def _():
        sout_ref[...] = s_scr[...]


def _rwkv(zs3, shift0, wkv0, p, chunk):
    b, t, _ = zs3.shape
    const = lambda bi, ci: (0, 0)
    full = lambda a: pl.BlockSpec(a.shape, const)
    params = [p["mu"], p["w0"], p["w2"], p["a0"], p["a2"], p["g2"], p["k_k"], p["k_a"], p["r_k"],
              p["lnx_g"], p["lnx_b"], p["bd"]]
    state_spec = pl.BlockSpec((None, RWKV_HEADS, HEAD_DIM, HEAD_DIM), lambda bi, ci: (bi, 0, 0, 0))
    return pl.pallas_call(
        functools.partial(_rwkv_kernel, chunk=chunk),
        grid=(b, t // chunk),
        in_specs=[pl.BlockSpec((None, chunk, SHIFT_PAD), lambda bi, ci: (bi, ci, 0)),
                  pl.BlockSpec((None, 1, SHIFT_PAD), lambda bi, ci: (bi, 0, 0)),
                  state_spec] + [full(a) for a in params],
        out_specs=[pl.BlockSpec((None, chunk, RWKV_WIDTH), lambda bi, ci: (bi, ci, 0)), state_spec],
        out_shape=[jax.ShapeDtypeStruct((b, t, RWKV_WIDTH), BF16),
                   jax.ShapeDtypeStruct((b, RWKV_HEADS, HEAD_DIM, HEAD_DIM), F32)],
        scratch_shapes=[pltpu.VMEM((RWKV_HEADS, HEAD_DIM, HEAD_DIM), F32), pltpu.VMEM((1, SHIFT_PAD), F32)],
        compiler_params=pltpu.CompilerParams(dimension_semantics=("parallel", "arbitrary"),
                                             vmem_limit_bytes=VMEM_LIMIT),
        name="rwkv7",
    )(zs3, shift0, wkv0, *params)


def _attn_kernel(q_ref, qi_ref, misc_ref, k_ref, v_ref, ki_ref, cnt_ref, o_ref,
                 *, tq, n_keys, q_block0, pos0, n_valid, topk):
    qb = pl.program_id(1) + q_block0
    nl = n_keys // LANES

    pos = pos0 + qb * tq + lax.broadcasted_iota(jnp.int32, (tq, LANES), 0)
    limit = jnp.minimum((jnp.right_shift(pos, CHUNK_SHIFT) + 1) * CHUNK, n_valid)
    key_id = lax.broadcasted_iota(jnp.int32, (tq, n_keys), 1)
    adm = key_id < jnp.tile(limit, (1, nl))

    qi = qi_ref[...]
    ki = ki_ref[...]
    misc = misc_ref[...]
    score = None
    for h in range(IDX_HEADS):
        d = lax.dot_general(qi[:, h * IDX_DIM:(h + 1) * IDX_DIM], ki, NT, preferred_element_type=F32)
        term = misc[:, IDX_DIM + h:IDX_DIM + h + 1] * jnp.maximum(d, 0.0)
        score = term if score is None else score + term
    sm = jnp.where(adm, score, -jnp.inf)

    cnt_rhs = cnt_ref[...]
    ones_rhs = jnp.ones((n_keys, LANES), BF16)

    def count_ge(th):
        ge = jnp.where(sm >= jnp.tile(th, (1, nl)), 1.0, 0.0).astype(BF16)
        return jnp.dot(ge, ones_rhs, preferred_element_type=F32)

    row_max = jnp.max(sm, axis=-1, keepdims=True)
    row_min = jnp.min(jnp.where(adm, score, jnp.inf), axis=-1, keepdims=True)
    lo0 = jnp.broadcast_to(row_min, (tq, LANES))
    hi0 = jnp.broadcast_to(2.0 * row_max - row_min + 1.0, (tq, LANES))

    def bisect(_, carry):
        lo, hi = carry
        mid = 0.5 * (lo + hi)
        enough = count_ge(mid) >= topk
        return jnp.where(enough, mid, lo), jnp.where(enough, hi, mid)

    lo, hi = lax.fori_loop(0, BISECT_ITERS, bisect, (lo0, hi0))

    need = topk - count_ge(hi)
    take_all = limit <= topk
    carry = jnp.zeros((tq, LANES), F32)
    bias = []
    for j in range(nl):
        sm_j = sm[:, j * LANES:(j + 1) * LANES]
        above = sm_j >= hi
        group = jnp.logical_and(sm_j >= lo, jnp.logical_not(above))
        pc = jnp.dot(jnp.where(group, 1.0, 0.0).astype(BF16), cnt_rhs, preferred_element_type=F32)
        fill = carry + pc[:, :LANES] <= need
        carry = carry + pc[:, LANES:]
        picked = jnp.logical_or(above, jnp.logical_and(group, fill))
        sel = jnp.logical_and(sm_j > -jnp.inf, jnp.logical_or(take_all, picked))
        bias.append(jnp.where(sel, 0.0, MASK_BIAS))
    bias = jnp.concatenate(bias, axis=1)
    bias = jnp.concatenate([bias] * GQA_GROUP, axis=0)

    q = q_ref[...]
    kx = k_ref[...]
    vx = v_ref[...]
    outs = []
    for gi in range(N_KV_HEADS):
        kg = kx[:, gi * HEAD_DIM:(gi + 1) * HEAD_DIM]
        vg = vx[:, gi * HEAD_DIM:(gi + 1) * HEAD_DIM]
        qs = jnp.concatenate([q[:, (gi * GQA_GROUP + j) * HEAD_DIM:(gi * GQA_GROUP + j + 1) * HEAD_DIM]
                              for j in range(GQA_GROUP)], axis=0)
        s = lax.dot_general(qs, kg, NT, preferred_element_type=F32) + bias
        m = jnp.max(s, axis=-1, keepdims=True)
        pexp = jnp.exp(s - m)
        den = jnp.sum(pexp, axis=-1, keepdims=True)
        o = jnp.dot(pexp.astype(BF16), vg, preferred_element_type=F32) / den
        outs.extend(o[j * tq:(j + 1) * tq] for j in range(GQA_GROUP))
    o_ref[...] = jnp.concatenate(outs, axis=1).astype(o_ref.dtype)


def _attention(q3, qi3, misc3, k3, v3, ki3, cnt_rhs, *, tq, n_keys, q_block0, n_qblocks, pos0, n_valid, topk):
    b = q3.shape[0]
    qmap = lambda bi, qi_: (bi, qi_ + q_block0, 0)
    kmap = lambda bi, qi_: (bi, 0, 0)
    return pl.pallas_call(
        functools.partial(_attn_kernel, tq=tq, n_keys=n_keys, q_block0=q_block0, pos0=pos0,
                          n_valid=n_valid, topk=topk),
        grid=(b, n_qblocks),
        in_specs=[pl.BlockSpec((None, tq, ATT_WIDTH), qmap),
                  pl.BlockSpec((None, tq, IDX_WIDTH), qmap),
                  pl.BlockSpec((None, tq, LANES), qmap),
                  pl.BlockSpec((None, n_keys, KV_WIDTH), kmap),
                  pl.BlockSpec((None, n_keys, KV_WIDTH), kmap),
                  pl.BlockSpec((None, n_keys, IDX_DIM), kmap),
                  pl.BlockSpec(cnt_rhs.shape, lambda bi, qi_: (0, 0))],
        out_specs=pl.BlockSpec((None, tq, ATT_WIDTH), lambda bi, qi_: (bi, qi_, 0)),
        out_shape=jax.ShapeDtypeStruct((b, n_qblocks * tq, ATT_WIDTH), BF16),
        compiler_params=pltpu.CompilerParams(dimension_semantics=("parallel", "arbitrary"),
                                             vmem_limit_bytes=VMEM_LIMIT),
        name=f"dsa_attn_{n_keys}",
    )(q3, qi3, misc3, k3, v3, ki3, cnt_rhs)


def _outffn_kernel(x_ref, oa_ref, ob_ref, wo_a_ref, wo_b_ref, g_ref, w1_ref, w2_ref, y_ref, *, ff_chunk):
    mix = (jnp.dot(oa_ref[...], wo_a_ref[...], preferred_element_type=F32)
           + jnp.dot(ob_ref[...], wo_b_ref[...], preferred_element_type=F32))
    h = x_ref[...] + mix
    ms = jnp.mean(h * h, axis=-1, keepdims=True)
    hn = (h * lax.rsqrt(ms + NORM_EPS) * g_ref[...]).astype(BF16)
    acc = h
    for j in range(D_FF // ff_chunk):
        act = jnp.dot(hn, w1_ref[:, j * ff_chunk:(j + 1) * ff_chunk], preferred_element_type=F32)
        act = jnp.square(jnp.maximum(act, 0.0)).astype(BF16)
        acc = acc + jnp.dot(act, w2_ref[j * ff_chunk:(j + 1) * ff_chunk, :], preferred_element_type=F32)
    y_ref[...] = acc


def _outffn(x2d, o_att, o_rwkv, p, tm):
    n = x2d.shape[0]
    row = lambda i: (i, 0)
    const = lambda i: (0, 0)
    params = [p["wo_a"], p["wo_b"], p["norm_ffn"], p["w_ff1"], p["w_ff2"]]
    return pl.pallas_call(
        functools.partial(_outffn_kernel, ff_chunk=1024),
        grid=(n // tm,),
        in_specs=[pl.BlockSpec((tm, D_MODEL), row), pl.BlockSpec((tm, ATT_WIDTH), row),
                  pl.BlockSpec((tm, RWKV_WIDTH), row)]
        + [pl.BlockSpec(a.shape, const, pipeline_mode=pl.Buffered(1)) for a in params],
        out_specs=pl.BlockSpec((tm, D_MODEL), row),
        out_shape=jax.ShapeDtypeStruct((n, D_MODEL), F32),
        compiler_params=pltpu.CompilerParams(dimension_semantics=("parallel",), vmem_limit_bytes=VMEM_LIMIT),
        name="outproj_ffn",
    )(x2d, o_att, o_rwkv, *params)


def _block_diag_ones():
    i = np.arange(ATT_WIDTH)
    return jnp.asarray((i[:, None] // HEAD_DIM) == (i[None, :] // HEAD_DIM), BF16)


def _count_rhs():
    i = np.arange(LANES)
    tri = (i[:, None] <= i[None, :]).astype(np.float32)
    return jnp.asarray(np.concatenate([tri, np.ones((LANES, LANES), np.float32)], axis=1), BF16)


def _prep_params(wl):
    (norm_mix, w_in, q_gain, k_gain, kidx_ln_g, kidx_ln_b, mu_shift, w0, w2, a0, a2, g2, k_k, k_a, r_k,
     ln_x_g, ln_x_b, w_out, norm_ffn, w_ff1, w_ff2) = wl
    row = lambda a: a.reshape(1, -1).astype(F32)
    o_ki = QKV_WIDTH
    o_wi = o_ki + IDX_DIM
    o_zs = o_wi + IDX_HEADS
    pad_cols = lambda a, wdt: jnp.pad(a, ((0, 0), (0, wdt - a.shape[1])))
    lora_rows = lambda a, off: jnp.pad(a, ((off, LORA_PAD - off - a.shape[0]), (0, 0))).astype(BF16)
    return {
        "norm_mix": row(norm_mix),
        "wa": w_in[:, :QKV_WIDTH].astype(BF16),
        "wb": pad_cols(w_in[:, o_ki:o_zs], LANES).astype(BF16),
        "wc": pad_cols(w_in[:, o_zs:], SHIFT_PAD).astype(BF16),
        "q_gain": row(jnp.tile(q_gain, N_HEADS)),
        "k_gain": row(jnp.tile(k_gain, N_KV_HEADS)),
        "ln_g": pad_cols(row(kidx_ln_g), LANES),
        "ln_b": pad_cols(row(kidx_ln_b), LANES),
        "bd": _block_diag_ones(),
        "mu": pad_cols(row(mu_shift), SHIFT_PAD),
        "w0": row(w0), "a0": row(a0),
        "w2": lora_rows(w2, 0), "a2": lora_rows(a2, W_LORA), "g2": lora_rows(g2, W_LORA + A_LORA),
        "k_k": row(k_k), "k_a": row(k_a), "r_k": row(r_k), "lnx_g": row(ln_x_g), "lnx_b": row(ln_x_b),
        "wo_a": w_out[:ATT_WIDTH].astype(BF16), "wo_b": w_out[ATT_WIDTH:].astype(BF16),
        "norm_ffn": row(norm_ffn),
        "w_ff1": w_ff1.astype(BF16), "w_ff2": w_ff2.astype(BF16),
    }


def _layer(x, pos0, k_past, v_past, ki_past, wkv0, shift0, p, cnt_rhs, *, chunk, tq, key_bucket):
    b, t, _ = x.shape
    n = b * t
    past = k_past.shape[1]
    x2d = x.reshape(n, D_MODEL)
    tm = min(ROW_TILE, n)
    q, k, v, qi, misc, zs, kbf, vbf, kibf = _inproj(x2d, t, pos0, p, tm)

    shift0p = jnp.pad(shift0, ((0, 0), (0, 0), (0, SHIFT_PAD - SHIFT_WIDTH)))
    zs3 = zs.reshape(b, t, SHIFT_PAD)
    o_rwkv, wkv_new = _rwkv(zs3, shift0p, wkv0, p, chunk)

    n_valid = past + t
    topk = min(TOPK_MAX, n_valid // 4)
    three = lambda a: a.reshape(b, t, a.shape[-1])
    k_new, v_new, ki_new = three(kbf), three(vbf), three(kibf)
    if past:
        n_keys = -(-n_valid // LANES) * LANES
        cat = lambda old, new: jnp.pad(
            jnp.concatenate([old.reshape(b, past, -1).astype(BF16), new], axis=1),
            ((0, 0), (0, n_keys - n_valid), (0, 0)))
        k_all, v_all, ki_all = cat(k_past, k_new), cat(v_past, v_new), cat(ki_past, ki_new)
    else:
        k_all, v_all, ki_all = k_new, v_new, ki_new
    q3, qi3, misc3 = three(q), three(qi), three(misc)
    if past:
        o_att = _attention(q3, qi3, misc3, k_all, v_all, ki_all, cnt_rhs, tq=tq, n_keys=k_all.shape[1],
                           q_block0=0, n_qblocks=t // tq, pos0=pos0, n_valid=n_valid, topk=topk)
    else:
        parts = []
        per = key_bucket // tq
        for j in range(t // key_bucket):
            parts.append(_attention(q3, qi3, misc3, k_all, v_all, ki_all, cnt_rhs, tq=tq,
                                    n_keys=(j + 1) * key_bucket, q_block0=j * per, n_qblocks=per,
                                    pos0=pos0, n_valid=n_valid, topk=topk))
        o_att = jnp.concatenate(parts, axis=1)

    y = _outffn(x2d, o_att.reshape(n, ATT_WIDTH), o_rwkv.reshape(n, RWKV_WIDTH), p, tm)
    return (y.reshape(b, t, D_MODEL), k.reshape(b, t, N_KV_HEADS, HEAD_DIM),
            v.reshape(b, t, N_KV_HEADS, HEAD_DIM), misc3[:, :, :IDX_DIM], wkv_new,
            zs3[:, t - 1:, :SHIFT_WIDTH])


def kernel(x_prompt, x_sample, cache_k, cache_v, cache_kidx, state_wkv, state_shift, norm_mix, w_in, q_gain, k_gain, kidx_ln_g, kidx_ln_b, mu_shift, w0, w2, a0, a2, g2, k_k, k_a, r_k, ln_x_g, ln_x_b, w_out, norm_ffn, w_ff1, w_ff2):
    depth = norm_mix.shape[0]
    bp, tp = x_prompt.shape[0], x_prompt.shape[1]
    bs, ts = x_sample.shape[0], x_sample.shape[1]
    past_len = cache_k.shape[2]
    cnt_rhs = _count_rhs()
    no_kv = jnp.zeros((bp, 0, KV_WIDTH), F32)
    no_ki = jnp.zeros((bp, 0, IDX_DIM), F32)
    wkv_zero = jnp.zeros((bp, RWKV_HEADS, HEAD_DIM, HEAD_DIM), F32)
    shift_zero = jnp.zeros((bp, 1, SHIFT_WIDTH), F32)

    y_p, y_s = x_prompt, x_sample
    outs_p, outs_s = [], []
    for l in range(depth):
        wl = (norm_mix[l], w_in[l], q_gain[l], k_gain[l], kidx_ln_g[l], kidx_ln_b[l], mu_shift[l],
              w0[l], w2[l], a0[l], a2[l], g2[l], k_k[l], k_a[l], r_k[l], ln_x_g[l], ln_x_b[l],
              w_out[l], norm_ffn[l], w_ff1[l], w_ff2[l])
        p = _prep_params(wl)
        y_p, *rest_p = _layer(y_p, 0, no_kv, no_kv, no_ki, wkv_zero, shift_zero, p, cnt_rhs,
                              chunk=64, tq=128, key_bucket=512)
        y_s, *rest_s = _layer(y_s, past_len, cache_k[l], cache_v[l], cache_kidx[l], state_wkv[l],
                              state_shift[l], p, cnt_rhs, chunk=ts, tq=ts, key_bucket=None)
        outs_p.append(rest_p)
        outs_s.append(rest_s)
    stack = lambda outs, i: jnp.stack([o[i] for o in outs])
    return (y_p, y_s,
            *(stack(outs_p, i) for i in range(5)),
            *(stack(outs_s, i) for i in range(5)))
```

```python
import functools

import numpy as np
import jax
import jax.numpy as jnp
from jax import lax
from jax.experimental import pallas as pl
from jax.experimental.pallas import tpu as pltpu

F32 = jnp.float32
BF16 = jnp.bfloat16

D_MODEL = 1024
CHUNK = 64
HEAD_DIM = 64
ATT_WIDTH = D_MODEL // 2
N_HEADS = ATT_WIDTH // HEAD_DIM
N_KV_HEADS = N_HEADS // 2
GQA_GROUP = N_HEADS // N_KV_HEADS
KV_WIDTH = N_KV_HEADS * HEAD_DIM
ROT_DIM = HEAD_DIM // 4
ROPE_THETA = 500000.0
IDX_HEADS = 8
IDX_DIM = 64
IDX_WIDTH = IDX_HEADS * IDX_DIM
TOPK_MAX = 256
RWKV_WIDTH = D_MODEL - ATT_WIDTH
RWKV_HEADS = RWKV_WIDTH // HEAD_DIM
W_LORA = 32
A_LORA = 32
G_LORA = 96
LORA_WIDTH = W_LORA + A_LORA + G_LORA
SHIFT_WIDTH = 3 * RWKV_WIDTH + LORA_WIDTH
D_FF = 4 * D_MODEL
NORM_EPS = 1e-6
GN_EPS = 64e-5
L2_EPS = 1e-12

LANES = 128
SHIFT_PAD = 3 * RWKV_WIDTH + 2 * LANES
LORA_PAD = SHIFT_PAD - 3 * RWKV_WIDTH
QKV_WIDTH = ATT_WIDTH + 2 * KV_WIDTH + IDX_WIDTH
INV_BLOCK = 16
INV_SHIFT = 4
CHUNK_SHIFT = 6
MASK_BIAS = -1e30
ROW_TILE = 256
BISECT_ITERS = 32
VMEM_LIMIT = 56 * 1024 * 1024

NN = (((1,), (0,)), ((), ()))
NT = (((1,), (1,)), ((), ()))
TN = (((0,), (0,)), ((), ()))


def _dot(a, b, dims=NN):
    return lax.dot_general(a.astype(BF16), b.astype(BF16), dims, preferred_element_type=F32)


def _group_sum(x, bd):
    w = x.shape[-1]
    return _dot(x, bd[:w, :w])


def _rope(x, cos, sa, sb):
    w = x.shape[-1]
    reps = w // LANES
    if reps > 1:
        cos, sa, sb = (jnp.tile(t, (1, reps)) for t in (cos, sa, sb))
    half = ROT_DIM // 2
    return x * cos + pltpu.roll(x, w - half, 1) * sa + pltpu.roll(x, half, 1) * sb


def _inproj_kernel(x_ref, g_ref, wa_ref, wb_ref, wc_ref, qg_ref, kg_ref, lng_ref, lnb_ref, bd_ref,
                   cos_ref, sa_ref, sb_ref,
                   q_ref, k_ref, v_ref, qi_ref, misc_ref, zs_ref, kbf_ref, vbf_ref, kibf_ref):
    x = x_ref[...]
    ms = jnp.mean(x * x, axis=-1, keepdims=True)
    xn = (x * lax.rsqrt(ms + NORM_EPS) * g_ref[...]).astype(BF16)
    za = jnp.dot(xn, wa_ref[...], preferred_element_type=F32)
    zb = jnp.dot(xn, wb_ref[...], preferred_element_type=F32)
    zs_ref[...] = jnp.dot(xn, wc_ref[...], preferred_element_type=F32)

    cos, sa, sb = cos_ref[...], sa_ref[...], sb_ref[...]
    bd = bd_ref[...]

    zq = za[:, :ATT_WIDTH]
    qn = zq * lax.rsqrt(_group_sum(zq * zq, bd) * (1.0 / HEAD_DIM) + NORM_EPS) * qg_ref[...]
    q_ref[...] = (_rope(qn, cos, sa, sb) * (HEAD_DIM ** -0.5)).astype(BF16)

    zk = za[:, ATT_WIDTH:ATT_WIDTH + KV_WIDTH]
    kn = zk * lax.rsqrt(_group_sum(zk * zk, bd) * (1.0 / HEAD_DIM) + NORM_EPS) * kg_ref[...]
    k = _rope(kn, cos, sa, sb)
    k_ref[...] = k
    kbf_ref[...] = k.astype(BF16)

    v = za[:, ATT_WIDTH + KV_WIDTH:ATT_WIDTH + 2 * KV_WIDTH]
    v_ref[...] = v
    vbf_ref[...] = v.astype(BF16)

    zqi = za[:, ATT_WIDTH + 2 * KV_WIDTH:]
    qi_ref[...] = (_rope(zqi, cos, sa, sb) * (IDX_DIM ** -0.5)).astype(BF16)

    lane = lax.broadcasted_iota(jnp.int32, zb.shape, 1)
    is_ki = lane < IDX_DIM
    mu = jnp.sum(jnp.where(is_ki, zb, 0.0), axis=-1, keepdims=True) * (1.0 / IDX_DIM)
    d = jnp.where(is_ki, zb - mu, 0.0)
    var = jnp.sum(d * d, axis=-1, keepdims=True) * (1.0 / IDX_DIM)
    kin = d * lax.rsqrt(var + NORM_EPS) * lng_ref[...] + lnb_ref[...]
    ki = _rope(kin, cos, sa, sb)
    misc_ref[...] = jnp.where(is_ki, ki, zb * (IDX_HEADS ** -0.5))
    kibf_ref[...] = ki[:, :IDX_DIM].astype(BF16)


def _rope_tables(pos):
    half = ROT_DIM // 2
    inv = ROPE_THETA ** (-jnp.arange(0, ROT_DIM, 2, dtype=F32) / ROT_DIM)
    ang = pos.astype(F32)[:, None] * inv[None, :]
    cos, sin = jnp.cos(ang), jnp.sin(ang)
    n = pos.shape[0]
    ones = jnp.ones((n, HEAD_DIM - ROT_DIM), F32)
    zeros_h = jnp.zeros((n, half), F32)
    zeros_r = jnp.zeros((n, HEAD_DIM - ROT_DIM), F32)
    c64 = jnp.concatenate([cos, cos, ones], axis=1)
    sa64 = jnp.concatenate([-sin, zeros_h, zeros_r], axis=1)
    sb64 = jnp.concatenate([zeros_h, sin, zeros_r], axis=1)
    rep = LANES // HEAD_DIM
    return tuple(jnp.tile(t, (1, rep)) for t in (c64, sa64, sb64))


def _inproj(x2d, seq_len, pos0, p, tm):
    n = x2d.shape[0]
    tab_rows = max(seq_len, tm)
    pos = pos0 + (jnp.arange(tab_rows, dtype=jnp.int32) % seq_len)
    cos, sa, sb = _rope_tables(pos)
    nblk_t = tab_rows // tm

    def row(i):
        return (i, 0)

    def const(i):
        return (0, 0)

    def tab(i):
        return (i % nblk_t, 0)

    full = lambda a: pl.BlockSpec(a.shape, const)
    outs = [
        (ATT_WIDTH, BF16), (KV_WIDTH, F32), (KV_WIDTH, F32), (IDX_WIDTH, BF16), (LANES, F32),
        (SHIFT_PAD, F32), (KV_WIDTH, BF16), (KV_WIDTH, BF16), (IDX_DIM, BF16),
    ]
    params = [p["norm_mix"], p["wa"], p["wb"], p["wc"], p["q_gain"], p["k_gain"], p["ln_g"], p["ln_b"], p["bd"]]
    return pl.pallas_call(
        _inproj_kernel,
        grid=(n // tm,),
        in_specs=[pl.BlockSpec((tm, D_MODEL), row)] + [full(a) for a in params]
        + [pl.BlockSpec((tm, LANES), tab)] * 3,
        out_specs=[pl.BlockSpec((tm, w), row) for w, _ in outs],
        out_shape=[jax.ShapeDtypeStruct((n, w), dt) for w, dt in outs],
        compiler_params=pltpu.CompilerParams(dimension_semantics=("parallel",), vmem_limit_bytes=VMEM_LIMIT),
        name="inproj",
    )(x2d, *params, cos, sa, sb)


def _solve_unit_lower(lows, rhss, eye, blk):
    c = lows[0].shape[0]
    d = [jnp.where(blk, l, 0.0) for l in lows]
    e = [l - dh for l, dh in zip(lows, d)]
    dk = [_dot(dh, dh) for dh in d]
    p = [_dot(eye - dh, eye + d2) for dh, d2 in zip(d, dk)]
    span = 4
    while span < INV_BLOCK:
        dk = [_dot(m, m) for m in dk]
        p = [_dot(ph, eye + m) for ph, m in zip(p, dk)]
        span *= 2
    x = [_dot(ph, r) for ph, r in zip(p, rhss)]
    nb = c // INV_BLOCK
    if nb > 1:
        nk = [_dot(ph, eh) for ph, eh in zip(p, e)]
        x = [_dot(eye - m, xh) for m, xh in zip(nk, x)]
        order = 2
        while order < nb:
            nk = [_dot(m, m) for m in nk]
            x = [_dot(eye + m, xh) for m, xh in zip(nk, x)]
            order *= 2
    return x


def _rwkv_kernel(zs_ref, shift0_ref, wkv0_ref, mu_ref, w0_ref, w2_ref, a0_ref, a2_ref, g2_ref,
                 kk_ref, ka_ref, rk_ref, lng_ref, lnb_ref, bd_ref,
                 o_ref, sout_ref, s_scr, prev_scr, *, chunk):
    c = chunk
    ci = pl.program_id(1)
    w = RWKV_WIDTH

    @pl.when(ci == 0)
    def _():
        s_scr[...] = wkv0_ref[...]
        prev_scr[...] = shift0_ref[...]

    z = zs_ref[...]
    prev_row = prev_scr[...]
    mu = mu_ref[...]

    def mixed(lo, hi):
        zp = z[:, lo:hi]
        rolled = pltpu.roll(zp, 1, 0)
        first = lax.broadcasted_iota(jnp.int32, zp.shape, 0) == 0
        prv = jnp.where(first, prev_row[:, lo:hi], rolled)
        return zp + mu[:, lo:hi] * (prv - zp)

    r = mixed(0, w)
    k = mixed(w, 2 * w)
    v = mixed(2 * w, 3 * w)
    tail = mixed(3 * w, SHIFT_PAD)
    prev_scr[...] = z[c - 1:c, :]

    lw = w0_ref[...] + _dot(jnp.tanh(tail), w2_ref[...])
    nlw = -lw
    softplus = jnp.maximum(nlw, 0.0) + jnp.log(1.0 + jnp.exp(-jnp.abs(nlw)))
    logd = -jnp.exp(-softplus - 0.5)
    a = 1.0 / (1.0 + jnp.exp(-(a0_ref[...] + _dot(tail, a2_ref[...]))))
    gate = _dot(1.0 / (1.0 + jnp.exp(-tail)), g2_ref[...])

    bd = bd_ref[...]
    kk = k * kk_ref[...]
    kk = kk / jnp.maximum(jnp.sqrt(_group_sum(kk * kk, bd)), L2_EPS)
    k = k * (1.0 + (a - 1.0) * ka_ref[...])
    b = kk * a

    row = lax.broadcasted_iota(jnp.int32, (c, c), 0)
    col = lax.broadcasted_iota(jnp.int32, (c, c), 1)
    tri_incl = row >= col
    tri_strict = row > col
    eye = jnp.where(row == col, 1.0, 0.0)
    blk = jnp.right_shift(row, INV_SHIFT) == jnp.right_shift(col, INV_SHIFT)

    tri = jnp.where(tri_incl, 1.0, 0.0).astype(BF16)
    hi = logd.astype(BF16)
    rem = logd - hi.astype(F32)
    mid = rem.astype(BF16)
    lo = (rem - mid.astype(F32)).astype(BF16)
    lg = (jnp.dot(tri, hi, preferred_element_type=F32) + jnp.dot(tri, mid, preferred_element_type=F32)
          + jnp.dot(tri, lo, preferred_element_type=F32))
    g = jnp.exp(lg)
    ginv = jnp.exp(-lg)
    gprev = jnp.exp(lg - logd)

    a_m = gprev * kk
    b_m = b * ginv
    k_m = k * ginv
    p_m = g * r
    bonus_rk = r * k * rk_ref[...]

    heads = range(RWKV_HEADS)
    sls = [slice(h * HEAD_DIM, (h + 1) * HEAD_DIM) for h in heads]
    ar = [jnp.concatenate([a_m[:, sl], p_m[:, sl]], axis=0).astype(BF16) for sl in sls]
    kb = [jnp.concatenate([k_m[:, sl], b_m[:, sl]], axis=0).astype(BF16) for sl in sls]
    vh = [v[:, sl] for sl in sls]
    s_h = [s_scr[h] for h in heads]
    gm = [_dot(ar[h], kb[h], NT) for h in heads]
    a_s = [_dot(ar[h], s_h[h], NT) for h in heads]
    rhs = [a_s[h][:c] + _dot(jnp.where(tri_strict, gm[h][:c, :c], 0.0), vh[h]) for h in heads]
    u = _solve_unit_lower([jnp.where(tri_strict, gm[h][:c, c:], 0.0) for h in heads], rhs, eye, blk)
    ys, bonus = [], []
    for h in heads:
        n_pk = jnp.where(tri_incl, gm[h][c:, :c], 0.0)
        n_pb = jnp.where(tri_incl, gm[h][c:, c:], 0.0)
        y = a_s[h][c:] + _dot(jnp.concatenate([n_pk, -n_pb], axis=1), jnp.concatenate([vh[h], u[h]], axis=0))
        s_scr[h] = (s_h[h] + _dot(jnp.concatenate([vh[h], -u[h]], axis=0), kb[h], TN)) * g[c - 1:c, sls[h]]
        m = jnp.mean(y, axis=-1, keepdims=True)
        yc = y - m
        var = jnp.mean(yc * yc, axis=-1, keepdims=True)
        ys.append(yc * lax.rsqrt(var + GN_EPS))
        bonus.append(jnp.sum(bonus_rk[:, sls[h]], axis=-1, keepdims=True) * vh[h])

    yn = jnp.concatenate(ys, axis=1) * lng_ref[...] + lnb_ref[...]
    o_ref[...] = ((yn + jnp.concatenate(bonus, axis=1)) * gate).astype(o_ref.dtype)

    @pl.when(ci == pl.num_programs(1) - 1)
    def _():
        sout_ref[...] = s_scr[...]


def _rwkv(zs3, shift0, wkv0, p, chunk):
    b, t, _ = zs3.shape
    const = lambda bi, ci: (0, 0)
    full = lambda a: pl.BlockSpec(a.shape, const)
    params = [p["mu"], p["w0"], p["w2"], p["a0"], p["a2"], p["g2"], p["k_k"], p["k_a"], p["r_k"],
              p["lnx_g"], p["lnx_b"], p["bd"]]
    state_spec = pl.BlockSpec((None, RWKV_HEADS, HEAD_DIM, HEAD_DIM), lambda bi, ci: (bi, 0, 0, 0))
    return pl.pallas_call(
        functools.partial(_rwkv_kernel, chunk=chunk),
        grid=(b, t // chunk),
        in_specs=[pl.BlockSpec((None, chunk, SHIFT_PAD), lambda bi, ci: (bi, ci, 0)),
                  pl.BlockSpec((None, 1, SHIFT_PAD), lambda bi, ci: (bi, 0, 0)),
                  state_spec] + [full(a) for a in params],
        out_specs=[pl.BlockSpec((None, chunk, RWKV_WIDTH), lambda bi, ci: (bi, ci, 0)), state_spec],
        out_shape=[jax.ShapeDtypeStruct((b, t, RWKV_WIDTH), BF16),
                   jax.ShapeDtypeStruct((b, RWKV_HEADS, HEAD_DIM, HEAD_DIM), F32)],
        scratch_shapes=[pltpu.VMEM((RWKV_HEADS, HEAD_DIM, HEAD_DIM), F32), pltpu.VMEM((1, SHIFT_PAD), F32)],
        compiler_params=pltpu.CompilerParams(dimension_semantics=("parallel", "arbitrary"),
                                             vmem_limit_bytes=VMEM_LIMIT),
        name="rwkv7",
    )(zs3, shift0, wkv0, *params)


def _attn_kernel(q_ref, qi_ref, misc_ref, k_ref, v_ref, ki_ref, cnt_ref, o_ref,
                 *, tq, n_keys, q_block0, pos0, n_valid, topk):
    qb = pl.program_id(1) + q_block0
    nl = n_keys // LANES

    pos = pos0 + qb * tq + lax.broadcasted_iota(jnp.int32, (tq, LANES), 0)
    limit = jnp.minimum((jnp.right_shift(pos, CHUNK_SHIFT) + 1) * CHUNK, n_valid)
    key_id = lax.broadcasted_iota(jnp.int32, (tq, n_keys), 1)
    adm = key_id < jnp.tile(limit, (1, nl))

    qi = qi_ref[...]
    ki = ki_ref[...]
    misc = misc_ref[...]
    score = None
    for h in range(IDX_HEADS):
        d = lax.dot_general(qi[:, h * IDX_DIM:(h + 1) * IDX_DIM], ki, NT, preferred_element_type=F32)
        term = misc[:, IDX_DIM + h:IDX_DIM + h + 1] * jnp.maximum(d, 0.0)
        score = term if score is None else score + term
    sm = jnp.where(adm, score, -jnp.inf)

    cnt_rhs = cnt_ref[...]
    ones_rhs = jnp.ones((n_keys, LANES), BF16)

    def count_ge(th):
        ge = jnp.where(sm >= jnp.tile(th, (1, nl)), 1.0, 0.0).astype(BF16)
        return jnp.dot(ge, ones_rhs, preferred_element_type=F32)

    row_max = jnp.max(sm, axis=-1, keepdims=True)
    row_min = jnp.min(jnp.where(adm, score, jnp.inf), axis=-1, keepdims=True)
    lo0 = jnp.broadcast_to(row_min, (tq, LANES))
    hi0 = jnp.broadcast_to(2.0 * row_max - row_min + 1.0, (tq, LANES))

    def bisect(_, carry):
        lo, hi = carry
        mid = 0.5 * (lo + hi)
        enough = count_ge(mid) >= topk
        return jnp.where(enough, mid, lo), jnp.where(enough, hi, mid)

    lo, hi = lax.fori_loop(0, BISECT_ITERS, bisect, (lo0, hi0))

    need = topk - count_ge(hi)
    take_all = limit <= topk
    carry = jnp.zeros((tq, LANES), F32)
    bias = []
    for j in range(nl):
        sm_j = sm[:, j * LANES:(j + 1) * LANES]
        above = sm_j >= hi
        group = jnp.logical_and(sm_j >= lo, jnp.logical_not(above))
        pc = jnp.dot(jnp.where(group, 1.0, 0.0).astype(BF16), cnt_rhs, preferred_element_type=F32)
        fill = carry + pc[:, :LANES] <= need
        carry = carry + pc[:, LANES:]
        picked = jnp.logical_or(above, jnp.logical_and(group, fill))
        sel = jnp.logical_and(sm_j > -jnp.inf, jnp.logical_or(take_all, picked))
        bias.append(jnp.where(sel, 0.0, MASK_BIAS))
    bias = jnp.concatenate(bias, axis=1)
    bias = jnp.concatenate([bias] * GQA_GROUP, axis=0)

    q = q_ref[...]
    kx = k_ref[...]
    vx = v_ref[...]
    outs = []
    for gi in range(N_KV_HEADS):
        kg = kx[:, gi * HEAD_DIM:(gi + 1) * HEAD_DIM]
        vg = vx[:, gi * HEAD_DIM:(gi + 1) * HEAD_DIM]
        qs = jnp.concatenate([q[:, (gi * GQA_GROUP + j) * HEAD_DIM:(gi * GQA_GROUP + j + 1) * HEAD_DIM]
                              for j in range(GQA_GROUP)], axis=0)
        s = lax.dot_general(qs, kg, NT, preferred_element_type=F32) + bias
        m = jnp.max(s, axis=-1, keepdims=True)
        pexp = jnp.exp(s - m)
        den = jnp.sum(pexp, axis=-1, keepdims=True)
        o = jnp.dot(pexp.astype(BF16), vg, preferred_element_type=F32) / den
        outs.extend(o[j * tq:(j + 1) * tq] for j in range(GQA_GROUP))
    o_ref[...] = jnp.concatenate(outs, axis=1).astype(o_ref.dtype)


def _attention(q3, qi3, misc3, k3, v3, ki3, cnt_rhs, *, tq, n_keys, q_block0, n_qblocks, pos0, n_valid, topk):
    b = q3.shape[0]
    qmap = lambda bi, qi_: (bi, qi_ + q_block0, 0)
    kmap = lambda bi, qi_: (bi, 0, 0)
    return pl.pallas_call(
        functools.partial(_attn_kernel, tq=tq, n_keys=n_keys, q_block0=q_block0, pos0=pos0,
                          n_valid=n_valid, topk=topk),
        grid=(b, n_qblocks),
        in_specs=[pl.BlockSpec((None, tq, ATT_WIDTH), qmap),
                  pl.BlockSpec((None, tq, IDX_WIDTH), qmap),
                  pl.BlockSpec((None, tq, LANES), qmap),
                  pl.BlockSpec((None, n_keys, KV_WIDTH), kmap),
                  pl.BlockSpec((None, n_keys, KV_WIDTH), kmap),
                  pl.BlockSpec((None, n_keys, IDX_DIM), kmap),
                  pl.BlockSpec(cnt_rhs.shape, lambda bi, qi_: (0, 0))],
        out_specs=pl.BlockSpec((None, tq, ATT_WIDTH), lambda bi, qi_: (bi, qi_, 0)),
        out_shape=jax.ShapeDtypeStruct((b, n_qblocks * tq, ATT_WIDTH), BF16),
        compiler_params=pltpu.CompilerParams(dimension_semantics=("parallel", "arbitrary"),
                                             vmem_limit_bytes=VMEM_LIMIT),
        name=f"dsa_attn_{n_keys}",
    )(q3, qi3, misc3, k3, v3, ki3, cnt_rhs)


def _outffn_kernel(x_ref, oa_ref, ob_ref, wo_a_ref, wo_b_ref, g_ref, w1_ref, w2_ref, y_ref, *, ff_chunk):
    mix = (jnp.dot(oa_ref[...], wo_a_ref[...], preferred_element_type=F32)
           + jnp.dot(ob_ref[...], wo_b_ref[...], preferred_element_type=F32))
    h = x_ref[...] + mix
    ms = jnp.mean(h * h, axis=-1, keepdims=True)
    hn = (h * lax.rsqrt(ms + NORM_EPS) * g_ref[...]).astype(BF16)
    acc = h
    for j in range(D_FF // ff_chunk):
        act = jnp.dot(hn, w1_ref[:, j * ff_chunk:(j + 1) * ff_chunk], preferred_element_type=F32)
        act = jnp.square(jnp.maximum(act, 0.0)).astype(BF16)
        acc = acc + jnp.dot(act, w2_ref[j * ff_chunk:(j + 1) * ff_chunk, :], preferred_element_type=F32)
    y_ref[...] = acc


def _outffn(x2d, o_att, o_rwkv, p, tm):
    n = x2d.shape[0]
    row = lambda i: (i, 0)
    const = lambda i: (0, 0)
    params = [p["wo_a"], p["wo_b"], p["norm_ffn"], p["w_ff1"], p["w_ff2"]]
    return pl.pallas_call(
        functools.partial(_outffn_kernel, ff_chunk=1024),
        grid=(n // tm,),
        in_specs=[pl.BlockSpec((tm, D_MODEL), row), pl.BlockSpec((tm, ATT_WIDTH), row),
                  pl.BlockSpec((tm, RWKV_WIDTH), row)]
        + [pl.BlockSpec(a.shape, const, pipeline_mode=pl.Buffered(1)) for a in params],
        out_specs=pl.BlockSpec((tm, D_MODEL), row),
        out_shape=jax.ShapeDtypeStruct((n, D_MODEL), F32),
        compiler_params=pltpu.CompilerParams(dimension_semantics=("parallel",), vmem_limit_bytes=VMEM_LIMIT),
        name="outproj_ffn",
    )(x2d, o_att, o_rwkv, *params)


def _block_diag_ones():
    i = np.arange(ATT_WIDTH)
    return jnp.asarray((i[:, None] // HEAD_DIM) == (i[None, :] // HEAD_DIM), BF16)


def _count_rhs():
    i = np.arange(LANES)
    tri = (i[:, None] <= i[None, :]).astype(np.float32)
    return jnp.asarray(np.concatenate([tri, np.ones((LANES, LANES), np.float32)], axis=1), BF16)


def _prep_params(wl):
    (norm_mix, w_in, q_gain, k_gain, kidx_ln_g, kidx_ln_b, mu_shift, w0, w2, a0, a2, g2, k_k, k_a, r_k,
     ln_x_g, ln_x_b, w_out, norm_ffn, w_ff1, w_ff2) = wl
    row = lambda a: a.reshape(1, -1).astype(F32)
    o_ki = QKV_WIDTH
    o_wi = o_ki + IDX_DIM
    o_zs = o_wi + IDX_HEADS
    pad_cols = lambda a, wdt: jnp.pad(a, ((0, 0), (0, wdt - a.shape[1])))
    lora_rows = lambda a, off: jnp.pad(a, ((off, LORA_PAD - off - a.shape[0]), (0, 0))).astype(BF16)
    return {
        "norm_mix": row(norm_mix),
        "wa": w_in[:, :QKV_WIDTH].astype(BF16),
        "wb": pad_cols(w_in[:, o_ki:o_zs], LANES).astype(BF16),
        "wc": pad_cols(w_in[:, o_zs:], SHIFT_PAD).astype(BF16),
        "q_gain": row(jnp.tile(q_gain, N_HEADS)),
        "k_gain": row(jnp.tile(k_gain, N_KV_HEADS)),
        "ln_g": pad_cols(row(kidx_ln_g), LANES),
        "ln_b": pad_cols(row(kidx_ln_b), LANES),
        "bd": _block_diag_ones(),
        "mu": pad_cols(row(mu_shift), SHIFT_PAD),
        "w0": row(w0), "a0": row(a0),
        "w2": lora_rows(w2, 0), "a2": lora_rows(a2, W_LORA), "g2": lora_rows(g2, W_LORA + A_LORA),
        "k_k": row(k_k), "k_a": row(k_a), "r_k": row(r_k), "lnx_g": row(ln_x_g), "lnx_b": row(ln_x_b),
        "wo_a": w_out[:ATT_WIDTH].astype(BF16), "wo_b": w_out[ATT_WIDTH:].astype(BF16),
        "norm_ffn": row(norm_ffn),
        "w_ff1": w_ff1.astype(BF16), "w_ff2": w_ff2.astype(BF16),
    }


def _layer(x, pos0, k_past, v_past, ki_past, wkv0, shift0, p, cnt_rhs, *, chunk, tq, key_bucket):
    b, t, _ = x.shape
    n = b * t
    past = k_past.shape[1]
    x2d = x.reshape(n, D_MODEL)
    tm = min(ROW_TILE, n)
    q, k, v, qi, misc, zs, kbf, vbf, kibf = _inproj(x2d, t, pos0, p, tm)

    shift0p = jnp.pad(shift0, ((0, 0), (0, 0), (0, SHIFT_PAD - SHIFT_WIDTH)))
    zs3 = zs.reshape(b, t, SHIFT_PAD)
    o_rwkv, wkv_new = _rwkv(zs3, shift0p, wkv0, p, chunk)

    n_valid = past + t
    topk = min(TOPK_MAX, n_valid // 4)
    three = lambda a: a.reshape(b, t, a.shape[-1])
    k_new, v_new, ki_new = three(kbf), three(vbf), three(kibf)
    if past:
        n_keys = -(-n_valid // LANES) * LANES
        cat = lambda old, new: jnp.pad(
            jnp.concatenate([old.reshape(b, past, -1).astype(BF16), new], axis=1),
            ((0, 0), (0, n_keys - n_valid), (0, 0)))
        k_all, v_all, ki_all = cat(k_past, k_new), cat(v_past, v_new), cat(ki_past, ki_new)
    else:
        k_all, v_all, ki_all = k_new, v_new, ki_new
    q3, qi3, misc3 = three(q), three(qi), three(misc)
    if past:
        o_att = _attention(q3, qi3, misc3, k_all, v_all, ki_all, cnt_rhs, tq=tq, n_keys=k_all.shape[1],
                           q_block0=0, n_qblocks=t // tq, pos0=pos0, n_valid=n_valid, topk=topk)
    else:
        parts = []
        per = key_bucket // tq
        for j in range(t // key_bucket):
            parts.append(_attention(q3, qi3, misc3, k_all, v_all, ki_all, cnt_rhs, tq=tq,
                                    n_keys=(j + 1) * key_bucket, q_block0=j * per, n_qblocks=per,
                                    pos0=pos0, n_valid=n_valid, topk=topk))
        o_att = jnp.concatenate(parts, axis=1)

    y = _outffn(x2d, o_att.reshape(n, ATT_WIDTH), o_rwkv.reshape(n, RWKV_WIDTH), p, tm)
    return (y.reshape(b, t, D_MODEL), k.reshape(b, t, N_KV_HEADS, HEAD_DIM),
            v.reshape(b, t, N_KV_HEADS, HEAD_DIM), misc3[:, :, :IDX_DIM], wkv_new,
            zs3[:, t - 1:, :SHIFT_WIDTH])


def kernel(x_prompt, x_sample, cache_k, cache_v, cache_kidx, state_wkv, state_shift, norm_mix, w_in, q_gain, k_gain, kidx_ln_g, kidx_ln_b, mu_shift, w0, w2, a0, a2, g2, k_k, k_a, r_k, ln_x_g, ln_x_b, w_out, norm_ffn, w_ff1, w_ff2):
    depth = norm_mix.shape[0]
    bp, tp = x_prompt.shape[0], x_prompt.shape[1]
    bs, ts = x_sample.shape[0], x_sample.shape[1]
    past_len = cache_k.shape[2]
    cnt_rhs = _count_rhs()
    no_kv = jnp.zeros((bp, 0, KV_WIDTH), F32)
    no_ki = jnp.zeros((bp, 0, IDX_DIM), F32)
    wkv_zero = jnp.zeros((bp, RWKV_HEADS, HEAD_DIM, HEAD_DIM), F32)
    shift_zero = jnp.zeros((bp, 1, SHIFT_WIDTH), F32)

    y_p, y_s = x_prompt, x_sample
    outs_p, outs_s = [], []
    for l in range(depth):
        wl = (norm_mix[l], w_in[l], q_gain[l], k_gain[l], kidx_ln_g[l], kidx_ln_b[l], mu_shift[l],
              w0[l], w2[l], a0[l], a2[l], g2[l], k_k[l], k_a[l], r_k[l], ln_x_g[l], ln_x_b[l],
              w_out[l], norm_ffn[l], w_ff1[l], w_ff2[l])
        p = _prep_params(wl)
        y_p, *rest_p = _layer(y_p, 0, no_kv, no_kv, no_ki, wkv_zero, shift_zero, p, cnt_rhs,
                              chunk=64, tq=128, key_bucket=512)
        y_s, *rest_s = _layer(y_s, past_len, cache_k[l], cache_v[l], cache_kidx[l], state_wkv[l],
                              state_shift[l], p, cnt_rhs, chunk=ts, tq=ts, key_bucket=None)
        outs_p.append(rest_p)
        outs_s.append(rest_s)
    stack = lambda outs, i: jnp.stack([o[i] for o in outs])
    return (y_p, y_s,
            *(stack(outs_p, i) for i in range(5)),
            *(stack(outs_s, i) for i in range(5)))
```

```python
import functools

import numpy as np
import jax
import jax.numpy as jnp
from jax import lax
from jax.experimental import pallas as pl
from jax.experimental.pallas import tpu as pltpu

F32 = jnp.float32
BF16 = jnp.bfloat16

D_MODEL = 1024
CHUNK = 64
HEAD_DIM = 64
ATT_WIDTH = D_MODEL // 2
N_HEADS = ATT_WIDTH // HEAD_DIM
N_KV_HEADS = N_HEADS // 2
GQA_GROUP = N_HEADS // N_KV_HEADS
KV_WIDTH = N_KV_HEADS * HEAD_DIM
ROT_DIM = HEAD_DIM // 4
ROPE_THETA = 500000.0
IDX_HEADS = 8
IDX_DIM = 64
IDX_WIDTH = IDX_HEADS * IDX_DIM
TOPK_MAX = 256
RWKV_WIDTH = D_MODEL - ATT_WIDTH
RWKV_HEADS = RWKV_WIDTH // HEAD_DIM
W_LORA = 32
A_LORA = 32
G_LORA = 96
LORA_WIDTH = W_LORA + A_LORA + G_LORA
SHIFT_WIDTH = 3 * RWKV_WIDTH + LORA_WIDTH
D_FF = 4 * D_MODEL
NORM_EPS = 1e-6
GN_EPS = 64e-5
L2_EPS = 1e-12

LANES = 128
SHIFT_PAD = 3 * RWKV_WIDTH + 2 * LANES
LORA_PAD = SHIFT_PAD - 3 * RWKV_WIDTH
QKV_WIDTH = ATT_WIDTH + 2 * KV_WIDTH + IDX_WIDTH
INV_BLOCK = 16
INV_SHIFT = 4
CHUNK_SHIFT = 6
MASK_BIAS = -1e30
ROW_TILE = 256
BISECT_ITERS = 32
BISECT_CHAINS = 4
VMEM_LIMIT = 56 * 1024 * 1024

NN = (((1,), (0,)), ((), ()))
NT = (((1,), (1,)), ((), ()))
TN = (((0,), (0,)), ((), ()))


def _dot(a, b, dims=NN):
    return lax.dot_general(a.astype(BF16), b.astype(BF16), dims, preferred_element_type=F32)


def _group_sum(x, bd):
    w = x.shape[-1]
    return _dot(x, bd[:w, :w])


def _rope(x, cos, sa, sb):
    w = x.shape[-1]
    reps = w // LANES
    if reps > 1:
        cos, sa, sb = (jnp.tile(t, (1, reps)) for t in (cos, sa, sb))
    half = ROT_DIM // 2
    return x * cos + pltpu.roll(x, w - half, 1) * sa + pltpu.roll(x, half, 1) * sb


def _inproj_kernel(x_ref, g_ref, wa_ref, wb_ref, wc_ref, qg_ref, kg_ref, lng_ref, lnb_ref, bd_ref,
                   cos_ref, sa_ref, sb_ref,
                   q_ref, k_ref, v_ref, qi_ref, misc_ref, zs_ref, kbf_ref, vbf_ref, kibf_ref):
    x = x_ref[...]
    ms = jnp.mean(x * x, axis=-1, keepdims=True)
    xn = (x * lax.rsqrt(ms + NORM_EPS) * g_ref[...]).astype(BF16)
    za = jnp.dot(xn, wa_ref[...], preferred_element_type=F32)
    zb = jnp.dot(xn, wb_ref[...], preferred_element_type=F32)
    zs_ref[...] = jnp.dot(xn, wc_ref[...], preferred_element_type=F32)

    cos, sa, sb = cos_ref[...], sa_ref[...], sb_ref[...]
    bd = bd_ref[...]

    zq = za[:, :ATT_WIDTH]
    qn = zq * lax.rsqrt(_group_sum(zq * zq, bd) * (1.0 / HEAD_DIM) + NORM_EPS) * qg_ref[...]
    q_ref[...] = (_rope(qn, cos, sa, sb) * (HEAD_DIM ** -0.5)).astype(BF16)

    zk = za[:, ATT_WIDTH:ATT_WIDTH + KV_WIDTH]
    kn = zk * lax.rsqrt(_group_sum(zk * zk, bd) * (1.0 / HEAD_DIM) + NORM_EPS) * kg_ref[...]
    k = _rope(kn, cos, sa, sb)
    k_ref[...] = k
    kbf_ref[...] = k.astype(BF16)

    v = za[:, ATT_WIDTH + KV_WIDTH:ATT_WIDTH + 2 * KV_WIDTH]
    v_ref[...] = v
    vbf_ref[...] = v.astype(BF16)

    zqi = za[:, ATT_WIDTH + 2 * KV_WIDTH:]
    qi_ref[...] = (_rope(zqi, cos, sa, sb) * (IDX_DIM ** -0.5)).astype(BF16)

    lane = lax.broadcasted_iota(jnp.int32, zb.shape, 1)
    is_ki = lane < IDX_DIM
    mu = jnp.sum(jnp.where(is_ki, zb, 0.0), axis=-1, keepdims=True) * (1.0 / IDX_DIM)
    d = jnp.where(is_ki, zb - mu, 0.0)
    var = jnp.sum(d * d, axis=-1, keepdims=True) * (1.0 / IDX_DIM)
    kin = d * lax.rsqrt(var + NORM_EPS) * lng_ref[...] + lnb_ref[...]
    ki = _rope(kin, cos, sa, sb)
    misc_ref[...] = jnp.where(is_ki, ki, zb * (IDX_HEADS ** -0.5))
    kibf_ref[...] = ki[:, :IDX_DIM].astype(BF16)


def _rope_tables(pos):
    half = ROT_DIM // 2
    inv = ROPE_THETA ** (-jnp.arange(0, ROT_DIM, 2, dtype=F32) / ROT_DIM)
    ang = pos.astype(F32)[:, None] * inv[None, :]
    cos, sin = jnp.cos(ang), jnp.sin(ang)
    n = pos.shape[0]
    ones = jnp.ones((n, HEAD_DIM - ROT_DIM), F32)
    zeros_h = jnp.zeros((n, half), F32)
    zeros_r = jnp.zeros((n, HEAD_DIM - ROT_DIM), F32)
    c64 = jnp.concatenate([cos, cos, ones], axis=1)
    sa64 = jnp.concatenate([-sin, zeros_h, zeros_r], axis=1)
    sb64 = jnp.concatenate([zeros_h, sin, zeros_r], axis=1)
    rep = LANES // HEAD_DIM
    return tuple(jnp.tile(t, (1, rep)) for t in (c64, sa64, sb64))


def _inproj(x2d, seq_len, pos0, p, tm):
    n = x2d.shape[0]
    tab_rows = max(seq_len, tm)
    pos = pos0 + (jnp.arange(tab_rows, dtype=jnp.int32) % seq_len)
    cos, sa, sb = _rope_tables(pos)
    nblk_t = tab_rows // tm

    def row(i):
        return (i, 0)

    def const(i):
        return (0, 0)

    def tab(i):
        return (i % nblk_t, 0)

    full = lambda a: pl.BlockSpec(a.shape, const)
    outs = [
        (ATT_WIDTH, BF16), (KV_WIDTH, F32), (KV_WIDTH, F32), (IDX_WIDTH, BF16), (LANES, F32),
        (SHIFT_PAD, F32), (KV_WIDTH, BF16), (KV_WIDTH, BF16), (IDX_DIM, BF16),
    ]
    params = [p["norm_mix"], p["wa"], p["wb"], p["wc"], p["q_gain"], p["k_gain"], p["ln_g"], p["ln_b"], p["bd"]]
    return pl.pallas_call(
        _inproj_kernel,
        grid=(n // tm,),
        in_specs=[pl.BlockSpec((tm, D_MODEL), row)] + [full(a) for a in params]
        + [pl.BlockSpec((tm, LANES), tab)] * 3,
        out_specs=[pl.BlockSpec((tm, w), row) for w, _ in outs],
        out_shape=[jax.ShapeDtypeStruct((n, w), dt) for w, dt in outs],
        compiler_params=pltpu.CompilerParams(dimension_semantics=("parallel",), vmem_limit_bytes=VMEM_LIMIT),
        name="inproj",
    )(x2d, *params, cos, sa, sb)


def _solve_unit_lower(lows, rhss, eye, blk):
    c = lows[0].shape[0]
    d = [jnp.where(blk, l, 0.0) for l in lows]
    e = [l - dh for l, dh in zip(lows, d)]
    dk = [_dot(dh, dh) for dh in d]
    p = [_dot(eye - dh, eye + d2) for dh, d2 in zip(d, dk)]
    span = 4
    while span < INV_BLOCK:
        dk = [_dot(m, m) for m in dk]
        p = [_dot(ph, eye + m) for ph, m in zip(p, dk)]
        span *= 2
    x = [_dot(ph, r) for ph, r in zip(p, rhss)]
    nb = c // INV_BLOCK
    if nb > 1:
        nk = [_dot(ph, eh) for ph, eh in zip(p, e)]
        x = [_dot(eye - m, xh) for m, xh in zip(nk, x)]
        order = 2
        while order < nb:
            nk = [_dot(m, m) for m in nk]
            x = [_dot(eye + m, xh) for m, xh in zip(nk, x)]
            order *= 2
    return x


def _rwkv_kernel(zs_ref, shift0_ref, wkv0_ref, mu_ref, w0_ref, w2_ref, a0_ref, a2_ref, g2_ref,
                 kk_ref, ka_ref, rk_ref, lng_ref, lnb_ref, bd_ref,
                 o_ref, sout_ref, s_scr, prev_scr, *, chunk):
    c = chunk
    ci = pl.program_id(1)
    w = RWKV_WIDTH

    @pl.when(ci == 0)
    def _():
        s_scr[...] = wkv0_ref[...]
        prev_scr[...] = shift0_ref[...]

    z = zs_ref[...]
    prev_row = prev_scr[...]
    mu = mu_ref[...]

    def mixed(lo, hi):
        zp = z[:, lo:hi]
        rolled = pltpu.roll(zp, 1, 0)
        first = lax.broadcasted_iota(jnp.int32, zp.shape, 0) == 0
        prv = jnp.where(first, prev_row[:, lo:hi], rolled)
        return zp + mu[:, lo:hi] * (prv - zp)

    r = mixed(0, w)
    k = mixed(w, 2 * w)
    v = mixed(2 * w, 3 * w)
    tail = mixed(3 * w, SHIFT_PAD)
    prev_scr[...] = z[c - 1:c, :]

    lw = w0_ref[...] + _dot(jnp.tanh(tail), w2_ref[...])
    nlw = -lw
    softplus = jnp.maximum(nlw, 0.0) + jnp.log(1.0 + jnp.exp(-jnp.abs(nlw)))
    logd = -jnp.exp(-softplus - 0.5)
    a = 1.0 / (1.0 + jnp.exp(-(a0_ref[...] + _dot(tail, a2_ref[...]))))
    gate = _dot(1.0 / (1.0 + jnp.exp(-tail)), g2_ref[...])

    bd = bd_ref[...]
    kk = k * kk_ref[...]
    kk = kk / jnp.maximum(jnp.sqrt(_group_sum(kk * kk, bd)), L2_EPS)
    k = k * (1.0 + (a - 1.0) * ka_ref[...])
    b = kk * a

    row = lax.broadcasted_iota(jnp.int32, (c, c), 0)
    col = lax.broadcasted_iota(jnp.int32, (c, c), 1)
    tri_incl = row >= col
    tri_strict = row > col
    eye = jnp.where(row == col, 1.0, 0.0)
    blk = jnp.right_shift(row, INV_SHIFT) == jnp.right_shift(col, INV_SHIFT)

    tri = jnp.where(tri_incl, 1.0, 0.0).astype(BF16)
    hi = logd.astype(BF16)
    rem = logd - hi.astype(F32)
    mid = rem.astype(BF16)
    lo = (rem - mid.astype(F32)).astype(BF16)
    lg = (jnp.dot(tri, hi, preferred_element_type=F32) + jnp.dot(tri, mid, preferred_element_type=F32)
          + jnp.dot(tri, lo, preferred_element_type=F32))
    g = jnp.exp(lg)
    ginv = jnp.exp(-lg)
    gprev = jnp.exp(lg - logd)

    a_m = gprev * kk
    b_m = b * ginv
    k_m = k * ginv
    p_m = g * r
    bonus_rk = r * k * rk_ref[...]

    heads = range(RWKV_HEADS)
    sls = [slice(h * HEAD_DIM, (h + 1) * HEAD_DIM) for h in heads]
    ar = [jnp.concatenate([a_m[:, sl], p_m[:, sl]], axis=0).astype(BF16) for sl in sls]
    kb = [jnp.concatenate([k_m[:, sl], b_m[:, sl]], axis=0).astype(BF16) for sl in sls]
    vh = [v[:, sl] for sl in sls]
    s_h = [s_scr[h] for h in heads]
    gm = [_dot(ar[h], kb[h], NT) for h in heads]
    a_s = [_dot(ar[h], s_h[h], NT) for h in heads]
    rhs = [a_s[h][:c] + _dot(jnp.where(tri_strict, gm[h][:c, :c], 0.0), vh[h]) for h in heads]
    u = _solve_unit_lower([jnp.where(tri_strict, gm[h][:c, c:], 0.0) for h in heads], rhs, eye, blk)
    ys, bonus = [], []
    for h in heads:
        n_pk = jnp.where(tri_incl, gm[h][c:, :c], 0.0)
        n_pb = jnp.where(tri_incl, gm[h][c:, c:], 0.0)
        y = a_s[h][c:] + _dot(jnp.concatenate([n_pk, -n_pb], axis=1), jnp.concatenate([vh[h], u[h]], axis=0))
        s_scr[h] = (s_h[h] + _dot(jnp.concatenate([vh[h], -u[h]], axis=0), kb[h], TN)) * g[c - 1:c, sls[h]]
        m = jnp.mean(y, axis=-1, keepdims=True)
        yc = y - m
        var = jnp.mean(yc * yc, axis=-1, keepdims=True)
        ys.append(yc * lax.rsqrt(var + GN_EPS))
        bonus.append(jnp.sum(bonus_rk[:, sls[h]], axis=-1, keepdims=True) * vh[h])

    yn = jnp.concatenate(ys, axis=1) * lng_ref[...] + lnb_ref[...]
    o_ref[...] = ((yn + jnp.concatenate(bonus, axis=1)) * gate).astype(o_ref.dtype)

    @pl.when(ci == pl.num_programs(1) - 1)
    def _():
        sout_ref[...] = s_scr[...]


def _rwkv(zs3, shift0, wkv0, p, chunk):
    b, t, _ = zs3.shape
    const = lambda bi, ci: (0, 0)
    full = lambda a: pl.BlockSpec(a.shape, const)
    params = [p["mu"], p["w0"], p["w2"], p["a0"], p["a2"], p["g2"], p["k_k"], p["k_a"], p["r_k"],
              p["lnx_g"], p["lnx_b"], p["bd"]]
    state_spec = pl.BlockSpec((None, RWKV_HEADS, HEAD_DIM, HEAD_DIM), lambda bi, ci: (bi, 0, 0, 0))
    return pl.pallas_call(
        functools.partial(_rwkv_kernel, chunk=chunk),
        grid=(b, t // chunk),
        in_specs=[pl.BlockSpec((None, chunk, SHIFT_PAD), lambda bi, ci: (bi, ci, 0)),
                  pl.BlockSpec((None, 1, SHIFT_PAD), lambda bi, ci: (bi, 0, 0)),
                  state_spec] + [full(a) for a in params],
        out_specs=[pl.BlockSpec((None, chunk, RWKV_WIDTH), lambda bi, ci: (bi, ci, 0)), state_spec],
        out_shape=[jax.ShapeDtypeStruct((b, t, RWKV_WIDTH), BF16),
                   jax.ShapeDtypeStruct((b, RWKV_HEADS, HEAD_DIM, HEAD_DIM), F32)],
        scratch_shapes=[pltpu.VMEM((RWKV_HEADS, HEAD_DIM, HEAD_DIM), F32), pltpu.VMEM((1, SHIFT_PAD), F32)],
        compiler_params=pltpu.CompilerParams(dimension_semantics=("parallel", "arbitrary"),
                                             vmem_limit_bytes=VMEM_LIMIT),
        name="rwkv7",
    )(zs3, shift0, wkv0, *params)


def _attn_kernel(q_ref, qi_ref, misc_ref, k_ref, v_ref, ki_ref, cnt_ref, o_ref,
                 *, n_grp, tq, n_keys, q_block0, pos0, n_valid, topk):
    qb = pl.program_id(1) + q_block0
    nl = n_keys // LANES
    rows = n_grp * tq

    pos = pos0 + qb * tq + lax.broadcasted_iota(jnp.int32, (tq, LANES), 0)
    limit = jnp.minimum((jnp.right_shift(pos, CHUNK_SHIFT) + 1) * CHUNK, n_valid)
    if n_grp > 1:
        limit = jnp.concatenate([limit] * n_grp, axis=0)
    key_id = lax.broadcasted_iota(jnp.int32, (rows, n_keys), 1)
    adm = key_id < jnp.tile(limit, (1, nl))

    parts = []
    for g in range(n_grp):
        qi = qi_ref[g]
        ki = ki_ref[g]
        misc = misc_ref[g]
        sc = None
        for h in range(IDX_HEADS):
            d = lax.dot_general(qi[:, h * IDX_DIM:(h + 1) * IDX_DIM], ki, NT, preferred_element_type=F32)
            term = misc[:, IDX_DIM + h:IDX_DIM + h + 1] * jnp.maximum(d, 0.0)
            sc = term if sc is None else sc + term
        parts.append(sc)
    score = parts[0] if n_grp == 1 else jnp.concatenate(parts, axis=0)
    sm = jnp.where(adm, score, -jnp.inf)

    cnt_rhs = cnt_ref[...]
    ones_sq = cnt_ref[:, LANES:]

    def count_ge(sm_c, th):
        part = None
        for j in range(nl):
            ind = jnp.where(sm_c[:, j * LANES:(j + 1) * LANES] >= th, 1.0, 0.0)
            part = ind if part is None else part + ind
        return jnp.dot(part.astype(BF16), ones_sq, preferred_element_type=F32)

    row_max = jnp.max(sm, axis=-1, keepdims=True)
    row_min = jnp.min(jnp.where(adm, score, jnp.inf), axis=-1, keepdims=True)
    lo0 = jnp.broadcast_to(row_min, (rows, LANES))
    hi0 = jnp.broadcast_to(2.0 * row_max - row_min + 1.0, (rows, LANES))

    rc = rows // BISECT_CHAINS
    chains = [slice(i * rc, (i + 1) * rc) for i in range(BISECT_CHAINS)]

    def bisect(_, carry):
        out = []
        for ch, (lo, hi) in zip(chains, carry):
            mid = 0.5 * (lo + hi)
            enough = count_ge(sm[ch], mid) >= topk
            out.append((jnp.where(enough, mid, lo), jnp.where(enough, hi, mid)))
        return tuple(out)

    bounds = lax.fori_loop(0, BISECT_ITERS, bisect, tuple((lo0[ch], hi0[ch]) for ch in chains))
    lo = jnp.concatenate([b[0] for b in bounds], axis=0)
    hi = jnp.concatenate([b[1] for b in bounds], axis=0)

    need = topk - count_ge(sm, hi)
    take_all = limit <= topk
    carry = jnp.zeros((rows, LANES), F32)
    bias = []
    for j in range(nl):
        sm_j = sm[:, j * LANES:(j + 1) * LANES]
        above = sm_j >= hi
        group = jnp.logical_and(sm_j >= lo, jnp.logical_not(above))
        pc = jnp.dot(jnp.where(group, 1.0, 0.0).astype(BF16), cnt_rhs, preferred_element_type=F32)
        fill = carry + pc[:, :LANES] <= need
        carry = carry + pc[:, LANES:]
        picked = jnp.logical_or(above, jnp.logical_and(group, fill))
        sel = jnp.logical_and(sm_j > -jnp.inf, jnp.logical_or(take_all, picked))
        bias.append(jnp.where(sel, 0.0, MASK_BIAS))
    bias = jnp.concatenate(bias, axis=1)

    for g in range(n_grp):
        q = q_ref[g]
        kx = k_ref[g]
        vx = v_ref[g]
        bias_g = jnp.concatenate([bias[g * tq:(g + 1) * tq]] * GQA_GROUP, axis=0)
        outs = []
        for gi in range(N_KV_HEADS):
            kg = kx[:, gi * HEAD_DIM:(gi + 1) * HEAD_DIM]
            vg = vx[:, gi * HEAD_DIM:(gi + 1) * HEAD_DIM]
            qs = jnp.concatenate([q[:, (gi * GQA_GROUP + j) * HEAD_DIM:(gi * GQA_GROUP + j + 1) * HEAD_DIM]
                                  for j in range(GQA_GROUP)], axis=0)
            s = lax.dot_general(qs, kg, NT, preferred_element_type=F32) + bias_g
            m = jnp.max(s, axis=-1, keepdims=True)
            pexp = jnp.exp(s - m)
            den = jnp.sum(pexp, axis=-1, keepdims=True)
            o = jnp.dot(pexp.astype(BF16), vg, preferred_element_type=F32) / den
            outs.extend(o[j * tq:(j + 1) * tq] for j in range(GQA_GROUP))
        o_ref[g] = jnp.concatenate(outs, axis=1).astype(o_ref.dtype)


def _attention(q3, qi3, misc3, k3, v3, ki3, cnt_rhs, *, n_grp, tq, n_keys, q_block0, n_qblocks, pos0,
               n_valid, topk):
    b = q3.shape[0]
    qmap = lambda bi, qi_: (bi, qi_ + q_block0, 0)
    kmap = lambda bi, qi_: (bi, 0, 0)
    return pl.pallas_call(
        functools.partial(_attn_kernel, n_grp=n_grp, tq=tq, n_keys=n_keys, q_block0=q_block0, pos0=pos0,
                          n_valid=n_valid, topk=topk),
        grid=(b // n_grp, n_qblocks),
        in_specs=[pl.BlockSpec((n_grp, tq, ATT_WIDTH), qmap),
                  pl.BlockSpec((n_grp, tq, IDX_WIDTH), qmap),
                  pl.BlockSpec((n_grp, tq, LANES), qmap),
                  pl.BlockSpec((n_grp, n_keys, KV_WIDTH), kmap),
                  pl.BlockSpec((n_grp, n_keys, KV_WIDTH), kmap),
                  pl.BlockSpec((n_grp, n_keys, IDX_DIM), kmap),
                  pl.BlockSpec(cnt_rhs.shape, lambda bi, qi_: (0, 0))],
        out_specs=pl.BlockSpec((n_grp, tq, ATT_WIDTH), lambda bi, qi_: (bi, qi_, 0)),
        out_shape=jax.ShapeDtypeStruct((b, n_qblocks * tq, ATT_WIDTH), BF16),
        compiler_params=pltpu.CompilerParams(dimension_semantics=("parallel", "arbitrary"),
                                             vmem_limit_bytes=VMEM_LIMIT),
        name=f"dsa_attn_{n_keys}",
    )(q3, qi3, misc3, k3, v3, ki3, cnt_rhs)


def _outffn_kernel(x_ref, oa_ref, ob_ref, wo_a_ref, wo_b_ref, g_ref, w1_ref, w2_ref, y_ref, *, ff_chunk):
    mix = (jnp.dot(oa_ref[...], wo_a_ref[...], preferred_element_type=F32)
           + jnp.dot(ob_ref[...], wo_b_ref[...], preferred_element_type=F32))
    h = x_ref[...] + mix
    ms = jnp.mean(h * h, axis=-1, keepdims=True)
    hn = (h * lax.rsqrt(ms + NORM_EPS) * g_ref[...]).astype(BF16)
    acc = h
    for j in range(D_FF // ff_chunk):
        act = jnp.dot(hn, w1_ref[:, j * ff_chunk:(j + 1) * ff_chunk], preferred_element_type=F32)
        act = jnp.square(jnp.maximum(act, 0.0)).astype(BF16)
        acc = acc + jnp.dot(act, w2_ref[j * ff_chunk:(j + 1) * ff_chunk, :], preferred_element_type=F32)
    y_ref[...] = acc


def _outffn(x2d, o_att, o_rwkv, p, tm):
    n = x2d.shape[0]
    row = lambda i: (i, 0)
    const = lambda i: (0, 0)
    params = [p["wo_a"], p["wo_b"], p["norm_ffn"], p["w_ff1"], p["w_ff2"]]
    return pl.pallas_call(
        functools.partial(_outffn_kernel, ff_chunk=1024),
        grid=(n // tm,),
        in_specs=[pl.BlockSpec((tm, D_MODEL), row), pl.BlockSpec((tm, ATT_WIDTH), row),
                  pl.BlockSpec((tm, RWKV_WIDTH), row)]
        + [pl.BlockSpec(a.shape, const, pipeline_mode=pl.Buffered(1)) for a in params],
        out_specs=pl.BlockSpec((tm, D_MODEL), row),
        out_shape=jax.ShapeDtypeStruct((n, D_MODEL), F32),
        compiler_params=pltpu.CompilerParams(dimension_semantics=("parallel",), vmem_limit_bytes=VMEM_LIMIT),
        name="outproj_ffn",
    )(x2d, o_att, o_rwkv, *params)


def _block_diag_ones():
    i = np.arange(ATT_WIDTH)
    return jnp.asarray((i[:, None] // HEAD_DIM) == (i[None, :] // HEAD_DIM), BF16)


def _count_rhs():
    i = np.arange(LANES)
    tri = (i[:, None] <= i[None, :]).astype(np.float32)
    return jnp.asarray(np.concatenate([tri, np.ones((LANES, LANES), np.float32)], axis=1), BF16)


def _prep_params(wl):
    (norm_mix, w_in, q_gain, k_gain, kidx_ln_g, kidx_ln_b, mu_shift, w0, w2, a0, a2, g2, k_k, k_a, r_k,
     ln_x_g, ln_x_b, w_out, norm_ffn, w_ff1, w_ff2) = wl
    row = lambda a: a.reshape(1, -1).astype(F32)
    o_ki = QKV_WIDTH
    o_wi = o_ki + IDX_DIM
    o_zs = o_wi + IDX_HEADS
    pad_cols = lambda a, wdt: jnp.pad(a, ((0, 0), (0, wdt - a.shape[1])))
    lora_rows = lambda a, off: jnp.pad(a, ((off, LORA_PAD - off - a.shape[0]), (0, 0))).astype(BF16)
    return {
        "norm_mix": row(norm_mix),
        "wa": w_in[:, :QKV_WIDTH].astype(BF16),
        "wb": pad_cols(w_in[:, o_ki:o_zs], LANES).astype(BF16),
        "wc": pad_cols(w_in[:, o_zs:], SHIFT_PAD).astype(BF16),
        "q_gain": row(jnp.tile(q_gain, N_HEADS)),
        "k_gain": row(jnp.tile(k_gain, N_KV_HEADS)),
        "ln_g": pad_cols(row(kidx_ln_g), LANES),
        "ln_b": pad_cols(row(kidx_ln_b), LANES),
        "bd": _block_diag_ones(),
        "mu": pad_cols(row(mu_shift), SHIFT_PAD),
        "w0": row(w0), "a0": row(a0),
        "w2": lora_rows(w2, 0), "a2": lora_rows(a2, W_LORA), "g2": lora_rows(g2, W_LORA + A_LORA),
        "k_k": row(k_k), "k_a": row(k_a), "r_k": row(r_k), "lnx_g": row(ln_x_g), "lnx_b": row(ln_x_b),
        "wo_a": w_out[:ATT_WIDTH].astype(BF16), "wo_b": w_out[ATT_WIDTH:].astype(BF16),
        "norm_ffn": row(norm_ffn),
        "w_ff1": w_ff1.astype(BF16), "w_ff2": w_ff2.astype(BF16),
    }


def _layer(x, pos0, k_past, v_past, ki_past, wkv0, shift0, p, cnt_rhs, *, chunk, n_grp, tq, key_bucket):
    b, t, _ = x.shape
    n = b * t
    past = k_past.shape[1]
    x2d = x.reshape(n, D_MODEL)
    tm = min(ROW_TILE, n)
    q, k, v, qi, misc, zs, kbf, vbf, kibf = _inproj(x2d, t, pos0, p, tm)

    shift0p = jnp.pad(shift0, ((0, 0), (0, 0), (0, SHIFT_PAD - SHIFT_WIDTH)))
    zs3 = zs.reshape(b, t, SHIFT_PAD)
    o_rwkv, wkv_new = _rwkv(zs3, shift0p, wkv0, p, chunk)

    n_valid = past + t
    topk = min(TOPK_MAX, n_valid // 4)
    three = lambda a: a.reshape(b, t, a.shape[-1])
    k_new, v_new, ki_new = three(kbf), three(vbf), three(kibf)
    if past:
        n_keys = -(-n_valid // LANES) * LANES
        cat = lambda old, new: jnp.pad(
            jnp.concatenate([old.reshape(b, past, -1).astype(BF16), new], axis=1),
            ((0, 0), (0, n_keys - n_valid), (0, 0)))
        k_all, v_all, ki_all = cat(k_past, k_new), cat(v_past, v_new), cat(ki_past, ki_new)
    else:
        k_all, v_all, ki_all = k_new, v_new, ki_new
    q3, qi3, misc3 = three(q), three(qi), three(misc)
    if past:
        o_att = _attention(q3, qi3, misc3, k_all, v_all, ki_all, cnt_rhs, n_grp=n_grp, tq=tq,
                           n_keys=k_all.shape[1], q_block0=0, n_qblocks=t // tq, pos0=pos0,
                           n_valid=n_valid, topk=topk)
    else:
        parts = []
        per = key_bucket // tq
        for j in range(t // key_bucket):
            parts.append(_attention(q3, qi3, misc3, k_all, v_all, ki_all, cnt_rhs, n_grp=n_grp, tq=tq,
                                    n_keys=(j + 1) * key_bucket, q_block0=j * per, n_qblocks=per,
                                    pos0=pos0, n_valid=n_valid, topk=topk))
        o_att = jnp.concatenate(parts, axis=1)

    y = _outffn(x2d, o_att.reshape(n, ATT_WIDTH), o_rwkv.reshape(n, RWKV_WIDTH), p, tm)
    return (y.reshape(b, t, D_MODEL), k.reshape(b, t, N_KV_HEADS, HEAD_DIM),
            v.reshape(b, t, N_KV_HEADS, HEAD_DIM), misc3[:, :, :IDX_DIM], wkv_new,
            zs3[:, t - 1:, :SHIFT_WIDTH])


def kernel(x_prompt, x_sample, cache_k, cache_v, cache_kidx, state_wkv, state_shift, norm_mix, w_in, q_gain, k_gain, kidx_ln_g, kidx_ln_b, mu_shift, w0, w2, a0, a2, g2, k_k, k_a, r_k, ln_x_g, ln_x_b, w_out, norm_ffn, w_ff1, w_ff2):
    depth = norm_mix.shape[0]
    bp, tp = x_prompt.shape[0], x_prompt.shape[1]
    bs, ts = x_sample.shape[0], x_sample.shape[1]
    past_len = cache_k.shape[2]
    cnt_rhs = _count_rhs()
    no_kv = jnp.zeros((bp, 0, KV_WIDTH), F32)
    no_ki = jnp.zeros((bp, 0, IDX_DIM), F32)
    wkv_zero = jnp.zeros((bp, RWKV_HEADS, HEAD_DIM, HEAD_DIM), F32)
    shift_zero = jnp.zeros((bp, 1, SHIFT_WIDTH), F32)

    y_p, y_s = x_prompt, x_sample
    outs_p, outs_s = [], []
    for l in range(depth):
        wl = (norm_mix[l], w_in[l], q_gain[l], k_gain[l], kidx_ln_g[l], kidx_ln_b[l], mu_shift[l],
              w0[l], w2[l], a0[l], a2[l], g2[l], k_k[l], k_a[l], r_k[l], ln_x_g[l], ln_x_b[l],
              w_out[l], norm_ffn[l], w_ff1[l], w_ff2[l])
        p = _prep_params(wl)
        y_p, *rest_p = _layer(y_p, 0, no_kv, no_kv, no_ki, wkv_zero, shift_zero, p, cnt_rhs,
                              chunk=64, n_grp=1, tq=256, key_bucket=512)
        y_s, *rest_s = _layer(y_s, past_len, cache_k[l], cache_v[l], cache_kidx[l], state_wkv[l],
                              state_shift[l], p, cnt_rhs, chunk=ts, n_grp=min(bs, 256 // ts), tq=ts,
                              key_bucket=None)
        outs_p.append(rest_p)
        outs_s.append(rest_s)
    stack = lambda outs, i: jnp.stack([o[i] for o in outs])
    return (y_p, y_s,
            *(stack(outs_p, i) for i in range(5)),
            *(stack(outs_s, i) for i in range(5)))
```

```python
import functools

import numpy as np
import jax
import jax.numpy as jnp
from jax import lax
from jax.experimental import pallas as pl
from jax.experimental.pallas import tpu as pltpu

F32 = jnp.float32
BF16 = jnp.bfloat16

D_MODEL = 1024
CHUNK = 64
HEAD_DIM = 64
ATT_WIDTH = D_MODEL // 2
N_HEADS = ATT_WIDTH // HEAD_DIM
N_KV_HEADS = N_HEADS // 2
GQA_GROUP = N_HEADS // N_KV_HEADS
KV_WIDTH = N_KV_HEADS * HEAD_DIM
ROT_DIM = HEAD_DIM // 4
ROPE_THETA = 500000.0
IDX_HEADS = 8
IDX_DIM = 64
IDX_WIDTH = IDX_HEADS * IDX_DIM
TOPK_MAX = 256
RWKV_WIDTH = D_MODEL - ATT_WIDTH
RWKV_HEADS = RWKV_WIDTH // HEAD_DIM
W_LORA = 32
A_LORA = 32
G_LORA = 96
LORA_WIDTH = W_LORA + A_LORA + G_LORA
SHIFT_WIDTH = 3 * RWKV_WIDTH + LORA_WIDTH
D_FF = 4 * D_MODEL
NORM_EPS = 1e-6
GN_EPS = 64e-5
L2_EPS = 1e-12

LANES = 128
SHIFT_PAD = 3 * RWKV_WIDTH + 2 * LANES
LORA_PAD = SHIFT_PAD - 3 * RWKV_WIDTH
QKV_WIDTH = ATT_WIDTH + 2 * KV_WIDTH + IDX_WIDTH
INV_BLOCK = 16
INV_SHIFT = 4
CHUNK_SHIFT = 6
MASK_BIAS = -1e30
ROW_TILE = 256
BISECT_ITERS = 32
BISECT_CHAINS = 4
BISECT_UNROLL = 2
VMEM_LIMIT = 56 * 1024 * 1024

NN = (((1,), (0,)), ((), ()))
NT = (((1,), (1,)), ((), ()))
TN = (((0,), (0,)), ((), ()))


def _dot(a, b, dims=NN):
    return lax.dot_general(a.astype(BF16), b.astype(BF16), dims, preferred_element_type=F32)


def _group_sum(x, bd):
    w = x.shape[-1]
    return _dot(x, bd[:w, :w])


def _rope(x, cos, sa, sb):
    w = x.shape[-1]
    reps = w // LANES
    if reps > 1:
        cos, sa, sb = (jnp.tile(t, (1, reps)) for t in (cos, sa, sb))
    half = ROT_DIM // 2
    return x * cos + pltpu.roll(x, w - half, 1) * sa + pltpu.roll(x, half, 1) * sb


def _spread_heads(x, fill):
    low = lax.broadcasted_iota(jnp.int32, (x.shape[0], LANES), 1) < HEAD_DIM
    cols = []
    for c in range(x.shape[1] // LANES):
        pair = x[:, c * LANES:(c + 1) * LANES]
        cols.append(jnp.where(low, pair, fill))
        cols.append(jnp.where(low, pltpu.roll(pair, HEAD_DIM, 1), fill))
    return jnp.concatenate(cols, axis=1)


def _pack_heads(slabs):
    low = lax.broadcasted_iota(jnp.int32, slabs[0].shape, 1) < HEAD_DIM
    cols = [jnp.where(low, slabs[i], pltpu.roll(slabs[i + 1], HEAD_DIM, 1)) for i in range(0, len(slabs), 2)]
    return jnp.concatenate(cols, axis=1)


def _inproj_kernel(x_ref, g_ref, wa_ref, wb_ref, wc_ref, qg_ref, kg_ref, lng_ref, lnb_ref, bd_ref,
                   cos_ref, sa_ref, sb_ref,
                   q_ref, k_ref, v_ref, qi_ref, misc_ref, zs_ref, kbf_ref, vbf_ref, kibf_ref):
    x = x_ref[...]
    ms = jnp.mean(x * x, axis=-1, keepdims=True)
    xn = (x * lax.rsqrt(ms + NORM_EPS) * g_ref[...]).astype(BF16)
    za = jnp.dot(xn, wa_ref[...], preferred_element_type=F32)
    zb = jnp.dot(xn, wb_ref[...], preferred_element_type=F32)
    zs_ref[...] = jnp.dot(xn, wc_ref[...], preferred_element_type=F32)

    cos, sa, sb = cos_ref[...], sa_ref[...], sb_ref[...]
    bd = bd_ref[...]

    zq = za[:, :ATT_WIDTH]
    qn = zq * lax.rsqrt(_group_sum(zq * zq, bd) * (1.0 / HEAD_DIM) + NORM_EPS) * qg_ref[...]
    q_ref[...] = _spread_heads(_rope(qn, cos, sa, sb) * (HEAD_DIM ** -0.5), 0.0).astype(BF16)

    zk = za[:, ATT_WIDTH:ATT_WIDTH + KV_WIDTH]
    kn = zk * lax.rsqrt(_group_sum(zk * zk, bd) * (1.0 / HEAD_DIM) + NORM_EPS) * kg_ref[...]
    k = _rope(kn, cos, sa, sb)
    k_ref[...] = k
    kbf_ref[...] = _spread_heads(k, 0.0).astype(BF16)

    v = za[:, ATT_WIDTH + KV_WIDTH:ATT_WIDTH + 2 * KV_WIDTH]
    v_ref[...] = v
    vbf_ref[...] = _spread_heads(v, 1.0).astype(BF16)

    zqi = za[:, ATT_WIDTH + 2 * KV_WIDTH:]
    qi_ref[...] = _spread_heads(_rope(zqi, cos, sa, sb) * (IDX_DIM ** -0.5), 0.0).astype(BF16)

    lane = lax.broadcasted_iota(jnp.int32, zb.shape, 1)
    is_ki = lane < IDX_DIM
    mu = jnp.sum(jnp.where(is_ki, zb, 0.0), axis=-1, keepdims=True) * (1.0 / IDX_DIM)
    d = jnp.where(is_ki, zb - mu, 0.0)
    var = jnp.sum(d * d, axis=-1, keepdims=True) * (1.0 / IDX_DIM)
    kin = d * lax.rsqrt(var + NORM_EPS) * lng_ref[...] + lnb_ref[...]
    ki = _rope(kin, cos, sa, sb)
    misc_ref[...] = jnp.where(is_ki, ki, zb * (IDX_HEADS ** -0.5))
    kibf_ref[...] = jnp.where(is_ki, ki, 0.0).astype(BF16)


def _rope_tables(pos):
    half = ROT_DIM // 2
    inv = ROPE_THETA ** (-jnp.arange(0, ROT_DIM, 2, dtype=F32) / ROT_DIM)
    ang = pos.astype(F32)[:, None] * inv[None, :]
    cos, sin = jnp.cos(ang), jnp.sin(ang)
    n = pos.shape[0]
    ones = jnp.ones((n, HEAD_DIM - ROT_DIM), F32)
    zeros_h = jnp.zeros((n, half), F32)
    zeros_r = jnp.zeros((n, HEAD_DIM - ROT_DIM), F32)
    c64 = jnp.concatenate([cos, cos, ones], axis=1)
    sa64 = jnp.concatenate([-sin, zeros_h, zeros_r], axis=1)
    sb64 = jnp.concatenate([zeros_h, sin, zeros_r], axis=1)
    rep = LANES // HEAD_DIM
    return tuple(jnp.tile(t, (1, rep)) for t in (c64, sa64, sb64))


def _inproj(x2d, seq_len, pos0, p, tm):
    n = x2d.shape[0]
    tab_rows = max(seq_len, tm)
    pos = pos0 + (jnp.arange(tab_rows, dtype=jnp.int32) % seq_len)
    cos, sa, sb = _rope_tables(pos)
    nblk_t = tab_rows // tm

    def row(i):
        return (i, 0)

    def const(i):
        return (0, 0)

    def tab(i):
        return (i % nblk_t, 0)

    full = lambda a: pl.BlockSpec(a.shape, const)
    outs = [
        (N_HEADS * LANES, BF16), (KV_WIDTH, F32), (KV_WIDTH, F32), (IDX_HEADS * LANES, BF16), (LANES, F32),
        (SHIFT_PAD, F32), (N_KV_HEADS * LANES, BF16), (N_KV_HEADS * LANES, BF16), (LANES, BF16),
    ]
    params = [p["norm_mix"], p["wa"], p["wb"], p["wc"], p["q_gain"], p["k_gain"], p["ln_g"], p["ln_b"], p["bd"]]
    return pl.pallas_call(
        _inproj_kernel,
        grid=(n // tm,),
        in_specs=[pl.BlockSpec((tm, D_MODEL), row)] + [full(a) for a in params]
        + [pl.BlockSpec((tm, LANES), tab)] * 3,
        out_specs=[pl.BlockSpec((tm, w), row) for w, _ in outs],
        out_shape=[jax.ShapeDtypeStruct((n, w), dt) for w, dt in outs],
        compiler_params=pltpu.CompilerParams(dimension_semantics=("parallel",), vmem_limit_bytes=VMEM_LIMIT),
        name="inproj",
    )(x2d, *params, cos, sa, sb)


def _solve_unit_lower(lows, rhss, eye, blk):
    c = lows[0].shape[0]
    d = [jnp.where(blk, l, 0.0) for l in lows]
    e = [l - dh for l, dh in zip(lows, d)]
    dk = [_dot(dh, dh) for dh in d]
    p = [_dot(eye - dh, eye + d2) for dh, d2 in zip(d, dk)]
    span = 4
    while span < INV_BLOCK:
        dk = [_dot(m, m) for m in dk]
        p = [_dot(ph, eye + m) for ph, m in zip(p, dk)]
        span *= 2
    x = [_dot(ph, r) for ph, r in zip(p, rhss)]
    nb = c // INV_BLOCK
    if nb > 1:
        nk = [_dot(ph, eh) for ph, eh in zip(p, e)]
        x = [_dot(eye - m, xh) for m, xh in zip(nk, x)]
        order = 2
        while order < nb:
            nk = [_dot(m, m) for m in nk]
            x = [_dot(eye + m, xh) for m, xh in zip(nk, x)]
            order *= 2
    return x


def _rwkv_kernel(zs_ref, shift0_ref, wkv0_ref, mu_ref, w0_ref, w2_ref, a0_ref, a2_ref, g2_ref,
                 kk_ref, ka_ref, rk_ref, lng_ref, lnb_ref, bd_ref,
                 o_ref, sout_ref, s_scr, prev_scr, *, chunk):
    c = chunk
    ci = pl.program_id(1)
    w = RWKV_WIDTH

    @pl.when(ci == 0)
    def _():
        s_scr[...] = wkv0_ref[...]
        prev_scr[...] = shift0_ref[...]

    z = zs_ref[...]
    prev_row = prev_scr[...]
    mu = mu_ref[...]

    def mixed(lo, hi):
        zp = z[:, lo:hi]
        rolled = pltpu.roll(zp, 1, 0)
        first = lax.broadcasted_iota(jnp.int32, zp.shape, 0) == 0
        prv = jnp.where(first, prev_row[:, lo:hi], rolled)
        return zp + mu[:, lo:hi] * (prv - zp)

    r = mixed(0, w)
    k = mixed(w, 2 * w)
    v = mixed(2 * w, 3 * w)
    tail = mixed(3 * w, SHIFT_PAD)
    prev_scr[...] = z[c - 1:c, :]

    lw = w0_ref[...] + _dot(jnp.tanh(tail), w2_ref[...])
    nlw = -lw
    softplus = jnp.maximum(nlw, 0.0) + jnp.log(1.0 + jnp.exp(-jnp.abs(nlw)))
    logd = -jnp.exp(-softplus - 0.5)
    a = 1.0 / (1.0 + jnp.exp(-(a0_ref[...] + _dot(tail, a2_ref[...]))))
    gate = _dot(1.0 / (1.0 + jnp.exp(-tail)), g2_ref[...])

    bd = bd_ref[...]
    kk = k * kk_ref[...]
    kk = kk / jnp.maximum(jnp.sqrt(_group_sum(kk * kk, bd)), L2_EPS)
    k = k * (1.0 + (a - 1.0) * ka_ref[...])
    b = kk * a

    row = lax.broadcasted_iota(jnp.int32, (c, c), 0)
    col = lax.broadcasted_iota(jnp.int32, (c, c), 1)
    tri_incl = row >= col
    tri_strict = row > col
    eye = jnp.where(row == col, 1.0, 0.0)
    blk = jnp.right_shift(row, INV_SHIFT) == jnp.right_shift(col, INV_SHIFT)

    tri = jnp.where(tri_incl, 1.0, 0.0).astype(BF16)
    hi = logd.astype(BF16)
    rem = logd - hi.astype(F32)
    mid = rem.astype(BF16)
    lo = (rem - mid.astype(F32)).astype(BF16)
    lg = (jnp.dot(tri, hi, preferred_element_type=F32) + jnp.dot(tri, mid, preferred_element_type=F32)
          + jnp.dot(tri, lo, preferred_element_type=F32))
    g = jnp.exp(lg)
    ginv = jnp.exp(-lg)
    gprev = jnp.exp(lg - logd)

    a_m = gprev * kk
    b_m = b * ginv
    k_m = k * ginv
    p_m = g * r
    bonus_rk = r * k * rk_ref[...]

    heads = range(RWKV_HEADS)
    sls = [slice(h * HEAD_DIM, (h + 1) * HEAD_DIM) for h in heads]
    ar = [jnp.concatenate([a_m[:, sl], p_m[:, sl]], axis=0).astype(BF16) for sl in sls]
    kb = [jnp.concatenate([k_m[:, sl], b_m[:, sl]], axis=0).astype(BF16) for sl in sls]
    vh = [v[:, sl] for sl in sls]
    s_h = [s_scr[h] for h in heads]
    gm = [_dot(ar[h], kb[h], NT) for h in heads]
    a_s = [_dot(ar[h], s_h[h], NT) for h in heads]
    rhs = [a_s[h][:c] + _dot(jnp.where(tri_strict, gm[h][:c, :c], 0.0), vh[h]) for h in heads]
    u = _solve_unit_lower([jnp.where(tri_strict, gm[h][:c, c:], 0.0) for h in heads], rhs, eye, blk)
    ys, bonus = [], []
    for h in heads:
        n_pk = jnp.where(tri_incl, gm[h][c:, :c], 0.0)
        n_pb = jnp.where(tri_incl, gm[h][c:, c:], 0.0)
        y = a_s[h][c:] + _dot(jnp.concatenate([n_pk, -n_pb], axis=1), jnp.concatenate([vh[h], u[h]], axis=0))
        s_scr[h] = (s_h[h] + _dot(jnp.concatenate([vh[h], -u[h]], axis=0), kb[h], TN)) * g[c - 1:c, sls[h]]
        m = jnp.mean(y, axis=-1, keepdims=True)
        yc = y - m
        var = jnp.mean(yc * yc, axis=-1, keepdims=True)
        ys.append(yc * lax.rsqrt(var + GN_EPS))
        bonus.append(jnp.sum(bonus_rk[:, sls[h]], axis=-1, keepdims=True) * vh[h])

    yn = jnp.concatenate(ys, axis=1) * lng_ref[...] + lnb_ref[...]
    o_ref[...] = ((yn + jnp.concatenate(bonus, axis=1)) * gate).astype(o_ref.dtype)

    @pl.when(ci == pl.num_programs(1) - 1)
    def _():
        sout_ref[...] = s_scr[...]


def _rwkv(zs3, shift0, wkv0, p, chunk):
    b, t, _ = zs3.shape
    const = lambda bi, ci: (0, 0)
    full = lambda a: pl.BlockSpec(a.shape, const)
    params = [p["mu"], p["w0"], p["w2"], p["a0"], p["a2"], p["g2"], p["k_k"], p["k_a"], p["r_k"],
              p["lnx_g"], p["lnx_b"], p["bd"]]
    state_spec = pl.BlockSpec((None, RWKV_HEADS, HEAD_DIM, HEAD_DIM), lambda bi, ci: (bi, 0, 0, 0))
    return pl.pallas_call(
        functools.partial(_rwkv_kernel, chunk=chunk),
        grid=(b, t // chunk),
        in_specs=[pl.BlockSpec((None, chunk, SHIFT_PAD), lambda bi, ci: (bi, ci, 0)),
                  pl.BlockSpec((None, 1, SHIFT_PAD), lambda bi, ci: (bi, 0, 0)),
                  state_spec] + [full(a) for a in params],
        out_specs=[pl.BlockSpec((None, chunk, RWKV_WIDTH), lambda bi, ci: (bi, ci, 0)), state_spec],
        out_shape=[jax.ShapeDtypeStruct((b, t, RWKV_WIDTH), BF16),
                   jax.ShapeDtypeStruct((b, RWKV_HEADS, HEAD_DIM, HEAD_DIM), F32)],
        scratch_shapes=[pltpu.VMEM((RWKV_HEADS, HEAD_DIM, HEAD_DIM), F32), pltpu.VMEM((1, SHIFT_PAD), F32)],
        compiler_params=pltpu.CompilerParams(dimension_semantics=("parallel", "arbitrary"),
                                             vmem_limit_bytes=VMEM_LIMIT),
        name="rwkv7",
    )(zs3, shift0, wkv0, *params)


def _attn_kernel(q_ref, qi_ref, misc_ref, k_ref, v_ref, ki_ref, cnt_ref, o_ref,
                 *, n_grp, tq, n_keys, q_block0, pos0, n_valid, topk):
    qb = pl.program_id(1) + q_block0
    nl = n_keys // LANES
    rows = n_grp * tq

    pos = pos0 + qb * tq + lax.broadcasted_iota(jnp.int32, (tq, LANES), 0)
    limit = jnp.minimum((jnp.right_shift(pos, CHUNK_SHIFT) + 1) * CHUNK, n_valid)
    if n_grp > 1:
        limit = jnp.concatenate([limit] * n_grp, axis=0)
    key_id = lax.broadcasted_iota(jnp.int32, (rows, n_keys), 1)
    adm = key_id < jnp.tile(limit, (1, nl))

    parts = []
    for g in range(n_grp):
        qi = qi_ref[g]
        ki = ki_ref[g]
        misc = misc_ref[g]
        sc = None
        for h in range(IDX_HEADS):
            d = lax.dot_general(qi[:, h * LANES:(h + 1) * LANES], ki, NT, preferred_element_type=F32)
            term = misc[:, IDX_DIM + h:IDX_DIM + h + 1] * jnp.maximum(d, 0.0)
            sc = term if sc is None else sc + term
        parts.append(sc)
    score = parts[0] if n_grp == 1 else jnp.concatenate(parts, axis=0)
    sm = jnp.where(adm, score, -jnp.inf)

    cnt_rhs = cnt_ref[...]
    ones_sq = cnt_ref[:, LANES:]

    def count_ge(sm_c, th):
        part = None
        for j in range(nl):
            ind = jnp.where(sm_c[:, j * LANES:(j + 1) * LANES] >= th, 1.0, 0.0)
            part = ind if part is None else part + ind
        return jnp.dot(part.astype(BF16), ones_sq, preferred_element_type=F32)

    row_max = jnp.max(sm, axis=-1, keepdims=True)
    row_min = jnp.min(jnp.where(adm, score, jnp.inf), axis=-1, keepdims=True)
    lo0 = jnp.broadcast_to(row_min, (rows, LANES))
    hi0 = jnp.broadcast_to(2.0 * row_max - row_min + 1.0, (rows, LANES))

    rc = rows // BISECT_CHAINS
    chains = [slice(i * rc, (i + 1) * rc) for i in range(BISECT_CHAINS)]

    def bisect(_, carry):
        out = []
        for ch, (lo, hi) in zip(chains, carry):
            mid = 0.5 * (lo + hi)
            enough = count_ge(sm[ch], mid) >= topk
            out.append((jnp.where(enough, mid, lo), jnp.where(enough, hi, mid)))
        return tuple(out)

    bounds = lax.fori_loop(0, BISECT_ITERS, bisect, tuple((lo0[ch], hi0[ch]) for ch in chains),
                           unroll=BISECT_UNROLL)
    lo = jnp.concatenate([b[0] for b in bounds], axis=0)
    hi = jnp.concatenate([b[1] for b in bounds], axis=0)

    need = topk - count_ge(sm, hi)
    take_all = limit <= topk
    carry = jnp.zeros((rows, LANES), F32)
    bias = []
    for j in range(nl):
        sm_j = sm[:, j * LANES:(j + 1) * LANES]
        above = sm_j >= hi
        group = jnp.logical_and(sm_j >= lo, jnp.logical_not(above))
        pc = jnp.dot(jnp.where(group, 1.0, 0.0).astype(BF16), cnt_rhs, preferred_element_type=F32)
        fill = carry + pc[:, :LANES] <= need
        carry = carry + pc[:, LANES:]
        picked = jnp.logical_or(above, jnp.logical_and(group, fill))
        sel = jnp.logical_and(sm_j > -jnp.inf, jnp.logical_or(take_all, picked))
        bias.append(jnp.where(sel, 0.0, MASK_BIAS))
    bias = jnp.concatenate(bias, axis=1)

    for g in range(n_grp):
        q = q_ref[g]
        kx = k_ref[g]
        vx = v_ref[g]
        bias_g = jnp.concatenate([bias[g * tq:(g + 1) * tq]] * GQA_GROUP, axis=0)
        slabs = []
        for gi in range(N_KV_HEADS):
            kg = kx[:, gi * LANES:(gi + 1) * LANES]
            vg = vx[:, gi * LANES:(gi + 1) * LANES]
            qs = jnp.concatenate([q[:, (gi * GQA_GROUP + j) * LANES:(gi * GQA_GROUP + j + 1) * LANES]
                                  for j in range(GQA_GROUP)], axis=0)
            s = lax.dot_general(qs, kg, NT, preferred_element_type=F32) + bias_g
            m = jnp.max(s, axis=-1, keepdims=True)
            pexp = jnp.exp((s - m).astype(BF16))
            o = jnp.dot(pexp, vg, preferred_element_type=F32)
            o = o / pltpu.roll(o, HEAD_DIM, 1)
            slabs.extend(o[j * tq:(j + 1) * tq] for j in range(GQA_GROUP))
        o_ref[g] = _pack_heads(slabs).astype(o_ref.dtype)


def _attention(q3, qi3, misc3, k3, v3, ki3, cnt_rhs, *, n_grp, tq, n_keys, q_block0, n_qblocks, pos0,
               n_valid, topk):
    b = q3.shape[0]
    qmap = lambda bi, qi_: (bi, qi_ + q_block0, 0)
    kmap = lambda bi, qi_: (bi, 0, 0)
    return pl.pallas_call(
        functools.partial(_attn_kernel, n_grp=n_grp, tq=tq, n_keys=n_keys, q_block0=q_block0, pos0=pos0,
                          n_valid=n_valid, topk=topk),
        grid=(b // n_grp, n_qblocks),
        in_specs=[pl.BlockSpec((n_grp, tq, N_HEADS * LANES), qmap),
                  pl.BlockSpec((n_grp, tq, IDX_HEADS * LANES), qmap),
                  pl.BlockSpec((n_grp, tq, LANES), qmap),
                  pl.BlockSpec((n_grp, n_keys, N_KV_HEADS * LANES), kmap),
                  pl.BlockSpec((n_grp, n_keys, N_KV_HEADS * LANES), kmap),
                  pl.BlockSpec((n_grp, n_keys, LANES), kmap),
                  pl.BlockSpec(cnt_rhs.shape, lambda bi, qi_: (0, 0))],
        out_specs=pl.BlockSpec((n_grp, tq, ATT_WIDTH), lambda bi, qi_: (bi, qi_, 0)),
        out_shape=jax.ShapeDtypeStruct((b, n_qblocks * tq, ATT_WIDTH), BF16),
        compiler_params=pltpu.CompilerParams(dimension_semantics=("parallel", "arbitrary"),
                                             vmem_limit_bytes=VMEM_LIMIT),
        name=f"dsa_attn_{n_keys}",
    )(q3, qi3, misc3, k3, v3, ki3, cnt_rhs)


def _outffn_kernel(x_ref, oa_ref, ob_ref, wo_a_ref, wo_b_ref, g_ref, w1_ref, w2_ref, y_ref, *, ff_chunk):
    mix = (jnp.dot(oa_ref[...], wo_a_ref[...], preferred_element_type=F32)
           + jnp.dot(ob_ref[...], wo_b_ref[...], preferred_element_type=F32))
    h = x_ref[...] + mix
    ms = jnp.mean(h * h, axis=-1, keepdims=True)
    hn = (h * lax.rsqrt(ms + NORM_EPS) * g_ref[...]).astype(BF16)
    acc = h
    for j in range(D_FF // ff_chunk):
        act = jnp.dot(hn, w1_ref[:, j * ff_chunk:(j + 1) * ff_chunk], preferred_element_type=F32)
        act = jnp.square(jnp.maximum(act, 0.0)).astype(BF16)
        acc = acc + jnp.dot(act, w2_ref[j * ff_chunk:(j + 1) * ff_chunk, :], preferred_element_type=F32)
    y_ref[...] = acc


def _outffn(x2d, o_att, o_rwkv, p, tm):
    n = x2d.shape[0]
    row = lambda i: (i, 0)
    const = lambda i: (0, 0)
    params = [p["wo_a"], p["wo_b"], p["norm_ffn"], p["w_ff1"], p["w_ff2"]]
    return pl.pallas_call(
        functools.partial(_outffn_kernel, ff_chunk=1024),
        grid=(n // tm,),
        in_specs=[pl.BlockSpec((tm, D_MODEL), row), pl.BlockSpec((tm, ATT_WIDTH), row),
                  pl.BlockSpec((tm, RWKV_WIDTH), row)]
        + [pl.BlockSpec(a.shape, const, pipeline_mode=pl.Buffered(1)) for a in params],
        out_specs=pl.BlockSpec((tm, D_MODEL), row),
        out_shape=jax.ShapeDtypeStruct((n, D_MODEL), F32),
        compiler_params=pltpu.CompilerParams(dimension_semantics=("parallel",), vmem_limit_bytes=VMEM_LIMIT),
        name="outproj_ffn",
    )(x2d, o_att, o_rwkv, *params)


def _block_diag_ones():
    i = np.arange(ATT_WIDTH)
    return jnp.asarray((i[:, None] // HEAD_DIM) == (i[None, :] // HEAD_DIM), BF16)


def _count_rhs():
    i = np.arange(LANES)
    tri = (i[:, None] <= i[None, :]).astype(np.float32)
    return jnp.asarray(np.concatenate([tri, np.ones((LANES, LANES), np.float32)], axis=1), BF16)


def _prep_params(wl):
    (norm_mix, w_in, q_gain, k_gain, kidx_ln_g, kidx_ln_b, mu_shift, w0, w2, a0, a2, g2, k_k, k_a, r_k,
     ln_x_g, ln_x_b, w_out, norm_ffn, w_ff1, w_ff2) = wl
    row = lambda a: a.reshape(1, -1).astype(F32)
    o_ki = QKV_WIDTH
    o_wi = o_ki + IDX_DIM
    o_zs = o_wi + IDX_HEADS
    pad_cols = lambda a, wdt: jnp.pad(a, ((0, 0), (0, wdt - a.shape[1])))
    lora_rows = lambda a, off: jnp.pad(a, ((off, LORA_PAD - off - a.shape[0]), (0, 0))).astype(BF16)
    return {
        "norm_mix": row(norm_mix),
        "wa": w_in[:, :QKV_WIDTH].astype(BF16),
        "wb": pad_cols(w_in[:, o_ki:o_zs], LANES).astype(BF16),
        "wc": pad_cols(w_in[:, o_zs:], SHIFT_PAD).astype(BF16),
        "q_gain": row(jnp.tile(q_gain, N_HEADS)),
        "k_gain": row(jnp.tile(k_gain, N_KV_HEADS)),
        "ln_g": pad_cols(row(kidx_ln_g), LANES),
        "ln_b": pad_cols(row(kidx_ln_b), LANES),
        "bd": _block_diag_ones(),
        "mu": pad_cols(row(mu_shift), SHIFT_PAD),
        "w0": row(w0), "a0": row(a0),
        "w2": lora_rows(w2, 0), "a2": lora_rows(a2, W_LORA), "g2": lora_rows(g2, W_LORA + A_LORA),
        "k_k": row(k_k), "k_a": row(k_a), "r_k": row(r_k), "lnx_g": row(ln_x_g), "lnx_b": row(ln_x_b),
        "wo_a": w_out[:ATT_WIDTH].astype(BF16), "wo_b": w_out[ATT_WIDTH:].astype(BF16),
        "norm_ffn": row(norm_ffn),
        "w_ff1": w_ff1.astype(BF16), "w_ff2": w_ff2.astype(BF16),
    }


def _layer(x, pos0, k_past, v_past, ki_past, wkv0, shift0, p, cnt_rhs, *, chunk, n_grp, tq, key_bucket):
    b, t, _ = x.shape
    n = b * t
    past = k_past.shape[1]
    x2d = x.reshape(n, D_MODEL)
    tm = min(ROW_TILE, n)
    q, k, v, qi, misc, zs, kbf, vbf, kibf = _inproj(x2d, t, pos0, p, tm)

    shift0p = jnp.pad(shift0, ((0, 0), (0, 0), (0, SHIFT_PAD - SHIFT_WIDTH)))
    zs3 = zs.reshape(b, t, SHIFT_PAD)
    o_rwkv, wkv_new = _rwkv(zs3, shift0p, wkv0, p, chunk)

    n_valid = past + t
    topk = min(TOPK_MAX, n_valid // 4)
    three = lambda a: a.reshape(b, t, a.shape[-1])
    k_new, v_new, ki_new = three(kbf), three(vbf), three(kibf)
    if past:
        n_keys = -(-n_valid // LANES) * LANES

        def cat(old, new, fill):
            old = old.reshape(b, past, -1, HEAD_DIM).astype(BF16)
            old = jnp.pad(old, ((0, 0), (0, 0), (0, 0), (0, LANES - HEAD_DIM)), constant_values=fill)
            both = jnp.concatenate([old.reshape(b, past, -1), new], axis=1)
            return jnp.pad(both, ((0, 0), (0, n_keys - n_valid), (0, 0)))

        k_all, v_all, ki_all = cat(k_past, k_new, 0.0), cat(v_past, v_new, 1.0), cat(ki_past, ki_new, 0.0)
    else:
        k_all, v_all, ki_all = k_new, v_new, ki_new
    q3, qi3, misc3 = three(q), three(qi), three(misc)
    if past:
        o_att = _attention(q3, qi3, misc3, k_all, v_all, ki_all, cnt_rhs, n_grp=n_grp, tq=tq,
                           n_keys=k_all.shape[1], q_block0=0, n_qblocks=t // tq, pos0=pos0,
                           n_valid=n_valid, topk=topk)
    else:
        parts = []
        per = key_bucket // tq
        for j in range(t // key_bucket):
            parts.append(_attention(q3, qi3, misc3, k_all, v_all, ki_all, cnt_rhs, n_grp=n_grp, tq=tq,
                                    n_keys=(j + 1) * key_bucket, q_block0=j * per, n_qblocks=per,
                                    pos0=pos0, n_valid=n_valid, topk=topk))
        o_att = jnp.concatenate(parts, axis=1)

    y = _outffn(x2d, o_att.reshape(n, ATT_WIDTH), o_rwkv.reshape(n, RWKV_WIDTH), p, tm)
    return (y.reshape(b, t, D_MODEL), k.reshape(b, t, N_KV_HEADS, HEAD_DIM),
            v.reshape(b, t, N_KV_HEADS, HEAD_DIM), misc3[:, :, :IDX_DIM], wkv_new,
            zs3[:, t - 1:, :SHIFT_WIDTH])


def kernel(x_prompt, x_sample, cache_k, cache_v, cache_kidx, state_wkv, state_shift, norm_mix, w_in, q_gain, k_gain, kidx_ln_g, kidx_ln_b, mu_shift, w0, w2, a0, a2, g2, k_k, k_a, r_k, ln_x_g, ln_x_b, w_out, norm_ffn, w_ff1, w_ff2):
    depth = norm_mix.shape[0]
    bp, tp = x_prompt.shape[0], x_prompt.shape[1]
    bs, ts = x_sample.shape[0], x_sample.shape[1]
    past_len = cache_k.shape[2]
    cnt_rhs = _count_rhs()
    no_kv = jnp.zeros((bp, 0, KV_WIDTH), F32)
    no_ki = jnp.zeros((bp, 0, IDX_DIM), F32)
    wkv_zero = jnp.zeros((bp, RWKV_HEADS, HEAD_DIM, HEAD_DIM), F32)
    shift_zero = jnp.zeros((bp, 1, SHIFT_WIDTH), F32)

    y_p, y_s = x_prompt, x_sample
    outs_p, outs_s = [], []
    for l in range(depth):
        wl = (norm_mix[l], w_in[l], q_gain[l], k_gain[l], kidx_ln_g[l], kidx_ln_b[l], mu_shift[l],
              w0[l], w2[l], a0[l], a2[l], g2[l], k_k[l], k_a[l], r_k[l], ln_x_g[l], ln_x_b[l],
              w_out[l], norm_ffn[l], w_ff1[l], w_ff2[l])
        p = _prep_params(wl)
        y_p, *rest_p = _layer(y_p, 0, no_kv, no_kv, no_ki, wkv_zero, shift_zero, p, cnt_rhs,
                              chunk=64, n_grp=1, tq=256, key_bucket=512)
        y_s, *rest_s = _layer(y_s, past_len, cache_k[l], cache_v[l], cache_kidx[l], state_wkv[l],
                              state_shift[l], p, cnt_rhs, chunk=ts, n_grp=min(bs, 256 // ts), tq=ts,
                              key_bucket=None)
        outs_p.append(rest_p)
        outs_s.append(rest_s)
    stack = lambda outs, i: jnp.stack([o[i] for o in outs])
    return (y_p, y_s,
            *(stack(outs_p, i) for i in range(5)),
            *(stack(outs_s, i) for i in range(5)))
```

```python
import functools

import numpy as np
import jax
import jax.numpy as jnp
from jax import lax
from jax.experimental import pallas as pl
from jax.experimental.pallas import tpu as pltpu

F32 = jnp.float32
BF16 = jnp.bfloat16

D_MODEL = 1024
CHUNK = 64
HEAD_DIM = 64
ATT_WIDTH = D_MODEL // 2
N_HEADS = ATT_WIDTH // HEAD_DIM
N_KV_HEADS = N_HEADS // 2
GQA_GROUP = N_HEADS // N_KV_HEADS
KV_WIDTH = N_KV_HEADS * HEAD_DIM
ROT_DIM = HEAD_DIM // 4
ROPE_THETA = 500000.0
IDX_HEADS = 8
IDX_DIM = 64
IDX_WIDTH = IDX_HEADS * IDX_DIM
TOPK_MAX = 256
RWKV_WIDTH = D_MODEL - ATT_WIDTH
RWKV_HEADS = RWKV_WIDTH // HEAD_DIM
W_LORA = 32
A_LORA = 32
G_LORA = 96
LORA_WIDTH = W_LORA + A_LORA + G_LORA
SHIFT_WIDTH = 3 * RWKV_WIDTH + LORA_WIDTH
D_FF = 4 * D_MODEL
NORM_EPS = 1e-6
GN_EPS = 64e-5
L2_EPS = 1e-12

LANES = 128
SHIFT_PAD = 3 * RWKV_WIDTH + 2 * LANES
LORA_PAD = SHIFT_PAD - 3 * RWKV_WIDTH
QKV_WIDTH = ATT_WIDTH + 2 * KV_WIDTH + IDX_WIDTH
INV_BLOCK = 16
INV_SHIFT = 4
CHUNK_SHIFT = 6
MASK_BIAS = -1e30
ROW_TILE = 256
RWKV_BATCH = 4
BISECT_ITERS = 32
BISECT_CHAINS = 4
BISECT_UNROLL = 2
VMEM_LIMIT = 56 * 1024 * 1024

NN = (((1,), (0,)), ((), ()))
NT = (((1,), (1,)), ((), ()))
TN = (((0,), (0,)), ((), ()))


def _dot(a, b, dims=NN):
    return lax.dot_general(a.astype(BF16), b.astype(BF16), dims, preferred_element_type=F32)


def _group_sum(x, bd):
    w = x.shape[-1]
    return _dot(x, bd[:w, :w])


def _rope(x, cos, sa, sb):
    w = x.shape[-1]
    reps = w // LANES
    if reps > 1:
        cos, sa, sb = (jnp.tile(t, (1, reps)) for t in (cos, sa, sb))
    half = ROT_DIM // 2
    return x * cos + pltpu.roll(x, w - half, 1) * sa + pltpu.roll(x, half, 1) * sb


def _spread_heads(x, fill):
    low = lax.broadcasted_iota(jnp.int32, (x.shape[0], LANES), 1) < HEAD_DIM
    cols = []
    for c in range(x.shape[1] // LANES):
        pair = x[:, c * LANES:(c + 1) * LANES]
        cols.append(jnp.where(low, pair, fill))
        cols.append(jnp.where(low, pltpu.roll(pair, HEAD_DIM, 1), fill))
    return jnp.concatenate(cols, axis=1)


def _pack_heads(slabs):
    low = lax.broadcasted_iota(jnp.int32, slabs[0].shape, 1) < HEAD_DIM
    cols = [jnp.where(low, slabs[i], pltpu.roll(slabs[i + 1], HEAD_DIM, 1)) for i in range(0, len(slabs), 2)]
    return jnp.concatenate(cols, axis=1)


def _inproj_kernel(x_ref, g_ref, wa_ref, wb_ref, wc_ref, qg_ref, kg_ref, lng_ref, lnb_ref, bd_ref,
                   cos_ref, sa_ref, sb_ref,
                   q_ref, k_ref, v_ref, qi_ref, misc_ref, zs_ref, kbf_ref, vbf_ref, kibf_ref):
    x = x_ref[...]
    ms = jnp.mean(x * x, axis=-1, keepdims=True)
    xn = (x * lax.rsqrt(ms + NORM_EPS) * g_ref[...]).astype(BF16)
    za = jnp.dot(xn, wa_ref[...], preferred_element_type=F32)
    zb = jnp.dot(xn, wb_ref[...], preferred_element_type=F32)
    zs_ref[...] = jnp.dot(xn, wc_ref[...], preferred_element_type=F32)

    cos, sa, sb = cos_ref[...], sa_ref[...], sb_ref[...]
    bd = bd_ref[...]

    zq = za[:, :ATT_WIDTH]
    qn = zq * lax.rsqrt(_group_sum(zq * zq, bd) * (1.0 / HEAD_DIM) + NORM_EPS) * qg_ref[...]
    q_ref[...] = _spread_heads(_rope(qn, cos, sa, sb) * (HEAD_DIM ** -0.5), 0.0).astype(BF16)

    zk = za[:, ATT_WIDTH:ATT_WIDTH + KV_WIDTH]
    kn = zk * lax.rsqrt(_group_sum(zk * zk, bd) * (1.0 / HEAD_DIM) + NORM_EPS) * kg_ref[...]
    k = _rope(kn, cos, sa, sb)
    k_ref[...] = k
    kbf_ref[...] = _spread_heads(k, 0.0).astype(BF16)

    v = za[:, ATT_WIDTH + KV_WIDTH:ATT_WIDTH + 2 * KV_WIDTH]
    v_ref[...] = v
    vbf_ref[...] = _spread_heads(v, 1.0).astype(BF16)

    zqi = za[:, ATT_WIDTH + 2 * KV_WIDTH:]
    qi_ref[...] = _spread_heads(_rope(zqi, cos, sa, sb) * (IDX_DIM ** -0.5), 0.0).astype(BF16)

    lane = lax.broadcasted_iota(jnp.int32, zb.shape, 1)
    is_ki = lane < IDX_DIM
    mu = jnp.sum(jnp.where(is_ki, zb, 0.0), axis=-1, keepdims=True) * (1.0 / IDX_DIM)
    d = jnp.where(is_ki, zb - mu, 0.0)
    var = jnp.sum(d * d, axis=-1, keepdims=True) * (1.0 / IDX_DIM)
    kin = d * lax.rsqrt(var + NORM_EPS) * lng_ref[...] + lnb_ref[...]
    ki = _rope(kin, cos, sa, sb)
    misc_ref[...] = jnp.where(is_ki, ki, zb * (IDX_HEADS ** -0.5))
    kibf_ref[...] = jnp.where(is_ki, ki, 0.0).astype(BF16)


def _rope_tables(pos):
    half = ROT_DIM // 2
    inv = ROPE_THETA ** (-jnp.arange(0, ROT_DIM, 2, dtype=F32) / ROT_DIM)
    ang = pos.astype(F32)[:, None] * inv[None, :]
    cos, sin = jnp.cos(ang), jnp.sin(ang)
    n = pos.shape[0]
    ones = jnp.ones((n, HEAD_DIM - ROT_DIM), F32)
    zeros_h = jnp.zeros((n, half), F32)
    zeros_r = jnp.zeros((n, HEAD_DIM - ROT_DIM), F32)
    c64 = jnp.concatenate([cos, cos, ones], axis=1)
    sa64 = jnp.concatenate([-sin, zeros_h, zeros_r], axis=1)
    sb64 = jnp.concatenate([zeros_h, sin, zeros_r], axis=1)
    rep = LANES // HEAD_DIM
    return tuple(jnp.tile(t, (1, rep)) for t in (c64, sa64, sb64))


def _inproj(x2d, seq_len, pos0, p, tm):
    n = x2d.shape[0]
    tab_rows = max(seq_len, tm)
    pos = pos0 + (jnp.arange(tab_rows, dtype=jnp.int32) % seq_len)
    cos, sa, sb = _rope_tables(pos)
    nblk_t = tab_rows // tm

    def row(i):
        return (i, 0)

    def const(i):
        return (0, 0)

    def tab(i):
        return (i % nblk_t, 0)

    full = lambda a: pl.BlockSpec(a.shape, const)
    outs = [
        (N_HEADS * LANES, BF16), (KV_WIDTH, F32), (KV_WIDTH, F32), (IDX_HEADS * LANES, BF16), (LANES, F32),
        (SHIFT_PAD, F32), (N_KV_HEADS * LANES, BF16), (N_KV_HEADS * LANES, BF16), (LANES, BF16),
    ]
    params = [p["norm_mix"], p["wa"], p["wb"], p["wc"], p["q_gain"], p["k_gain"], p["ln_g"], p["ln_b"], p["bd"]]
    return pl.pallas_call(
        _inproj_kernel,
        grid=(n // tm,),
        in_specs=[pl.BlockSpec((tm, D_MODEL), row)] + [full(a) for a in params]
        + [pl.BlockSpec((tm, LANES), tab)] * 3,
        out_specs=[pl.BlockSpec((tm, w), row) for w, _ in outs],
        out_shape=[jax.ShapeDtypeStruct((n, w), dt) for w, dt in outs],
        compiler_params=pltpu.CompilerParams(dimension_semantics=("parallel",), vmem_limit_bytes=VMEM_LIMIT),
        name="inproj",
    )(x2d, *params, cos, sa, sb)


def _solve_unit_lower(lows, rhss, eye, blk):
    c = lows[0].shape[0]
    d = [jnp.where(blk, l, 0.0) for l in lows]
    e = [l - dh for l, dh in zip(lows, d)]
    dk = [_dot(dh, dh) for dh in d]
    p = [_dot(eye - dh, eye + d2) for dh, d2 in zip(d, dk)]
    span = 4
    while span < INV_BLOCK:
        dk = [_dot(m, m) for m in dk]
        p = [_dot(ph, eye + m) for ph, m in zip(p, dk)]
        span *= 2
    x = [_dot(ph, r) for ph, r in zip(p, rhss)]
    nb = c // INV_BLOCK
    if nb > 1:
        nk = [_dot(ph, eh) for ph, eh in zip(p, e)]
        x = [_dot(eye - m, xh) for m, xh in zip(nk, x)]
        order = 2
        while order < nb:
            nk = [_dot(m, m) for m in nk]
            x = [_dot(eye + m, xh) for m, xh in zip(nk, x)]
            order *= 2
    return x


def _rwkv_kernel(zs_ref, shift0_ref, wkv0_ref, mu_ref, w0_ref, w2_ref, a0_ref, a2_ref, g2_ref,
                 kk_ref, ka_ref, rk_ref, lng_ref, lnb_ref, bd_ref,
                 o_ref, sout_ref, s_scr, prev_scr, *, chunk):
    c = chunk
    nb = zs_ref.shape[0]
    ci = pl.program_id(1)
    w = RWKV_WIDTH

    @pl.when(ci == 0)
    def _():
        s_scr[...] = wkv0_ref[...]
        prev_scr[...] = shift0_ref[...]

    row = lax.broadcasted_iota(jnp.int32, (c, c), 0)
    col = lax.broadcasted_iota(jnp.int32, (c, c), 1)
    tri_incl = row >= col
    tri_strict = row > col
    eye = jnp.where(row == col, 1.0, 0.0)
    blk = jnp.right_shift(row, INV_SHIFT) == jnp.right_shift(col, INV_SHIFT)
    tri = jnp.where(tri_incl, 1.0, 0.0).astype(BF16)
    mu = mu_ref[...]
    bd = bd_ref[...]
    sls = [slice(h * HEAD_DIM, (h + 1) * HEAD_DIM) for h in range(RWKV_HEADS)]

    ar, kb, vh, g_last, gates, bonus_rk = [], [], [], [], [], []
    for j in range(nb):
        z = zs_ref[j]
        prev_row = prev_scr[j]

        def mixed(lo, hi):
            zp = z[:, lo:hi]
            rolled = pltpu.roll(zp, 1, 0)
            first = lax.broadcasted_iota(jnp.int32, zp.shape, 0) == 0
            prv = jnp.where(first, prev_row[:, lo:hi], rolled)
            return zp + mu[:, lo:hi] * (prv - zp)

        r = mixed(0, w)
        k = mixed(w, 2 * w)
        v = mixed(2 * w, 3 * w)
        tail = mixed(3 * w, SHIFT_PAD)
        prev_scr[j] = z[c - 1:c, :]

        lw = w0_ref[...] + _dot(jnp.tanh(tail), w2_ref[...])
        nlw = -lw
        softplus = jnp.maximum(nlw, 0.0) + jnp.log(1.0 + jnp.exp(-jnp.abs(nlw)))
        logd = -jnp.exp(-softplus - 0.5)
        a = 1.0 / (1.0 + jnp.exp(-(a0_ref[...] + _dot(tail, a2_ref[...]))))
        gates.append(_dot(1.0 / (1.0 + jnp.exp(-tail)), g2_ref[...]))

        kk = k * kk_ref[...]
        kk = kk / jnp.maximum(jnp.sqrt(_group_sum(kk * kk, bd)), L2_EPS)
        k = k * (1.0 + (a - 1.0) * ka_ref[...])
        b = kk * a

        hi = logd.astype(BF16)
        rem = logd - hi.astype(F32)
        mid = rem.astype(BF16)
        lo = (rem - mid.astype(F32)).astype(BF16)
        lg = (jnp.dot(tri, hi, preferred_element_type=F32) + jnp.dot(tri, mid, preferred_element_type=F32)
              + jnp.dot(tri, lo, preferred_element_type=F32))
        g = jnp.exp(lg)
        ginv = jnp.exp(-lg)
        a_m = jnp.exp(lg - logd) * kk
        b_m = b * ginv
        k_m = k * ginv
        p_m = g * r
        bonus_rk.append(r * k * rk_ref[...])
        ar += [jnp.concatenate([a_m[:, sl], p_m[:, sl]], axis=0).astype(BF16) for sl in sls]
        kb += [jnp.concatenate([k_m[:, sl], b_m[:, sl]], axis=0).astype(BF16) for sl in sls]
        vh += [v[:, sl] for sl in sls]
        g_last += [g[c - 1:c, sl] for sl in sls]

    chains = range(nb * RWKV_HEADS)
    s_h = [s_scr[j, h] for j in range(nb) for h in range(RWKV_HEADS)]
    gm = [_dot(ar[i], kb[i], NT) for i in chains]
    a_s = [_dot(ar[i], s_h[i], NT) for i in chains]
    rhs = [a_s[i][:c] + _dot(jnp.where(tri_strict, gm[i][:c, :c], 0.0), vh[i]) for i in chains]
    u = _solve_unit_lower([jnp.where(tri_strict, gm[i][:c, c:], 0.0) for i in chains], rhs, eye, blk)
    ys, bonus = [], []
    for i in chains:
        j, h = divmod(i, RWKV_HEADS)
        n_pk = jnp.where(tri_incl, gm[i][c:, :c], 0.0)
        n_pb = jnp.where(tri_incl, gm[i][c:, c:], 0.0)
        y = a_s[i][c:] + _dot(jnp.concatenate([n_pk, -n_pb], axis=1), jnp.concatenate([vh[i], u[i]], axis=0))
        s_scr[j, h] = (s_h[i] + _dot(jnp.concatenate([vh[i], -u[i]], axis=0), kb[i], TN)) * g_last[i]
        m = jnp.mean(y, axis=-1, keepdims=True)
        yc = y - m
        var = jnp.mean(yc * yc, axis=-1, keepdims=True)
        ys.append(yc * lax.rsqrt(var + GN_EPS))
        bonus.append(jnp.sum(bonus_rk[j][:, sls[h]], axis=-1, keepdims=True) * vh[i])

    for j in range(nb):
        hs = slice(j * RWKV_HEADS, (j + 1) * RWKV_HEADS)
        yn = jnp.concatenate(ys[hs], axis=1) * lng_ref[...] + lnb_ref[...]
        o_ref[j] = ((yn + jnp.concatenate(bonus[hs], axis=1)) * gates[j]).astype(o_ref.dtype)

    @pl.when(ci == pl.num_programs(1) - 1)
    def _():
        sout_ref[...] = s_scr[...]


def _rwkv(zs3, shift0, wkv0, p, chunk, nb):
    b, t, _ = zs3.shape
    const = lambda bi, ci: (0, 0)
    full = lambda a: pl.BlockSpec(a.shape, const)
    params = [p["mu"], p["w0"], p["w2"], p["a0"], p["a2"], p["g2"], p["k_k"], p["k_a"], p["r_k"],
              p["lnx_g"], p["lnx_b"], p["bd"]]
    state_spec = pl.BlockSpec((nb, RWKV_HEADS, HEAD_DIM, HEAD_DIM), lambda bi, ci: (bi, 0, 0, 0))
    return pl.pallas_call(
        functools.partial(_rwkv_kernel, chunk=chunk),
        grid=(b // nb, t // chunk),
        in_specs=[pl.BlockSpec((nb, chunk, SHIFT_PAD), lambda bi, ci: (bi, ci, 0)),
                  pl.BlockSpec((nb, 1, SHIFT_PAD), lambda bi, ci: (bi, 0, 0)),
                  state_spec] + [full(a) for a in params],
        out_specs=[pl.BlockSpec((nb, chunk, RWKV_WIDTH), lambda bi, ci: (bi, ci, 0)), state_spec],
        out_shape=[jax.ShapeDtypeStruct((b, t, RWKV_WIDTH), BF16),
                   jax.ShapeDtypeStruct((b, RWKV_HEADS, HEAD_DIM, HEAD_DIM), F32)],
        scratch_shapes=[pltpu.VMEM((nb, RWKV_HEADS, HEAD_DIM, HEAD_DIM), F32),
                        pltpu.VMEM((nb, 1, SHIFT_PAD), F32)],
        compiler_params=pltpu.CompilerParams(dimension_semantics=("parallel", "arbitrary"),
                                             vmem_limit_bytes=VMEM_LIMIT),
        name="rwkv7",
    )(zs3, shift0, wkv0, *params)


def _attn_kernel(q_ref, qi_ref, misc_ref, k_ref, v_ref, ki_ref, cnt_ref, o_ref,
                 *, n_grp, tq, n_keys, q_block0, pos0, n_valid, topk):
    qb = pl.program_id(1) + q_block0
    nl = n_keys // LANES
    rows = n_grp * tq

    pos = pos0 + qb * tq + lax.broadcasted_iota(jnp.int32, (tq, LANES), 0)
    limit = jnp.minimum((jnp.right_shift(pos, CHUNK_SHIFT) + 1) * CHUNK, n_valid)
    if n_grp > 1:
        limit = jnp.concatenate([limit] * n_grp, axis=0)
    key_id = lax.broadcasted_iota(jnp.int32, (rows, n_keys), 1)
    adm = key_id < jnp.tile(limit, (1, nl))

    parts = []
    for g in range(n_grp):
        qi = qi_ref[g]
        ki = ki_ref[g]
        misc = misc_ref[g]
        sc = None
        for h in range(IDX_HEADS):
            d = lax.dot_general(qi[:, h * LANES:(h + 1) * LANES], ki, NT, preferred_element_type=F32)
            term = misc[:, IDX_DIM + h:IDX_DIM + h + 1] * jnp.maximum(d, 0.0)
            sc = term if sc is None else sc + term
        parts.append(sc)
    score = parts[0] if n_grp == 1 else jnp.concatenate(parts, axis=0)
    sm = jnp.where(adm, score, -jnp.inf)

    cnt_rhs = cnt_ref[...]
    ones_sq = cnt_ref[:, LANES:]

    def count_ge(sm_c, th):
        part = None
        for j in range(nl):
            ind = jnp.where(sm_c[:, j * LANES:(j + 1) * LANES] >= th, 1.0, 0.0)
            part = ind if part is None else part + ind
        return jnp.dot(part.astype(BF16), ones_sq, preferred_element_type=F32)

    row_max = jnp.max(sm, axis=-1, keepdims=True)
    row_min = jnp.min(jnp.where(adm, score, jnp.inf), axis=-1, keepdims=True)
    lo0 = jnp.broadcast_to(row_min, (rows, LANES))
    hi0 = jnp.broadcast_to(2.0 * row_max - row_min + 1.0, (rows, LANES))

    rc = rows // BISECT_CHAINS
    chains = [slice(i * rc, (i + 1) * rc) for i in range(BISECT_CHAINS)]

    def bisect(_, carry):
        out = []
        for ch, (lo, hi) in zip(chains, carry):
            mid = 0.5 * (lo + hi)
            enough = count_ge(sm[ch], mid) >= topk
            out.append((jnp.where(enough, mid, lo), jnp.where(enough, hi, mid)))
        return tuple(out)

    bounds = lax.fori_loop(0, BISECT_ITERS, bisect, tuple((lo0[ch], hi0[ch]) for ch in chains),
                           unroll=BISECT_UNROLL)
    lo = jnp.concatenate([b[0] for b in bounds], axis=0)
    hi = jnp.concatenate([b[1] for b in bounds], axis=0)

    need = topk - count_ge(sm, hi)
    take_all = limit <= topk
    carry = jnp.zeros((rows, LANES), F32)
    bias = []
    for j in range(nl):
        sm_j = sm[:, j * LANES:(j + 1) * LANES]
        above = sm_j >= hi
        group = jnp.logical_and(sm_j >= lo, jnp.logical_not(above))
        pc = jnp.dot(jnp.where(group, 1.0, 0.0).astype(BF16), cnt_rhs, preferred_element_type=F32)
        fill = carry + pc[:, :LANES] <= need
        carry = carry + pc[:, LANES:]
        picked = jnp.logical_or(above, jnp.logical_and(group, fill))
        sel = jnp.logical_and(sm_j > -jnp.inf, jnp.logical_or(take_all, picked))
        bias.append(jnp.where(sel, 0.0, MASK_BIAS))
    bias = jnp.concatenate(bias, axis=1)

    for g in range(n_grp):
        q = q_ref[g]
        kx = k_ref[g]
        vx = v_ref[g]
        bias_g = jnp.concatenate([bias[g * tq:(g + 1) * tq]] * GQA_GROUP, axis=0)
        slabs = []
        for gi in range(N_KV_HEADS):
            kg = kx[:, gi * LANES:(gi + 1) * LANES]
            vg = vx[:, gi * LANES:(gi + 1) * LANES]
            qs = jnp.concatenate([q[:, (gi * GQA_GROUP + j) * LANES:(gi * GQA_GROUP + j + 1) * LANES]
                                  for j in range(GQA_GROUP)], axis=0)
            s = lax.dot_general(qs, kg, NT, preferred_element_type=F32) + bias_g
            m = jnp.max(s, axis=-1, keepdims=True)
            pexp = jnp.exp((s - m).astype(BF16))
            o = jnp.dot(pexp, vg, preferred_element_type=F32)
            o = o / pltpu.roll(o, HEAD_DIM, 1)
            slabs.extend(o[j * tq:(j + 1) * tq] for j in range(GQA_GROUP))
        o_ref[g] = _pack_heads(slabs).astype(o_ref.dtype)


def _attention(q3, qi3, misc3, k3, v3, ki3, cnt_rhs, *, n_grp, tq, n_keys, q_block0, n_qblocks, pos0,
               n_valid, topk):
    b = q3.shape[0]
    qmap = lambda bi, qi_: (bi, qi_ + q_block0, 0)
    kmap = lambda bi, qi_: (bi, 0, 0)
    return pl.pallas_call(
        functools.partial(_attn_kernel, n_grp=n_grp, tq=tq, n_keys=n_keys, q_block0=q_block0, pos0=pos0,
                          n_valid=n_valid, topk=topk),
        grid=(b // n_grp, n_qblocks),
        in_specs=[pl.BlockSpec((n_grp, tq, N_HEADS * LANES), qmap),
                  pl.BlockSpec((n_grp, tq, IDX_HEADS * LANES), qmap),
                  pl.BlockSpec((n_grp, tq, LANES), qmap),
                  pl.BlockSpec((n_grp, n_keys, N_KV_HEADS * LANES), kmap),
                  pl.BlockSpec((n_grp, n_keys, N_KV_HEADS * LANES), kmap),
                  pl.BlockSpec((n_grp, n_keys, LANES), kmap),
                  pl.BlockSpec(cnt_rhs.shape, lambda bi, qi_: (0, 0))],
        out_specs=pl.BlockSpec((n_grp, tq, ATT_WIDTH), lambda bi, qi_: (bi, qi_, 0)),
        out_shape=jax.ShapeDtypeStruct((b, n_qblocks * tq, ATT_WIDTH), BF16),
        compiler_params=pltpu.CompilerParams(dimension_semantics=("parallel", "arbitrary"),
                                             vmem_limit_bytes=VMEM_LIMIT),
        name=f"dsa_attn_{n_keys}",
    )(q3, qi3, misc3, k3, v3, ki3, cnt_rhs)


def _outffn_kernel(x_ref, oa_ref, ob_ref, wo_a_ref, wo_b_ref, g_ref, w1_ref, w2_ref, y_ref, *, ff_chunk):
    mix = (jnp.dot(oa_ref[...], wo_a_ref[...], preferred_element_type=F32)
           + jnp.dot(ob_ref[...], wo_b_ref[...], preferred_element_type=F32))
    h = x_ref[...] + mix
    ms = jnp.mean(h * h, axis=-1, keepdims=True)
    hn = (h * lax.rsqrt(ms + NORM_EPS) * g_ref[...]).astype(BF16)
    acc = h
    for j in range(D_FF // ff_chunk):
        act = jnp.dot(hn, w1_ref[:, j * ff_chunk:(j + 1) * ff_chunk], preferred_element_type=F32)
        act = jnp.square(jnp.maximum(act, 0.0)).astype(BF16)
        acc = acc + jnp.dot(act, w2_ref[j * ff_chunk:(j + 1) * ff_chunk, :], preferred_element_type=F32)
    y_ref[...] = acc


def _outffn(x2d, o_att, o_rwkv, p, tm):
    n = x2d.shape[0]
    row = lambda i: (i, 0)
    const = lambda i: (0, 0)
    params = [p["wo_a"], p["wo_b"], p["norm_ffn"], p["w_ff1"], p["w_ff2"]]
    return pl.pallas_call(
        functools.partial(_outffn_kernel, ff_chunk=1024),
        grid=(n // tm,),
        in_specs=[pl.BlockSpec((tm, D_MODEL), row), pl.BlockSpec((tm, ATT_WIDTH), row),
                  pl.BlockSpec((tm, RWKV_WIDTH), row)]
        + [pl.BlockSpec(a.shape, const, pipeline_mode=pl.Buffered(1)) for a in params],
        out_specs=pl.BlockSpec((tm, D_MODEL), row),
        out_shape=jax.ShapeDtypeStruct((n, D_MODEL), F32),
        compiler_params=pltpu.CompilerParams(dimension_semantics=("parallel",), vmem_limit_bytes=VMEM_LIMIT),
        name="outproj_ffn",
    )(x2d, o_att, o_rwkv, *params)


def _block_diag_ones():
    i = np.arange(ATT_WIDTH)
    return jnp.asarray((i[:, None] // HEAD_DIM) == (i[None, :] // HEAD_DIM), BF16)


def _count_rhs():
    i = np.arange(LANES)
    tri = (i[:, None] <= i[None, :]).astype(np.float32)
    return jnp.asarray(np.concatenate([tri, np.ones((LANES, LANES), np.float32)], axis=1), BF16)


def _prep_params(wl):
    (norm_mix, w_in, q_gain, k_gain, kidx_ln_g, kidx_ln_b, mu_shift, w0, w2, a0, a2, g2, k_k, k_a, r_k,
     ln_x_g, ln_x_b, w_out, norm_ffn, w_ff1, w_ff2) = wl
    row = lambda a: a.reshape(1, -1).astype(F32)
    o_ki = QKV_WIDTH
    o_wi = o_ki + IDX_DIM
    o_zs = o_wi + IDX_HEADS
    pad_cols = lambda a, wdt: jnp.pad(a, ((0, 0), (0, wdt - a.shape[1])))
    lora_rows = lambda a, off: jnp.pad(a, ((off, LORA_PAD - off - a.shape[0]), (0, 0))).astype(BF16)
    return {
        "norm_mix": row(norm_mix),
        "wa": w_in[:, :QKV_WIDTH].astype(BF16),
        "wb": pad_cols(w_in[:, o_ki:o_zs], LANES).astype(BF16),
        "wc": pad_cols(w_in[:, o_zs:], SHIFT_PAD).astype(BF16),
        "q_gain": row(jnp.tile(q_gain, N_HEADS)),
        "k_gain": row(jnp.tile(k_gain, N_KV_HEADS)),
        "ln_g": pad_cols(row(kidx_ln_g), LANES),
        "ln_b": pad_cols(row(kidx_ln_b), LANES),
        "bd": _block_diag_ones(),
        "mu": pad_cols(row(mu_shift), SHIFT_PAD),
        "w0": row(w0), "a0": row(a0),
        "w2": lora_rows(w2, 0), "a2": lora_rows(a2, W_LORA), "g2": lora_rows(g2, W_LORA + A_LORA),
        "k_k": row(k_k), "k_a": row(k_a), "r_k": row(r_k), "lnx_g": row(ln_x_g), "lnx_b": row(ln_x_b),
        "wo_a": w_out[:ATT_WIDTH].astype(BF16), "wo_b": w_out[ATT_WIDTH:].astype(BF16),
        "norm_ffn": row(norm_ffn),
        "w_ff1": w_ff1.astype(BF16), "w_ff2": w_ff2.astype(BF16),
    }


def _layer(x, pos0, k_past, v_past, ki_past, wkv0, shift0, p, cnt_rhs, *, chunk, n_grp, tq, key_bucket):
    b, t, _ = x.shape
    n = b * t
    past = k_past.shape[1]
    x2d = x.reshape(n, D_MODEL)
    tm = min(ROW_TILE, n)
    q, k, v, qi, misc, zs, kbf, vbf, kibf = _inproj(x2d, t, pos0, p, tm)

    shift0p = jnp.pad(shift0, ((0, 0), (0, 0), (0, SHIFT_PAD - SHIFT_WIDTH)))
    zs3 = zs.reshape(b, t, SHIFT_PAD)
    o_rwkv, wkv_new = _rwkv(zs3, shift0p, wkv0, p, chunk, min(b, RWKV_BATCH))

    n_valid = past + t
    topk = min(TOPK_MAX, n_valid // 4)
    three = lambda a: a.reshape(b, t, a.shape[-1])
    k_new, v_new, ki_new = three(kbf), three(vbf), three(kibf)
    if past:
        n_keys = -(-n_valid // LANES) * LANES

        def cat(old, new, fill):
            old = old.reshape(b, past, -1, HEAD_DIM).astype(BF16)
            old = jnp.pad(old, ((0, 0), (0, 0), (0, 0), (0, LANES - HEAD_DIM)), constant_values=fill)
            both = jnp.concatenate([old.reshape(b, past, -1), new], axis=1)
            return jnp.pad(both, ((0, 0), (0, n_keys - n_valid), (0, 0)))

        k_all, v_all, ki_all = cat(k_past, k_new, 0.0), cat(v_past, v_new, 1.0), cat(ki_past, ki_new, 0.0)
    else:
        k_all, v_all, ki_all = k_new, v_new, ki_new
    q3, qi3, misc3 = three(q), three(qi), three(misc)
    if past:
        o_att = _attention(q3, qi3, misc3, k_all, v_all, ki_all, cnt_rhs, n_grp=n_grp, tq=tq,
                           n_keys=k_all.shape[1], q_block0=0, n_qblocks=t // tq, pos0=pos0,
                           n_valid=n_valid, topk=topk)
    else:
        parts = []
        per = key_bucket // tq
        for j in range(t // key_bucket):
            parts.append(_attention(q3, qi3, misc3, k_all, v_all, ki_all, cnt_rhs, n_grp=n_grp, tq=tq,
                                    n_keys=(j + 1) * key_bucket, q_block0=j * per, n_qblocks=per,
                                    pos0=pos0, n_valid=n_valid, topk=topk))
        o_att = jnp.concatenate(parts, axis=1)

    y = _outffn(x2d, o_att.reshape(n, ATT_WIDTH), o_rwkv.reshape(n, RWKV_WIDTH), p, tm)
    return (y.reshape(b, t, D_MODEL), k.reshape(b, t, N_KV_HEADS, HEAD_DIM),
            v.reshape(b, t, N_KV_HEADS, HEAD_DIM), misc3[:, :, :IDX_DIM], wkv_new,
            zs3[:, t - 1:, :SHIFT_WIDTH])


def kernel(x_prompt, x_sample, cache_k, cache_v, cache_kidx, state_wkv, state_shift, norm_mix, w_in, q_gain, k_gain, kidx_ln_g, kidx_ln_b, mu_shift, w0, w2, a0, a2, g2, k_k, k_a, r_k, ln_x_g, ln_x_b, w_out, norm_ffn, w_ff1, w_ff2):
    depth = norm_mix.shape[0]
    bp, tp = x_prompt.shape[0], x_prompt.shape[1]
    bs, ts = x_sample.shape[0], x_sample.shape[1]
    past_len = cache_k.shape[2]
    cnt_rhs = _count_rhs()
    no_kv = jnp.zeros((bp, 0, KV_WIDTH), F32)
    no_ki = jnp.zeros((bp, 0, IDX_DIM), F32)
    wkv_zero = jnp.zeros((bp, RWKV_HEADS, HEAD_DIM, HEAD_DIM), F32)
    shift_zero = jnp.zeros((bp, 1, SHIFT_WIDTH), F32)

    y_p, y_s = x_prompt, x_sample
    outs_p, outs_s = [], []
    for l in range(depth):
        wl = (norm_mix[l], w_in[l], q_gain[l], k_gain[l], kidx_ln_g[l], kidx_ln_b[l], mu_shift[l],
              w0[l], w2[l], a0[l], a2[l], g2[l], k_k[l], k_a[l], r_k[l], ln_x_g[l], ln_x_b[l],
              w_out[l], norm_ffn[l], w_ff1[l], w_ff2[l])
        p = _prep_params(wl)
        y_p, *rest_p = _layer(y_p, 0, no_kv, no_kv, no_ki, wkv_zero, shift_zero, p, cnt_rhs,
                              chunk=64, n_grp=1, tq=256, key_bucket=512)
        y_s, *rest_s = _layer(y_s, past_len, cache_k[l], cache_v[l], cache_kidx[l], state_wkv[l],
                              state_shift[l], p, cnt_rhs, chunk=ts, n_grp=min(bs, 256 // ts), tq=ts,
                              key_bucket=None)
        outs_p.append(rest_p)
        outs_s.append(rest_s)
    stack = lambda outs, i: jnp.stack([o[i] for o in outs])
    return (y_p, y_s,
            *(stack(outs_p, i) for i in range(5)),
            *(stack(outs_s, i) for i in range(5)))
```

```python
import functools

import numpy as np
import jax
import jax.numpy as jnp
from jax import lax
from jax.experimental import pallas as pl
from jax.experimental.pallas import tpu as pltpu

F32 = jnp.float32
BF16 = jnp.bfloat16

D_MODEL = 1024
CHUNK = 64
HEAD_DIM = 64
ATT_WIDTH = D_MODEL // 2
N_HEADS = ATT_WIDTH // HEAD_DIM
N_KV_HEADS = N_HEADS // 2
GQA_GROUP = N_HEADS // N_KV_HEADS
KV_WIDTH = N_KV_HEADS * HEAD_DIM
ROT_DIM = HEAD_DIM // 4
ROPE_THETA = 500000.0
IDX_HEADS = 8
IDX_DIM = 64
IDX_WIDTH = IDX_HEADS * IDX_DIM
TOPK_MAX = 256
RWKV_WIDTH = D_MODEL - ATT_WIDTH
RWKV_HEADS = RWKV_WIDTH // HEAD_DIM
W_LORA = 32
A_LORA = 32
G_LORA = 96
LORA_WIDTH = W_LORA + A_LORA + G_LORA
SHIFT_WIDTH = 3 * RWKV_WIDTH + LORA_WIDTH
D_FF = 4 * D_MODEL
NORM_EPS = 1e-6
GN_EPS = 64e-5
L2_EPS = 1e-12

LANES = 128
SHIFT_PAD = 3 * RWKV_WIDTH + 2 * LANES
LORA_PAD = SHIFT_PAD - 3 * RWKV_WIDTH
QKV_WIDTH = ATT_WIDTH + 2 * KV_WIDTH + IDX_WIDTH
INV_BLOCK = 16
INV_SHIFT = 4
CHUNK_SHIFT = 6
MASK_BIAS = -1e30
ROW_TILE = 256
RWKV_BATCH = 4
BISECT_ITERS = 32
BISECT_CHAINS = 4
BISECT_UNROLL = 2
VMEM_LIMIT = 56 * 1024 * 1024

NN = (((1,), (0,)), ((), ()))
NT = (((1,), (1,)), ((), ()))
TN = (((0,), (0,)), ((), ()))


def _dot(a, b, dims=NN):
    return lax.dot_general(a.astype(BF16), b.astype(BF16), dims, preferred_element_type=F32)


def _group_sum(x, bd):
    w = x.shape[-1]
    return _dot(x, bd[:w, :w])


def _rope(x, cos, sa, sb):
    w = x.shape[-1]
    reps = w // LANES
    if reps > 1:
        cos, sa, sb = (jnp.tile(t, (1, reps)) for t in (cos, sa, sb))
    half = ROT_DIM // 2
    return x * cos + pltpu.roll(x, w - half, 1) * sa + pltpu.roll(x, half, 1) * sb


def _spread_heads(x, fill):
    low = lax.broadcasted_iota(jnp.int32, (x.shape[0], LANES), 1) < HEAD_DIM
    cols = []
    for c in range(x.shape[1] // LANES):
        pair = x[:, c * LANES:(c + 1) * LANES]
        cols.append(jnp.where(low, pair, fill))
        cols.append(jnp.where(low, pltpu.roll(pair, HEAD_DIM, 1), fill))
    return jnp.concatenate(cols, axis=1)


def _pack_heads(slabs):
    low = lax.broadcasted_iota(jnp.int32, slabs[0].shape, 1) < HEAD_DIM
    cols = [jnp.where(low, slabs[i], pltpu.roll(slabs[i + 1], HEAD_DIM, 1)) for i in range(0, len(slabs), 2)]
    return jnp.concatenate(cols, axis=1)


def _inproj_kernel(x_ref, g_ref, wa_ref, wb_ref, wc_ref, qg_ref, kg_ref, lng_ref, lnb_ref, bd_ref,
                   cos_ref, sa_ref, sb_ref,
                   q_ref, k_ref, v_ref, qi_ref, misc_ref, zs_ref, kbf_ref, vbf_ref, kibf_ref):
    x = x_ref[...]
    ms = jnp.mean(x * x, axis=-1, keepdims=True)
    xn = (x * lax.rsqrt(ms + NORM_EPS) * g_ref[...]).astype(BF16)
    za = jnp.dot(xn, wa_ref[...], preferred_element_type=F32)
    zb = jnp.dot(xn, wb_ref[...], preferred_element_type=F32)
    zs_ref[...] = jnp.dot(xn, wc_ref[...], preferred_element_type=F32)

    cos, sa, sb = cos_ref[...], sa_ref[...], sb_ref[...]
    bd = bd_ref[...]

    zq = za[:, :ATT_WIDTH]
    qn = zq * lax.rsqrt(_group_sum(zq * zq, bd) * (1.0 / HEAD_DIM) + NORM_EPS) * qg_ref[...]
    q_ref[...] = _spread_heads(_rope(qn, cos, sa, sb) * (HEAD_DIM ** -0.5), 0.0).astype(BF16)

    zk = za[:, ATT_WIDTH:ATT_WIDTH + KV_WIDTH]
    kn = zk * lax.rsqrt(_group_sum(zk * zk, bd) * (1.0 / HEAD_DIM) + NORM_EPS) * kg_ref[...]
    k = _rope(kn, cos, sa, sb)
    k_ref[...] = k
    kbf_ref[...] = _spread_heads(k, 0.0).astype(BF16)

    v = za[:, ATT_WIDTH + KV_WIDTH:ATT_WIDTH + 2 * KV_WIDTH]
    v_ref[...] = v
    vbf_ref[...] = _spread_heads(v, 1.0).astype(BF16)

    zqi = za[:, ATT_WIDTH + 2 * KV_WIDTH:]
    qi_ref[...] = _spread_heads(_rope(zqi, cos, sa, sb) * (IDX_DIM ** -0.5), 0.0).astype(BF16)

    lane = lax.broadcasted_iota(jnp.int32, zb.shape, 1)
    is_ki = lane < IDX_DIM
    mu = jnp.sum(jnp.where(is_ki, zb, 0.0), axis=-1, keepdims=True) * (1.0 / IDX_DIM)
    d = jnp.where(is_ki, zb - mu, 0.0)
    var = jnp.sum(d * d, axis=-1, keepdims=True) * (1.0 / IDX_DIM)
    kin = d * lax.rsqrt(var + NORM_EPS) * lng_ref[...] + lnb_ref[...]
    ki = _rope(kin, cos, sa, sb)
    misc_ref[...] = jnp.where(is_ki, ki, zb * (IDX_HEADS ** -0.5))
    kibf_ref[...] = jnp.where(is_ki, ki, 0.0).astype(BF16)


def _rope_tables(pos):
    half = ROT_DIM // 2
    inv = ROPE_THETA ** (-jnp.arange(0, ROT_DIM, 2, dtype=F32) / ROT_DIM)
    ang = pos.astype(F32)[:, None] * inv[None, :]
    cos, sin = jnp.cos(ang), jnp.sin(ang)
    n = pos.shape[0]
    ones = jnp.ones((n, HEAD_DIM - ROT_DIM), F32)
    zeros_h = jnp.zeros((n, half), F32)
    zeros_r = jnp.zeros((n, HEAD_DIM - ROT_DIM), F32)
    c64 = jnp.concatenate([cos, cos, ones], axis=1)
    sa64 = jnp.concatenate([-sin, zeros_h, zeros_r], axis=1)
    sb64 = jnp.concatenate([zeros_h, sin, zeros_r], axis=1)
    rep = LANES // HEAD_DIM
    return tuple(jnp.tile(t, (1, rep)) for t in (c64, sa64, sb64))


def _inproj(x2d, seq_len, pos0, p, tm):
    n = x2d.shape[0]
    tab_rows = max(seq_len, tm)
    pos = pos0 + (jnp.arange(tab_rows, dtype=jnp.int32) % seq_len)
    cos, sa, sb = _rope_tables(pos)
    nblk_t = tab_rows // tm

    def row(i):
        return (i, 0)

    def const(i):
        return (0, 0)

    def tab(i):
        return (i % nblk_t, 0)

    full = lambda a: pl.BlockSpec(a.shape, const)
    outs = [
        (N_HEADS * LANES, BF16), (KV_WIDTH, F32), (KV_WIDTH, F32), (IDX_HEADS * LANES, BF16), (LANES, F32),
        (SHIFT_PAD, F32), (N_KV_HEADS * LANES, BF16), (N_KV_HEADS * LANES, BF16), (LANES, BF16),
    ]
    params = [p["norm_mix"], p["wa"], p["wb"], p["wc"], p["q_gain"], p["k_gain"], p["ln_g"], p["ln_b"], p["bd"]]
    return pl.pallas_call(
        _inproj_kernel,
        grid=(n // tm,),
        in_specs=[pl.BlockSpec((tm, D_MODEL), row)] + [full(a) for a in params]
        + [pl.BlockSpec((tm, LANES), tab)] * 3,
        out_specs=[pl.BlockSpec((tm, w), row) for w, _ in outs],
        out_shape=[jax.ShapeDtypeStruct((n, w), dt) for w, dt in outs],
        compiler_params=pltpu.CompilerParams(dimension_semantics=("parallel",), vmem_limit_bytes=VMEM_LIMIT),
        name="inproj",
    )(x2d, *params, cos, sa, sb)


def _solve_unit_lower(lows, rhss, eye, blk):
    c = lows[0].shape[0]
    d = [jnp.where(blk, l, 0.0) for l in lows]
    e = [l - dh for l, dh in zip(lows, d)]
    dk = [_dot(dh, dh) for dh in d]
    p = [_dot(eye - dh, eye + d2) for dh, d2 in zip(d, dk)]
    span = 4
    while span < INV_BLOCK:
        dk = [_dot(m, m) for m in dk]
        p = [_dot(ph, eye + m) for ph, m in zip(p, dk)]
        span *= 2
    x = [_dot(ph, r) for ph, r in zip(p, rhss)]
    nb = c // INV_BLOCK
    if nb > 1:
        nk = [_dot(ph, eh) for ph, eh in zip(p, e)]
        x = [_dot(eye - m, xh) for m, xh in zip(nk, x)]
        order = 2
        while order < nb:
            nk = [_dot(m, m) for m in nk]
            x = [_dot(eye + m, xh) for m, xh in zip(nk, x)]
            order *= 2
    return x


def _rwkv_kernel(zs_ref, shift0_ref, wkv0_ref, mu_ref, w0_ref, w2_ref, a0_ref, a2_ref, g2_ref,
                 kk_ref, ka_ref, rk_ref, lng_ref, lnb_ref, bd_ref,
                 o_ref, sout_ref, s_scr, prev_scr, *, chunk):
    c = chunk
    nb = zs_ref.shape[0]
    ci = pl.program_id(1)
    w = RWKV_WIDTH

    @pl.when(ci == 0)
    def _():
        s_scr[...] = wkv0_ref[...]
        prev_scr[...] = shift0_ref[...]

    row = lax.broadcasted_iota(jnp.int32, (c, c), 0)
    col = lax.broadcasted_iota(jnp.int32, (c, c), 1)
    tri_incl = row >= col
    tri_strict = row > col
    eye = jnp.where(row == col, 1.0, 0.0)
    blk = jnp.right_shift(row, INV_SHIFT) == jnp.right_shift(col, INV_SHIFT)
    tri = jnp.where(tri_incl, 1.0, 0.0).astype(BF16)
    mu = mu_ref[...]
    bd = bd_ref[...]
    sls = [slice(h * HEAD_DIM, (h + 1) * HEAD_DIM) for h in range(RWKV_HEADS)]

    ar, kb, vh, g_last, gates, bonus_rk = [], [], [], [], [], []
    for j in range(nb):
        z = zs_ref[j]
        prev_row = prev_scr[j]

        def mixed(lo, hi):
            zp = z[:, lo:hi]
            rolled = pltpu.roll(zp, 1, 0)
            first = lax.broadcasted_iota(jnp.int32, zp.shape, 0) == 0
            prv = jnp.where(first, prev_row[:, lo:hi], rolled)
            return zp + mu[:, lo:hi] * (prv - zp)

        r = mixed(0, w)
        k = mixed(w, 2 * w)
        v = mixed(2 * w, 3 * w)
        tail = mixed(3 * w, SHIFT_PAD)
        prev_scr[j] = z[c - 1:c, :]

        lw = w0_ref[...] + _dot(jnp.tanh(tail), w2_ref[...])
        nlw = -lw
        softplus = jnp.maximum(nlw, 0.0) + jnp.log(1.0 + jnp.exp(-jnp.abs(nlw)))
        logd = -jnp.exp(-softplus - 0.5)
        a = 1.0 / (1.0 + jnp.exp(-(a0_ref[...] + _dot(tail, a2_ref[...]))))
        gates.append(_dot(1.0 / (1.0 + jnp.exp(-tail)), g2_ref[...]))

        kk = k * kk_ref[...]
        kk = kk / jnp.maximum(jnp.sqrt(_group_sum(kk * kk, bd)), L2_EPS)
        k = k * (1.0 + (a - 1.0) * ka_ref[...])
        b = kk * a

        hi = logd.astype(BF16)
        rem = logd - hi.astype(F32)
        mid = rem.astype(BF16)
        lo = (rem - mid.astype(F32)).astype(BF16)
        lg = (jnp.dot(tri, hi, preferred_element_type=F32) + jnp.dot(tri, mid, preferred_element_type=F32)
              + jnp.dot(tri, lo, preferred_element_type=F32))
        g = jnp.exp(lg)
        ginv = jnp.exp(-lg)
        a_m = jnp.exp(lg - logd) * kk
        b_m = b * ginv
        k_m = k * ginv
        p_m = g * r
        bonus_rk.append(r * k * rk_ref[...])
        ar += [jnp.concatenate([a_m[:, sl], p_m[:, sl]], axis=0).astype(BF16) for sl in sls]
        kb += [jnp.concatenate([k_m[:, sl], b_m[:, sl]], axis=0).astype(BF16) for sl in sls]
        vh += [v[:, sl] for sl in sls]
        g_last += [g[c - 1:c, sl] for sl in sls]

    chains = range(nb * RWKV_HEADS)
    s_h = [s_scr[j, h] for j in range(nb) for h in range(RWKV_HEADS)]
    gm = [_dot(ar[i], kb[i], NT) for i in chains]
    a_s = [_dot(ar[i], s_h[i], NT) for i in chains]
    rhs = [a_s[i][:c] + _dot(jnp.where(tri_strict, gm[i][:c, :c], 0.0), vh[i]) for i in chains]
    u = _solve_unit_lower([jnp.where(tri_strict, gm[i][:c, c:], 0.0) for i in chains], rhs, eye, blk)
    ys, bonus = [], []
    for i in chains:
        j, h = divmod(i, RWKV_HEADS)
        n_pk = jnp.where(tri_incl, gm[i][c:, :c], 0.0)
        n_pb = jnp.where(tri_incl, gm[i][c:, c:], 0.0)
        y = a_s[i][c:] + _dot(jnp.concatenate([n_pk, -n_pb], axis=1), jnp.concatenate([vh[i], u[i]], axis=0))
        s_scr[j, h] = (s_h[i] + _dot(jnp.concatenate([vh[i], -u[i]], axis=0), kb[i], TN)) * g_last[i]
        m = jnp.mean(y, axis=-1, keepdims=True)
        yc = y - m
        var = jnp.mean(yc * yc, axis=-1, keepdims=True)
        ys.append(yc * lax.rsqrt(var + GN_EPS))
        bonus.append(jnp.sum(bonus_rk[j][:, sls[h]], axis=-1, keepdims=True) * vh[i])

    for j in range(nb):
        hs = slice(j * RWKV_HEADS, (j + 1) * RWKV_HEADS)
        yn = jnp.concatenate(ys[hs], axis=1) * lng_ref[...] + lnb_ref[...]
        o_ref[j] = ((yn + jnp.concatenate(bonus[hs], axis=1)) * gates[j]).astype(o_ref.dtype)

    @pl.when(ci == pl.num_programs(1) - 1)
    def _():
        sout_ref[...] = s_scr[...]


def _rwkv(zs3, shift0, wkv0, p, chunk, nb):
    b, t, _ = zs3.shape
    const = lambda bi, ci: (0, 0)
    full = lambda a: pl.BlockSpec(a.shape, const)
    params = [p["mu"], p["w0"], p["w2"], p["a0"], p["a2"], p["g2"], p["k_k"], p["k_a"], p["r_k"],
              p["lnx_g"], p["lnx_b"], p["bd"]]
    state_spec = pl.BlockSpec((nb, RWKV_HEADS, HEAD_DIM, HEAD_DIM), lambda bi, ci: (bi, 0, 0, 0))
    return pl.pallas_call(
        functools.partial(_rwkv_kernel, chunk=chunk),
        grid=(b // nb, t // chunk),
        in_specs=[pl.BlockSpec((nb, chunk, SHIFT_PAD), lambda bi, ci: (bi, ci, 0)),
                  pl.BlockSpec((nb, 1, SHIFT_PAD), lambda bi, ci: (bi, 0, 0)),
                  state_spec] + [full(a) for a in params],
        out_specs=[pl.BlockSpec((nb, chunk, RWKV_WIDTH), lambda bi, ci: (bi, ci, 0)), state_spec],
        out_shape=[jax.ShapeDtypeStruct((b, t, RWKV_WIDTH), BF16),
                   jax.ShapeDtypeStruct((b, RWKV_HEADS, HEAD_DIM, HEAD_DIM), F32)],
        scratch_shapes=[pltpu.VMEM((nb, RWKV_HEADS, HEAD_DIM, HEAD_DIM), F32),
                        pltpu.VMEM((nb, 1, SHIFT_PAD), F32)],
        compiler_params=pltpu.CompilerParams(dimension_semantics=("parallel", "arbitrary"),
                                             vmem_limit_bytes=VMEM_LIMIT),
        name="rwkv7",
    )(zs3, shift0, wkv0, *params)


def _attn_kernel(*refs, n_grp, tq, n_main, n_extra, qb, pos0, n_valid, topk):
    if n_extra:
        q_ref, qi_ref, misc_ref, k_ref, v_ref, ki_ref, kx_ref, vx_ref, kix_ref, cnt_ref, o_ref = refs
    else:
        q_ref, qi_ref, misc_ref, k_ref, v_ref, ki_ref, cnt_ref, o_ref = refs
    n_keys = n_main + (LANES if n_extra else 0)
    nl = n_keys // LANES
    rows = n_grp * tq

    def pad_rows(x):
        return jnp.concatenate([x, jnp.zeros((LANES - n_extra, x.shape[1]), x.dtype)], axis=0)

    def dot_keys(lhs, main, extra):
        out = lax.dot_general(lhs, main, NT, preferred_element_type=F32)
        if n_extra:
            out = jnp.concatenate([out, lax.dot_general(lhs, extra, NT, preferred_element_type=F32)], axis=1)
        return out

    def row_limit(r):
        return min(((pos0 + qb * tq + r % tq) // CHUNK + 1) * CHUNK, n_valid)

    pos = pos0 + qb * tq + lax.broadcasted_iota(jnp.int32, (tq, LANES), 0)
    limit = jnp.minimum((jnp.right_shift(pos, CHUNK_SHIFT) + 1) * CHUNK, n_valid)
    if n_grp > 1:
        limit = jnp.concatenate([limit] * n_grp, axis=0)
    key_id = lax.broadcasted_iota(jnp.int32, (rows, n_keys), 1)
    adm = key_id < jnp.tile(limit, (1, nl))

    parts = []
    for g in range(n_grp):
        qi = qi_ref[g]
        ki = ki_ref[g]
        kix = pad_rows(kix_ref[g]) if n_extra else None
        misc = misc_ref[g]
        sc = None
        for h in range(IDX_HEADS):
            d = dot_keys(qi[:, h * LANES:(h + 1) * LANES], ki, kix)
            term = misc[:, IDX_DIM + h:IDX_DIM + h + 1] * jnp.maximum(d, 0.0)
            sc = term if sc is None else sc + term
        parts.append(sc)
    score = parts[0] if n_grp == 1 else jnp.concatenate(parts, axis=0)
    sm = jnp.where(adm, score, -jnp.inf)

    cnt_rhs = cnt_ref[...]
    ones_sq = cnt_ref[:, LANES:]

    def count_ge(sm_c, th, tiles):
        part = None
        for j in range(tiles):
            ind = jnp.where(sm_c[:, j * LANES:(j + 1) * LANES] >= th, 1.0, 0.0)
            part = ind if part is None else part + ind
        return jnp.dot(part.astype(BF16), ones_sq, preferred_element_type=F32)

    row_max = jnp.max(sm, axis=-1, keepdims=True)
    row_min = jnp.min(jnp.where(adm, score, jnp.inf), axis=-1, keepdims=True)
    lo0 = jnp.broadcast_to(row_min, (rows, LANES))
    hi0 = jnp.broadcast_to(2.0 * row_max - row_min + 1.0, (rows, LANES))

    rc = rows // BISECT_CHAINS
    chains = [slice(i * rc, (i + 1) * rc) for i in range(BISECT_CHAINS)]
    tiles = [-(-max(row_limit(r) for r in range(ch.start, ch.stop)) // LANES) for ch in chains]

    def bisect(_, carry):
        out = []
        for ch, nt, (lo, hi) in zip(chains, tiles, carry):
            mid = 0.5 * (lo + hi)
            enough = count_ge(sm[ch], mid, nt) >= topk
            out.append((jnp.where(enough, mid, lo), jnp.where(enough, hi, mid)))
        return tuple(out)

    bounds = lax.fori_loop(0, BISECT_ITERS, bisect, tuple((lo0[ch], hi0[ch]) for ch in chains),
                           unroll=BISECT_UNROLL)
    lo = jnp.concatenate([b[0] for b in bounds], axis=0)
    hi = jnp.concatenate([b[1] for b in bounds], axis=0)

    need = topk - count_ge(sm, hi, nl)
    take_all = limit <= topk
    carry = jnp.zeros((rows, LANES), F32)
    bias = []
    for j in range(nl):
        sm_j = sm[:, j * LANES:(j + 1) * LANES]
        above = sm_j >= hi
        group = jnp.logical_and(sm_j >= lo, jnp.logical_not(above))
        pc = jnp.dot(jnp.where(group, 1.0, 0.0).astype(BF16), cnt_rhs, preferred_element_type=F32)
        fill = carry + pc[:, :LANES] <= need
        carry = carry + pc[:, LANES:]
        picked = jnp.logical_or(above, jnp.logical_and(group, fill))
        sel = jnp.logical_and(sm_j > -jnp.inf, jnp.logical_or(take_all, picked))
        bias.append(jnp.where(sel, 0.0, MASK_BIAS))
    bias = jnp.concatenate(bias, axis=1)

    for g in range(n_grp):
        q = q_ref[g]
        km, vm = k_ref[g], v_ref[g]
        kx, vx = (pad_rows(kx_ref[g]), pad_rows(vx_ref[g])) if n_extra else (None, None)
        bias_g = jnp.concatenate([bias[g * tq:(g + 1) * tq]] * GQA_GROUP, axis=0)
        slabs = []
        for gi in range(N_KV_HEADS):
            gs = slice(gi * LANES, (gi + 1) * LANES)
            qs = jnp.concatenate([q[:, (gi * GQA_GROUP + j) * LANES:(gi * GQA_GROUP + j + 1) * LANES]
                                  for j in range(GQA_GROUP)], axis=0)
            s = dot_keys(qs, km[:, gs], kx[:, gs] if n_extra else None) + bias_g
            m = jnp.max(s, axis=-1, keepdims=True)
            pexp = jnp.exp((s - m).astype(BF16))
            o = jnp.dot(pexp[:, :n_main], vm[:, gs], preferred_element_type=F32)
            if n_extra:
                o = o + jnp.dot(pexp[:, n_main:], vx[:, gs], preferred_element_type=F32)
            o = o / pltpu.roll(o, HEAD_DIM, 1)
            slabs.extend(o[j * tq:(j + 1) * tq] for j in range(GQA_GROUP))
        o_ref[g] = _pack_heads(slabs).astype(o_ref.dtype)


def _attention(q3, qi3, misc3, main, extra, cnt_rhs, *, n_grp, tq, n_main, qb, pos0, n_valid, topk):
    b = q3.shape[0]
    n_extra = extra[0].shape[1] if extra else 0
    qmap = lambda bi: (bi, qb, 0)
    kmap = lambda bi: (bi, 0, 0)
    widths = (N_KV_HEADS * LANES, N_KV_HEADS * LANES, LANES)
    key_specs = [pl.BlockSpec((n_grp, n_main, w), kmap) for w in widths]
    if extra:
        key_specs += [pl.BlockSpec((n_grp, n_extra, w), kmap) for w in widths]
    return pl.pallas_call(
        functools.partial(_attn_kernel, n_grp=n_grp, tq=tq, n_main=n_main, n_extra=n_extra, qb=qb, pos0=pos0,
                          n_valid=n_valid, topk=topk),
        grid=(b // n_grp,),
        in_specs=[pl.BlockSpec((n_grp, tq, N_HEADS * LANES), qmap),
                  pl.BlockSpec((n_grp, tq, IDX_HEADS * LANES), qmap),
                  pl.BlockSpec((n_grp, tq, LANES), qmap)] + key_specs
        + [pl.BlockSpec(cnt_rhs.shape, lambda bi: (0, 0))],
        out_specs=pl.BlockSpec((n_grp, tq, ATT_WIDTH), lambda bi: (bi, 0, 0)),
        out_shape=jax.ShapeDtypeStruct((b, tq, ATT_WIDTH), BF16),
        compiler_params=pltpu.CompilerParams(dimension_semantics=("parallel",), vmem_limit_bytes=VMEM_LIMIT),
        name=f"dsa_attn_{n_main + n_extra}",
    )(q3, qi3, misc3, *main, *(extra or ()), cnt_rhs)


def _outffn_kernel(x_ref, oa_ref, ob_ref, wo_a_ref, wo_b_ref, g_ref, w1_ref, w2_ref, y_ref, *, ff_chunk):
    mix = (jnp.dot(oa_ref[...], wo_a_ref[...], preferred_element_type=F32)
           + jnp.dot(ob_ref[...], wo_b_ref[...], preferred_element_type=F32))
    h = x_ref[...] + mix
    ms = jnp.mean(h * h, axis=-1, keepdims=True)
    hn = (h * lax.rsqrt(ms + NORM_EPS) * g_ref[...]).astype(BF16)
    acc = h
    for j in range(D_FF // ff_chunk):
        act = jnp.dot(hn, w1_ref[:, j * ff_chunk:(j + 1) * ff_chunk], preferred_element_type=F32)
        act = jnp.square(jnp.maximum(act, 0.0)).astype(BF16)
        acc = acc + jnp.dot(act, w2_ref[j * ff_chunk:(j + 1) * ff_chunk, :], preferred_element_type=F32)
    y_ref[...] = acc


def _outffn(x2d, o_att, o_rwkv, p, tm):
    n = x2d.shape[0]
    row = lambda i: (i, 0)
    const = lambda i: (0, 0)
    params = [p["wo_a"], p["wo_b"], p["norm_ffn"], p["w_ff1"], p["w_ff2"]]
    return pl.pallas_call(
        functools.partial(_outffn_kernel, ff_chunk=1024),
        grid=(n // tm,),
        in_specs=[pl.BlockSpec((tm, D_MODEL), row), pl.BlockSpec((tm, ATT_WIDTH), row),
                  pl.BlockSpec((tm, RWKV_WIDTH), row)]
        + [pl.BlockSpec(a.shape, const, pipeline_mode=pl.Buffered(1)) for a in params],
        out_specs=pl.BlockSpec((tm, D_MODEL), row),
        out_shape=jax.ShapeDtypeStruct((n, D_MODEL), F32),
        compiler_params=pltpu.CompilerParams(dimension_semantics=("parallel",), vmem_limit_bytes=VMEM_LIMIT),
        name="outproj_ffn",
    )(x2d, o_att, o_rwkv, *params)


def _block_diag_ones():
    i = np.arange(ATT_WIDTH)
    return jnp.asarray((i[:, None] // HEAD_DIM) == (i[None, :] // HEAD_DIM), BF16)


def _count_rhs():
    i = np.arange(LANES)
    tri = (i[:, None] <= i[None, :]).astype(np.float32)
    return jnp.asarray(np.concatenate([tri, np.ones((LANES, LANES), np.float32)], axis=1), BF16)


def _prep_params(wl):
    (norm_mix, w_in, q_gain, k_gain, kidx_ln_g, kidx_ln_b, mu_shift, w0, w2, a0, a2, g2, k_k, k_a, r_k,
     ln_x_g, ln_x_b, w_out, norm_ffn, w_ff1, w_ff2) = wl
    row = lambda a: a.reshape(1, -1).astype(F32)
    o_ki = QKV_WIDTH
    o_wi = o_ki + IDX_DIM
    o_zs = o_wi + IDX_HEADS
    pad_cols = lambda a, wdt: jnp.pad(a, ((0, 0), (0, wdt - a.shape[1])))
    lora_rows = lambda a, off: jnp.pad(a, ((off, LORA_PAD - off - a.shape[0]), (0, 0))).astype(BF16)
    return {
        "norm_mix": row(norm_mix),
        "wa": w_in[:, :QKV_WIDTH].astype(BF16),
        "wb": pad_cols(w_in[:, o_ki:o_zs], LANES).astype(BF16),
        "wc": pad_cols(w_in[:, o_zs:], SHIFT_PAD).astype(BF16),
        "q_gain": row(jnp.tile(q_gain, N_HEADS)),
        "k_gain": row(jnp.tile(k_gain, N_KV_HEADS)),
        "ln_g": pad_cols(row(kidx_ln_g), LANES),
        "ln_b": pad_cols(row(kidx_ln_b), LANES),
        "bd": _block_diag_ones(),
        "mu": pad_cols(row(mu_shift), SHIFT_PAD),
        "w0": row(w0), "a0": row(a0),
        "w2": lora_rows(w2, 0), "a2": lora_rows(a2, W_LORA), "g2": lora_rows(g2, W_LORA + A_LORA),
        "k_k": row(k_k), "k_a": row(k_a), "r_k": row(r_k), "lnx_g": row(ln_x_g), "lnx_b": row(ln_x_b),
        "wo_a": w_out[:ATT_WIDTH].astype(BF16), "wo_b": w_out[ATT_WIDTH:].astype(BF16),
        "norm_ffn": row(norm_ffn),
        "w_ff1": w_ff1.astype(BF16), "w_ff2": w_ff2.astype(BF16),
    }


def _layer(x, pos0, k_past, v_past, ki_past, wkv0, shift0, p, cnt_rhs, *, chunk, n_grp, tq):
    b, t, _ = x.shape
    n = b * t
    past = k_past.shape[1]
    x2d = x.reshape(n, D_MODEL)
    tm = min(ROW_TILE, n)
    q, k, v, qi, misc, zs, kbf, vbf, kibf = _inproj(x2d, t, pos0, p, tm)

    shift0p = jnp.pad(shift0, ((0, 0), (0, 0), (0, SHIFT_PAD - SHIFT_WIDTH)))
    zs3 = zs.reshape(b, t, SHIFT_PAD)
    o_rwkv, wkv_new = _rwkv(zs3, shift0p, wkv0, p, chunk, min(b, RWKV_BATCH))

    n_valid = past + t
    topk = min(TOPK_MAX, n_valid // 4)
    three = lambda a: a.reshape(b, t, a.shape[-1])
    k_new, v_new, ki_new = three(kbf), three(vbf), three(kibf)
    q3, qi3, misc3 = three(q), three(qi), three(misc)
    if past:
        def spread(old, fill):
            old = old.reshape(b, past, -1, HEAD_DIM).astype(BF16)
            old = jnp.pad(old, ((0, 0), (0, 0), (0, 0), (0, LANES - HEAD_DIM)), constant_values=fill)
            return old.reshape(b, past, -1)

        main = (spread(k_past, 0.0), spread(v_past, 1.0), spread(ki_past, 0.0))
        o_att = _attention(q3, qi3, misc3, main, (k_new, v_new, ki_new), cnt_rhs, n_grp=n_grp, tq=tq,
                           n_main=past, qb=0, pos0=pos0, n_valid=n_valid, topk=topk)
    else:
        o_att = jnp.concatenate(
            [_attention(q3, qi3, misc3, (k_new, v_new, ki_new), None, cnt_rhs, n_grp=n_grp, tq=tq,
                        n_main=(j + 1) * tq, qb=j, pos0=pos0, n_valid=n_valid, topk=topk)
             for j in range(t // tq)], axis=1)

    y = _outffn(x2d, o_att.reshape(n, ATT_WIDTH), o_rwkv.reshape(n, RWKV_WIDTH), p, tm)
    return (y.reshape(b, t, D_MODEL), k.reshape(b, t, N_KV_HEADS, HEAD_DIM),
            v.reshape(b, t, N_KV_HEADS, HEAD_DIM), misc3[:, :, :IDX_DIM], wkv_new,
            zs3[:, t - 1:, :SHIFT_WIDTH])


def kernel(x_prompt, x_sample, cache_k, cache_v, cache_kidx, state_wkv, state_shift, norm_mix, w_in, q_gain, k_gain, kidx_ln_g, kidx_ln_b, mu_shift, w0, w2, a0, a2, g2, k_k, k_a, r_k, ln_x_g, ln_x_b, w_out, norm_ffn, w_ff1, w_ff2):
    depth = norm_mix.shape[0]
    bp, tp = x_prompt.shape[0], x_prompt.shape[1]
    bs, ts = x_sample.shape[0], x_sample.shape[1]
    past_len = cache_k.shape[2]
    cnt_rhs = _count_rhs()
    no_kv = jnp.zeros((bp, 0, KV_WIDTH), F32)
    no_ki = jnp.zeros((bp, 0, IDX_DIM), F32)
    wkv_zero = jnp.zeros((bp, RWKV_HEADS, HEAD_DIM, HEAD_DIM), F32)
    shift_zero = jnp.zeros((bp, 1, SHIFT_WIDTH), F32)

    y_p, y_s = x_prompt, x_sample
    outs_p, outs_s = [], []
    for l in range(depth):
        wl = (norm_mix[l], w_in[l], q_gain[l], k_gain[l], kidx_ln_g[l], kidx_ln_b[l], mu_shift[l],
              w0[l], w2[l], a0[l], a2[l], g2[l], k_k[l], k_a[l], r_k[l], ln_x_g[l], ln_x_b[l],
              w_out[l], norm_ffn[l], w_ff1[l], w_ff2[l])
        p = _prep_params(wl)
        y_p, *rest_p = _layer(y_p, 0, no_kv, no_kv, no_ki, wkv_zero, shift_zero, p, cnt_rhs,
                              chunk=64, n_grp=1, tq=256)
        y_s, *rest_s = _layer(y_s, past_len, cache_k[l], cache_v[l], cache_kidx[l], state_wkv[l],
                              state_shift[l], p, cnt_rhs, chunk=ts, n_grp=min(bs, 256 // ts), tq=ts)
        outs_p.append(rest_p)
        outs_s.append(rest_s)
    stack = lambda outs, i: jnp.stack([o[i] for o in outs])
    return (y_p, y_s,
            *(stack(outs_p, i) for i in range(5)),
            *(stack(outs_s, i) for i in range(5)))
```

```python
import functools

import numpy as np
import jax
import jax.numpy as jnp
from jax import lax
from jax.experimental import pallas as pl
from jax.experimental.pallas import tpu as pltpu

F32 = jnp.float32
BF16 = jnp.bfloat16

D_MODEL = 1024
CHUNK = 64
HEAD_DIM = 64
ATT_WIDTH = D_MODEL // 2
N_HEADS = ATT_WIDTH // HEAD_DIM
N_KV_HEADS = N_HEADS // 2
GQA_GROUP = N_HEADS // N_KV_HEADS
KV_WIDTH = N_KV_HEADS * HEAD_DIM
ROT_DIM = HEAD_DIM // 4
ROPE_THETA = 500000.0
IDX_HEADS = 8
IDX_DIM = 64
IDX_WIDTH = IDX_HEADS * IDX_DIM
TOPK_MAX = 256
RWKV_WIDTH = D_MODEL - ATT_WIDTH
RWKV_HEADS = RWKV_WIDTH // HEAD_DIM
W_LORA = 32
A_LORA = 32
G_LORA = 96
LORA_WIDTH = W_LORA + A_LORA + G_LORA
SHIFT_WIDTH = 3 * RWKV_WIDTH + LORA_WIDTH
D_FF = 4 * D_MODEL
NORM_EPS = 1e-6
GN_EPS = 64e-5
L2_EPS = 1e-12

LANES = 128
SHIFT_PAD = 3 * RWKV_WIDTH + 2 * LANES
LORA_PAD = SHIFT_PAD - 3 * RWKV_WIDTH
QKV_WIDTH = ATT_WIDTH + 2 * KV_WIDTH + IDX_WIDTH
INV_BLOCK = 16
INV_SHIFT = 4
CHUNK_SHIFT = 6
MASK_BIAS = -1e30
ROW_TILE = 256
RWKV_BATCH = 4
BISECT_ITERS = 32
BISECT_CHAINS = 4
BISECT_UNROLL = 2
VMEM_LIMIT = 56 * 1024 * 1024

NN = (((1,), (0,)), ((), ()))
NT = (((1,), (1,)), ((), ()))
TN = (((0,), (0,)), ((), ()))


def _dot(a, b, dims=NN):
    return lax.dot_general(a.astype(BF16), b.astype(BF16), dims, preferred_element_type=F32)


def _group_sum(x, bd):
    w = x.shape[-1]
    return _dot(x, bd[:w, :w])


def _rope(x, cos, sa, sb):
    w = x.shape[-1]
    reps = w // LANES
    if reps > 1:
        cos, sa, sb = (jnp.tile(t, (1, reps)) for t in (cos, sa, sb))
    half = ROT_DIM // 2
    return x * cos + pltpu.roll(x, w - half, 1) * sa + pltpu.roll(x, half, 1) * sb


def _spread_heads(x, fill):
    low = lax.broadcasted_iota(jnp.int32, (x.shape[0], LANES), 1) < HEAD_DIM
    cols = []
    for c in range(x.shape[1] // LANES):
        pair = x[:, c * LANES:(c + 1) * LANES]
        cols.append(jnp.where(low, pair, fill))
        cols.append(jnp.where(low, pltpu.roll(pair, HEAD_DIM, 1), fill))
    return jnp.concatenate(cols, axis=1)


def _pack_heads(slabs):
    low = lax.broadcasted_iota(jnp.int32, slabs[0].shape, 1) < HEAD_DIM
    cols = [jnp.where(low, slabs[i], pltpu.roll(slabs[i + 1], HEAD_DIM, 1)) for i in range(0, len(slabs), 2)]
    return jnp.concatenate(cols, axis=1)


def _inproj_kernel(x_ref, g_ref, wa_ref, wb_ref, wc_ref, qg_ref, kg_ref, lng_ref, lnb_ref, bd_ref,
                   cos_ref, sa_ref, sb_ref,
                   q_ref, k_ref, v_ref, qi_ref, misc_ref, zs_ref, kbf_ref, vbf_ref, kibf_ref):
    x = x_ref[...]
    ms = jnp.mean(x * x, axis=-1, keepdims=True)
    xn = (x * lax.rsqrt(ms + NORM_EPS) * g_ref[...]).astype(BF16)
    za = jnp.dot(xn, wa_ref[...], preferred_element_type=F32)
    zb = jnp.dot(xn, wb_ref[...], preferred_element_type=F32)
    zs_ref[...] = jnp.dot(xn, wc_ref[...], preferred_element_type=F32)

    cos, sa, sb = cos_ref[...], sa_ref[...], sb_ref[...]
    bd = bd_ref[...]

    zq = za[:, :ATT_WIDTH]
    qn = zq * lax.rsqrt(_group_sum(zq * zq, bd) * (1.0 / HEAD_DIM) + NORM_EPS) * qg_ref[...]
    q_ref[...] = _spread_heads(_rope(qn, cos, sa, sb) * (HEAD_DIM ** -0.5), 0.0).astype(BF16)

    zk = za[:, ATT_WIDTH:ATT_WIDTH + KV_WIDTH]
    kn = zk * lax.rsqrt(_group_sum(zk * zk, bd) * (1.0 / HEAD_DIM) + NORM_EPS) * kg_ref[...]
    k = _rope(kn, cos, sa, sb)
    k_ref[...] = k
    kbf_ref[...] = _spread_heads(k, 0.0).astype(BF16)

    v = za[:, ATT_WIDTH + KV_WIDTH:ATT_WIDTH + 2 * KV_WIDTH]
    v_ref[...] = v
    vbf_ref[...] = _spread_heads(v, 1.0).astype(BF16)

    zqi = za[:, ATT_WIDTH + 2 * KV_WIDTH:]
    qi_ref[...] = _spread_heads(_rope(zqi, cos, sa, sb) * (IDX_DIM ** -0.5), 0.0).astype(BF16)

    lane = lax.broadcasted_iota(jnp.int32, zb.shape, 1)
    is_ki = lane < IDX_DIM
    mu = jnp.sum(jnp.where(is_ki, zb, 0.0), axis=-1, keepdims=True) * (1.0 / IDX_DIM)
    d = jnp.where(is_ki, zb - mu, 0.0)
    var = jnp.sum(d * d, axis=-1, keepdims=True) * (1.0 / IDX_DIM)
    kin = d * lax.rsqrt(var + NORM_EPS) * lng_ref[...] + lnb_ref[...]
    ki = _rope(kin, cos, sa, sb)
    misc_ref[...] = jnp.where(is_ki, ki, zb * (IDX_HEADS ** -0.5))
    kibf_ref[...] = jnp.where(is_ki, ki, 0.0).astype(BF16)


def _rope_tables(pos):
    half = ROT_DIM // 2
    inv = ROPE_THETA ** (-jnp.arange(0, ROT_DIM, 2, dtype=F32) / ROT_DIM)
    ang = pos.astype(F32)[:, None] * inv[None, :]
    cos, sin = jnp.cos(ang), jnp.sin(ang)
    n = pos.shape[0]
    ones = jnp.ones((n, HEAD_DIM - ROT_DIM), F32)
    zeros_h = jnp.zeros((n, half), F32)
    zeros_r = jnp.zeros((n, HEAD_DIM - ROT_DIM), F32)
    c64 = jnp.concatenate([cos, cos, ones], axis=1)
    sa64 = jnp.concatenate([-sin, zeros_h, zeros_r], axis=1)
    sb64 = jnp.concatenate([zeros_h, sin, zeros_r], axis=1)
    rep = LANES // HEAD_DIM
    return tuple(jnp.tile(t, (1, rep)) for t in (c64, sa64, sb64))


def _inproj(x2d, seq_len, pos0, p, tm):
    n = x2d.shape[0]
    tab_rows = max(seq_len, tm)
    pos = pos0 + (jnp.arange(tab_rows, dtype=jnp.int32) % seq_len)
    cos, sa, sb = _rope_tables(pos)
    nblk_t = tab_rows // tm

    def row(i):
        return (i, 0)

    def const(i):
        return (0, 0)

    def tab(i):
        return (i % nblk_t, 0)

    full = lambda a: pl.BlockSpec(a.shape, const)
    outs = [
        (N_HEADS * LANES, BF16), (KV_WIDTH, F32), (KV_WIDTH, F32), (IDX_HEADS * LANES, BF16), (LANES, F32),
        (SHIFT_PAD, F32), (N_KV_HEADS * LANES, BF16), (N_KV_HEADS * LANES, BF16), (LANES, BF16),
    ]
    params = [p["norm_mix"], p["wa"], p["wb"], p["wc"], p["q_gain"], p["k_gain"], p["ln_g"], p["ln_b"], p["bd"]]
    return pl.pallas_call(
        _inproj_kernel,
        grid=(n // tm,),
        in_specs=[pl.BlockSpec((tm, D_MODEL), row)] + [full(a) for a in params]
        + [pl.BlockSpec((tm, LANES), tab)] * 3,
        out_specs=[pl.BlockSpec((tm, w), row) for w, _ in outs],
        out_shape=[jax.ShapeDtypeStruct((n, w), dt) for w, dt in outs],
        compiler_params=pltpu.CompilerParams(dimension_semantics=("parallel",), vmem_limit_bytes=VMEM_LIMIT),
        name="inproj",
    )(x2d, *params, cos, sa, sb)


def _solve_unit_lower(lows, rhss, eye, blk):
    c = lows[0].shape[0]
    d = [jnp.where(blk, l, 0.0) for l in lows]
    e = [l - dh for l, dh in zip(lows, d)]
    dk = [_dot(dh, dh) for dh in d]
    p = [_dot(eye - dh, eye + d2) for dh, d2 in zip(d, dk)]
    span = 4
    while span < INV_BLOCK:
        dk = [_dot(m, m) for m in dk]
        p = [_dot(ph, eye + m) for ph, m in zip(p, dk)]
        span *= 2
    x = [_dot(ph, r) for ph, r in zip(p, rhss)]
    nb = c // INV_BLOCK
    if nb > 1:
        nk = [_dot(ph, eh) for ph, eh in zip(p, e)]
        x = [_dot(eye - m, xh) for m, xh in zip(nk, x)]
        order = 2
        while order < nb:
            nk = [_dot(m, m) for m in nk]
            x = [_dot(eye + m, xh) for m, xh in zip(nk, x)]
            order *= 2
    return x


def _rwkv_kernel(zs_ref, shift0_ref, wkv0_ref, mu_ref, w0_ref, w2_ref, a0_ref, a2_ref, g2_ref,
                 kk_ref, ka_ref, rk_ref, lng_ref, lnb_ref, bd_ref,
                 o_ref, sout_ref, s_scr, prev_scr, *, chunk):
    c = chunk
    nb = zs_ref.shape[0]
    ci = pl.program_id(1)
    w = RWKV_WIDTH

    @pl.when(ci == 0)
    def _():
        s_scr[...] = wkv0_ref[...]
        prev_scr[...] = shift0_ref[...]

    row = lax.broadcasted_iota(jnp.int32, (c, c), 0)
    col = lax.broadcasted_iota(jnp.int32, (c, c), 1)
    tri_incl = row >= col
    tri_strict = row > col
    eye = jnp.where(row == col, 1.0, 0.0)
    blk = jnp.right_shift(row, INV_SHIFT) == jnp.right_shift(col, INV_SHIFT)
    tri = jnp.where(tri_incl, 1.0, 0.0).astype(BF16)
    mu = mu_ref[...]
    bd = bd_ref[...]
    sls = [slice(h * HEAD_DIM, (h + 1) * HEAD_DIM) for h in range(RWKV_HEADS)]

    ar, km, bm, vh, g_last, gates, bonus_rk = [], [], [], [], [], [], []
    for j in range(nb):
        z = zs_ref[j]
        prev_row = prev_scr[j]

        def mixed(lo, hi):
            zp = z[:, lo:hi]
            rolled = pltpu.roll(zp, 1, 0)
            first = lax.broadcasted_iota(jnp.int32, zp.shape, 0) == 0
            prv = jnp.where(first, prev_row[:, lo:hi], rolled)
            return zp + mu[:, lo:hi] * (prv - zp)

        r = mixed(0, w)
        k = mixed(w, 2 * w)
        v = mixed(2 * w, 3 * w)
        tail = mixed(3 * w, SHIFT_PAD)
        prev_scr[j] = z[c - 1:c, :]

        lw = w0_ref[...] + _dot(jnp.tanh(tail), w2_ref[...])
        nlw = -lw
        softplus = jnp.maximum(nlw, 0.0) + jnp.log(1.0 + jnp.exp(-jnp.abs(nlw)))
        logd = -jnp.exp(-softplus - 0.5)
        a = 1.0 / (1.0 + jnp.exp(-(a0_ref[...] + _dot(tail, a2_ref[...]))))
        gates.append(_dot(1.0 / (1.0 + jnp.exp(-tail)), g2_ref[...]))

        kk = k * kk_ref[...]
        kk = kk / jnp.maximum(jnp.sqrt(_group_sum(kk * kk, bd)), L2_EPS)
        k = k * (1.0 + (a - 1.0) * ka_ref[...])
        b = kk * a

        hi = logd.astype(BF16)
        rem = logd - hi.astype(F32)
        mid = rem.astype(BF16)
        lo = (rem - mid.astype(F32)).astype(BF16)
        lg = (jnp.dot(tri, hi, preferred_element_type=F32) + jnp.dot(tri, mid, preferred_element_type=F32)
              + jnp.dot(tri, lo, preferred_element_type=F32))
        g = jnp.exp(lg)
        ginv = jnp.exp(-lg)
        a_m = jnp.exp(lg - logd) * kk
        b_m = b * ginv
        k_m = k * ginv
        p_m = g * r
        bonus_rk.append(r * k * rk_ref[...])
        ar += [jnp.concatenate([a_m[:, sl], p_m[:, sl]], axis=0).astype(BF16) for sl in sls]
        km += [k_m[:, sl].astype(BF16) for sl in sls]
        bm += [b_m[:, sl].astype(BF16) for sl in sls]
        vh += [v[:, sl] for sl in sls]
        g_last += [g[c - 1:c, sl] for sl in sls]

    chains = range(nb * RWKV_HEADS)
    s_h = [s_scr[j, h] for j in range(nb) for h in range(RWKV_HEADS)]
    g_k = [_dot(ar[i], km[i], NT) for i in chains]
    g_b = [_dot(ar[i], bm[i], NT) for i in chains]
    a_s = [_dot(ar[i], s_h[i], NT) for i in chains]
    rhs = [a_s[i][:c] + _dot(jnp.where(tri_strict, g_k[i][:c], 0.0), vh[i]) for i in chains]
    u = _solve_unit_lower([jnp.where(tri_strict, g_b[i][:c], 0.0) for i in chains], rhs, eye, blk)
    ys, bonus = [], []
    for i in chains:
        j, h = divmod(i, RWKV_HEADS)
        n_pk = jnp.where(tri_incl, g_k[i][c:], 0.0)
        n_pb = jnp.where(tri_incl, g_b[i][c:], 0.0)
        y = a_s[i][c:] + _dot(n_pk, vh[i]) - _dot(n_pb, u[i])
        s_scr[j, h] = (s_h[i] + _dot(vh[i], km[i], TN) - _dot(u[i], bm[i], TN)) * g_last[i]
        m = jnp.mean(y, axis=-1, keepdims=True)
        yc = y - m
        var = jnp.mean(yc * yc, axis=-1, keepdims=True)
        ys.append(yc * lax.rsqrt(var + GN_EPS))
        bonus.append(jnp.sum(bonus_rk[j][:, sls[h]], axis=-1, keepdims=True) * vh[i])

    for j in range(nb):
        hs = slice(j * RWKV_HEADS, (j + 1) * RWKV_HEADS)
        yn = jnp.concatenate(ys[hs], axis=1) * lng_ref[...] + lnb_ref[...]
        o_ref[j] = ((yn + jnp.concatenate(bonus[hs], axis=1)) * gates[j]).astype(o_ref.dtype)

    @pl.when(ci == pl.num_programs(1) - 1)
    def _():
        sout_ref[...] = s_scr[...]


def _rwkv(zs3, shift0, wkv0, p, chunk, nb):
    b, t, _ = zs3.shape
    const = lambda bi, ci: (0, 0)
    full = lambda a: pl.BlockSpec(a.shape, const)
    params = [p["mu"], p["w0"], p["w2"], p["a0"], p["a2"], p["g2"], p["k_k"], p["k_a"], p["r_k"],
              p["lnx_g"], p["lnx_b"], p["bd"]]
    state_spec = pl.BlockSpec((nb, RWKV_HEADS, HEAD_DIM, HEAD_DIM), lambda bi, ci: (bi, 0, 0, 0))
    return pl.pallas_call(
        functools.partial(_rwkv_kernel, chunk=chunk),
        grid=(b // nb, t // chunk),
        in_specs=[pl.BlockSpec((nb, chunk, SHIFT_PAD), lambda bi, ci: (bi, ci, 0)),
                  pl.BlockSpec((nb, 1, SHIFT_PAD), lambda bi, ci: (bi, 0, 0)),
                  state_spec] + [full(a) for a in params],
        out_specs=[pl.BlockSpec((nb, chunk, RWKV_WIDTH), lambda bi, ci: (bi, ci, 0)), state_spec],
        out_shape=[jax.ShapeDtypeStruct((b, t, RWKV_WIDTH), BF16),
                   jax.ShapeDtypeStruct((b, RWKV_HEADS, HEAD_DIM, HEAD_DIM), F32)],
        scratch_shapes=[pltpu.VMEM((nb, RWKV_HEADS, HEAD_DIM, HEAD_DIM), F32),
                        pltpu.VMEM((nb, 1, SHIFT_PAD), F32)],
        compiler_params=pltpu.CompilerParams(dimension_semantics=("parallel", "arbitrary"),
                                             vmem_limit_bytes=VMEM_LIMIT),
        name="rwkv7",
    )(zs3, shift0, wkv0, *params)


def _attn_kernel(*refs, n_grp, tq, n_main, n_extra, qb, pos0, n_valid, topk):
    if n_extra:
        q_ref, qi_ref, misc_ref, k_ref, v_ref, ki_ref, kx_ref, vx_ref, kix_ref, cnt_ref, o_ref = refs
    else:
        q_ref, qi_ref, misc_ref, k_ref, v_ref, ki_ref, cnt_ref, o_ref = refs
    n_keys = n_main + (LANES if n_extra else 0)
    nl = n_keys // LANES
    rows = n_grp * tq

    def pad_rows(x):
        return jnp.concatenate([x, jnp.zeros((LANES - n_extra, x.shape[1]), x.dtype)], axis=0)

    def dot_keys(lhs, main, extra):
        out = lax.dot_general(lhs, main, NT, preferred_element_type=F32)
        if n_extra:
            out = jnp.concatenate([out, lax.dot_general(lhs, extra, NT, preferred_element_type=F32)], axis=1)
        return out

    def row_limit(r):
        return min(((pos0 + qb * tq + r % tq) // CHUNK + 1) * CHUNK, n_valid)

    pos = pos0 + qb * tq + lax.broadcasted_iota(jnp.int32, (tq, LANES), 0)
    limit = jnp.minimum((jnp.right_shift(pos, CHUNK_SHIFT) + 1) * CHUNK, n_valid)
    if n_grp > 1:
        limit = jnp.concatenate([limit] * n_grp, axis=0)
    key_id = lax.broadcasted_iota(jnp.int32, (rows, n_keys), 1)
    adm = key_id < jnp.tile(limit, (1, nl))

    def topk_bias():
        parts = []
        for g in range(n_grp):
            qi = qi_ref[g]
            ki = ki_ref[g]
            kix = pad_rows(kix_ref[g]) if n_extra else None
            misc = misc_ref[g]
            sc = None
            for h in range(IDX_HEADS):
                d = dot_keys(qi[:, h * LANES:(h + 1) * LANES], ki, kix)
                term = misc[:, IDX_DIM + h:IDX_DIM + h + 1] * jnp.maximum(d, 0.0)
                sc = term if sc is None else sc + term
            parts.append(sc)
        score = parts[0] if n_grp == 1 else jnp.concatenate(parts, axis=0)
        sm = jnp.where(adm, score, -jnp.inf)

        cnt_rhs = cnt_ref[...]
        ones_sq = cnt_ref[:, LANES:]

        def count_ge(sm_c, th, tiles):
            part = None
            for j in range(tiles):
                ind = jnp.where(sm_c[:, j * LANES:(j + 1) * LANES] >= th, 1.0, 0.0)
                part = ind if part is None else part + ind
            return jnp.dot(part.astype(BF16), ones_sq, preferred_element_type=F32)

        row_max = jnp.max(sm, axis=-1, keepdims=True)
        row_min = jnp.min(jnp.where(adm, score, jnp.inf), axis=-1, keepdims=True)
        lo0 = jnp.broadcast_to(row_min, (rows, LANES))
        hi0 = jnp.broadcast_to(2.0 * row_max - row_min + 1.0, (rows, LANES))

        rc = rows // BISECT_CHAINS
        chains = [slice(i * rc, (i + 1) * rc) for i in range(BISECT_CHAINS)]
        tiles = [-(-max(row_limit(r) for r in range(ch.start, ch.stop)) // LANES) for ch in chains]

        def bisect(_, carry):
            out = []
            for ch, nt, (lo, hi) in zip(chains, tiles, carry):
                mid = 0.5 * (lo + hi)
                enough = count_ge(sm[ch], mid, nt) >= topk
                out.append((jnp.where(enough, mid, lo), jnp.where(enough, hi, mid)))
            return tuple(out)

        bounds = lax.fori_loop(0, BISECT_ITERS, bisect, tuple((lo0[ch], hi0[ch]) for ch in chains),
                               unroll=BISECT_UNROLL)
        lo = jnp.concatenate([b[0] for b in bounds], axis=0)
        hi = jnp.concatenate([b[1] for b in bounds], axis=0)

        need = topk - count_ge(sm, hi, nl)
        take_all = limit <= topk
        carry = jnp.zeros((rows, LANES), F32)
        bias = []
        for j in range(nl):
            sm_j = sm[:, j * LANES:(j + 1) * LANES]
            above = sm_j >= hi
            group = jnp.logical_and(sm_j >= lo, jnp.logical_not(above))
            pc = jnp.dot(jnp.where(group, 1.0, 0.0).astype(BF16), cnt_rhs, preferred_element_type=F32)
            fill = carry + pc[:, :LANES] <= need
            carry = carry + pc[:, LANES:]
            picked = jnp.logical_or(above, jnp.logical_and(group, fill))
            sel = jnp.logical_and(sm_j > -jnp.inf, jnp.logical_or(take_all, picked))
            bias.append(jnp.where(sel, 0.0, MASK_BIAS))
        return jnp.concatenate(bias, axis=1)

    if max(row_limit(r) for r in range(rows)) <= topk:
        bias = jnp.where(adm, 0.0, MASK_BIAS)
    else:
        bias = topk_bias()

    for g in range(n_grp):
        q = q_ref[g]
        km, vm = k_ref[g], v_ref[g]
        kx, vx = (pad_rows(kx_ref[g]), pad_rows(vx_ref[g])) if n_extra else (None, None)
        bias_g = jnp.concatenate([bias[g * tq:(g + 1) * tq]] * GQA_GROUP, axis=0)
        slabs = []
        for gi in range(N_KV_HEADS):
            gs = slice(gi * LANES, (gi + 1) * LANES)
            qs = jnp.concatenate([q[:, (gi * GQA_GROUP + j) * LANES:(gi * GQA_GROUP + j + 1) * LANES]
                                  for j in range(GQA_GROUP)], axis=0)
            s = dot_keys(qs, km[:, gs], kx[:, gs] if n_extra else None) + bias_g
            m = jnp.max(s, axis=-1, keepdims=True)
            pexp = jnp.exp((s - m).astype(BF16))
            o = jnp.dot(pexp[:, :n_main], vm[:, gs], preferred_element_type=F32)
            if n_extra:
                o = o + jnp.dot(pexp[:, n_main:], vx[:, gs], preferred_element_type=F32)
            o = o / pltpu.roll(o, HEAD_DIM, 1)
            slabs.extend(o[j * tq:(j + 1) * tq] for j in range(GQA_GROUP))
        o_ref[g] = _pack_heads(slabs).astype(o_ref.dtype)


def _attention(q3, qi3, misc3, main, extra, cnt_rhs, *, n_grp, tq, n_main, qb, pos0, n_valid, topk):
    b = q3.shape[0]
    n_extra = extra[0].shape[1] if extra else 0
    qmap = lambda bi: (bi, qb, 0)
    kmap = lambda bi: (bi, 0, 0)
    widths = (N_KV_HEADS * LANES, N_KV_HEADS * LANES, LANES)
    key_specs = [pl.BlockSpec((n_grp, n_main, w), kmap) for w in widths]
    if extra:
        key_specs += [pl.BlockSpec((n_grp, n_extra, w), kmap) for w in widths]
    return pl.pallas_call(
        functools.partial(_attn_kernel, n_grp=n_grp, tq=tq, n_main=n_main, n_extra=n_extra, qb=qb, pos0=pos0,
                          n_valid=n_valid, topk=topk),
        grid=(b // n_grp,),
        in_specs=[pl.BlockSpec((n_grp, tq, N_HEADS * LANES), qmap),
                  pl.BlockSpec((n_grp, tq, IDX_HEADS * LANES), qmap),
                  pl.BlockSpec((n_grp, tq, LANES), qmap)] + key_specs
        + [pl.BlockSpec(cnt_rhs.shape, lambda bi: (0, 0))],
        out_specs=pl.BlockSpec((n_grp, tq, ATT_WIDTH), lambda bi: (bi, 0, 0)),
        out_shape=jax.ShapeDtypeStruct((b, tq, ATT_WIDTH), BF16),
        compiler_params=pltpu.CompilerParams(dimension_semantics=("parallel",), vmem_limit_bytes=VMEM_LIMIT),
        name=f"dsa_attn_{n_main + n_extra}",
    )(q3, qi3, misc3, *main, *(extra or ()), cnt_rhs)


def _outffn_kernel(x_ref, oa_ref, ob_ref, wo_a_ref, wo_b_ref, g_ref, w1_ref, w2_ref, y_ref, *, ff_chunk):
    mix = (jnp.dot(oa_ref[...], wo_a_ref[...], preferred_element_type=F32)
           + jnp.dot(ob_ref[...], wo_b_ref[...], preferred_element_type=F32))
    h = x_ref[...] + mix
    ms = jnp.mean(h * h, axis=-1, keepdims=True)
    hn = (h * lax.rsqrt(ms + NORM_EPS) * g_ref[...]).astype(BF16)
    acc = h
    for j in range(D_FF // ff_chunk):
        act = jnp.dot(hn, w1_ref[:, j * ff_chunk:(j + 1) * ff_chunk], preferred_element_type=F32)
        act = jnp.square(jnp.maximum(act, 0.0)).astype(BF16)
        acc = acc + jnp.dot(act, w2_ref[j * ff_chunk:(j + 1) * ff_chunk, :], preferred_element_type=F32)
    y_ref[...] = acc


def _outffn(x2d, o_att, o_rwkv, p, tm):
    n = x2d.shape[0]
    row = lambda i: (i, 0)
    const = lambda i: (0, 0)
    params = [p["wo_a"], p["wo_b"], p["norm_ffn"], p["w_ff1"], p["w_ff2"]]
    return pl.pallas_call(
        functools.partial(_outffn_kernel, ff_chunk=1024),
        grid=(n // tm,),
        in_specs=[pl.BlockSpec((tm, D_MODEL), row), pl.BlockSpec((tm, ATT_WIDTH), row),
                  pl.BlockSpec((tm, RWKV_WIDTH), row)]
        + [pl.BlockSpec(a.shape, const, pipeline_mode=pl.Buffered(1)) for a in params],
        out_specs=pl.BlockSpec((tm, D_MODEL), row),
        out_shape=jax.ShapeDtypeStruct((n, D_MODEL), F32),
        compiler_params=pltpu.CompilerParams(dimension_semantics=("parallel",), vmem_limit_bytes=VMEM_LIMIT),
        name="outproj_ffn",
    )(x2d, o_att, o_rwkv, *params)


def _block_diag_ones():
    i = np.arange(ATT_WIDTH)
    return jnp.asarray((i[:, None] // HEAD_DIM) == (i[None, :] // HEAD_DIM), BF16)


def _count_rhs():
    i = np.arange(LANES)
    tri = (i[:, None] <= i[None, :]).astype(np.float32)
    return jnp.asarray(np.concatenate([tri, np.ones((LANES, LANES), np.float32)], axis=1), BF16)


def _prep_params(wl):
    (norm_mix, w_in, q_gain, k_gain, kidx_ln_g, kidx_ln_b, mu_shift, w0, w2, a0, a2, g2, k_k, k_a, r_k,
     ln_x_g, ln_x_b, w_out, norm_ffn, w_ff1, w_ff2) = wl
    row = lambda a: a.reshape(1, -1).astype(F32)
    o_ki = QKV_WIDTH
    o_wi = o_ki + IDX_DIM
    o_zs = o_wi + IDX_HEADS
    pad_cols = lambda a, wdt: jnp.pad(a, ((0, 0), (0, wdt - a.shape[1])))
    lora_rows = lambda a, off: jnp.pad(a, ((off, LORA_PAD - off - a.shape[0]), (0, 0))).astype(BF16)
    return {
        "norm_mix": row(norm_mix),
        "wa": w_in[:, :QKV_WIDTH].astype(BF16),
        "wb": pad_cols(w_in[:, o_ki:o_zs], LANES).astype(BF16),
        "wc": pad_cols(w_in[:, o_zs:], SHIFT_PAD).astype(BF16),
        "q_gain": row(jnp.tile(q_gain, N_HEADS)),
        "k_gain": row(jnp.tile(k_gain, N_KV_HEADS)),
        "ln_g": pad_cols(row(kidx_ln_g), LANES),
        "ln_b": pad_cols(row(kidx_ln_b), LANES),
        "bd": _block_diag_ones(),
        "mu": pad_cols(row(mu_shift), SHIFT_PAD),
        "w0": row(w0), "a0": row(a0),
        "w2": lora_rows(w2, 0), "a2": lora_rows(a2, W_LORA), "g2": lora_rows(g2, W_LORA + A_LORA),
        "k_k": row(k_k), "k_a": row(k_a), "r_k": row(r_k), "lnx_g": row(ln_x_g), "lnx_b": row(ln_x_b),
        "wo_a": w_out[:ATT_WIDTH].astype(BF16), "wo_b": w_out[ATT_WIDTH:].astype(BF16),
        "norm_ffn": row(norm_ffn),
        "w_ff1": w_ff1.astype(BF16), "w_ff2": w_ff2.astype(BF16),
    }


def _layer(x, pos0, k_past, v_past, ki_past, wkv0, shift0, p, cnt_rhs, *, chunk, n_grp, tq):
    b, t, _ = x.shape
    n = b * t
    past = k_past.shape[1]
    x2d = x.reshape(n, D_MODEL)
    tm = min(ROW_TILE, n)
    q, k, v, qi, misc, zs, kbf, vbf, kibf = _inproj(x2d, t, pos0, p, tm)

    shift0p = jnp.pad(shift0, ((0, 0), (0, 0), (0, SHIFT_PAD - SHIFT_WIDTH)))
    zs3 = zs.reshape(b, t, SHIFT_PAD)
    o_rwkv, wkv_new = _rwkv(zs3, shift0p, wkv0, p, chunk, min(b, RWKV_BATCH))

    n_valid = past + t
    topk = min(TOPK_MAX, n_valid // 4)
    three = lambda a: a.reshape(b, t, a.shape[-1])
    k_new, v_new, ki_new = three(kbf), three(vbf), three(kibf)
    q3, qi3, misc3 = three(q), three(qi), three(misc)
    if past:
        def spread(old, fill):
            old = old.reshape(b, past, -1, HEAD_DIM)
            gap = jnp.full((b, past, LANES - HEAD_DIM), fill, BF16)
            return jnp.concatenate(
                [piece for g in range(old.shape[2]) for piece in (old[:, :, g, :].astype(BF16), gap)], axis=-1)

        main = (spread(k_past, 0.0), spread(v_past, 1.0), spread(ki_past, 0.0))
        o_att = _attention(q3, qi3, misc3, main, (k_new, v_new, ki_new), cnt_rhs, n_grp=n_grp, tq=tq,
                           n_main=past, qb=0, pos0=pos0, n_valid=n_valid, topk=topk)
    else:
        o_att = jnp.concatenate(
            [_attention(q3, qi3, misc3, (k_new, v_new, ki_new), None, cnt_rhs, n_grp=n_grp, tq=tq,
                        n_main=(j + 1) * tq, qb=j, pos0=pos0, n_valid=n_valid, topk=topk)
             for j in range(t // tq)], axis=1)

    y = _outffn(x2d, o_att.reshape(n, ATT_WIDTH), o_rwkv.reshape(n, RWKV_WIDTH), p, tm)
    return (y.reshape(b, t, D_MODEL), k.reshape(b, t, N_KV_HEADS, HEAD_DIM),
            v.reshape(b, t, N_KV_HEADS, HEAD_DIM), misc3[:, :, :IDX_DIM], wkv_new,
            zs3[:, t - 1:, :SHIFT_WIDTH])


def kernel(x_prompt, x_sample, cache_k, cache_v, cache_kidx, state_wkv, state_shift, norm_mix, w_in, q_gain, k_gain, kidx_ln_g, kidx_ln_b, mu_shift, w0, w2, a0, a2, g2, k_k, k_a, r_k, ln_x_g, ln_x_b, w_out, norm_ffn, w_ff1, w_ff2):
    depth = norm_mix.shape[0]
    bp, tp = x_prompt.shape[0], x_prompt.shape[1]
    bs, ts = x_sample.shape[0], x_sample.shape[1]
    past_len = cache_k.shape[2]
    cnt_rhs = _count_rhs()
    no_kv = jnp.zeros((bp, 0, KV_WIDTH), F32)
    no_ki = jnp.zeros((bp, 0, IDX_DIM), F32)
    wkv_zero = jnp.zeros((bp, RWKV_HEADS, HEAD_DIM, HEAD_DIM), F32)
    shift_zero = jnp.zeros((bp, 1, SHIFT_WIDTH), F32)

    y_p, y_s = x_prompt, x_sample
    outs_p, outs_s = [], []
    for l in range(depth):
        wl = (norm_mix[l], w_in[l], q_gain[l], k_gain[l], kidx_ln_g[l], kidx_ln_b[l], mu_shift[l],
              w0[l], w2[l], a0[l], a2[l], g2[l], k_k[l], k_a[l], r_k[l], ln_x_g[l], ln_x_b[l],
              w_out[l], norm_ffn[l], w_ff1[l], w_ff2[l])
        p = _prep_params(wl)
        y_p, *rest_p = _layer(y_p, 0, no_kv, no_kv, no_ki, wkv_zero, shift_zero, p, cnt_rhs,
                              chunk=64, n_grp=1, tq=256)
        y_s, *rest_s = _layer(y_s, past_len, cache_k[l], cache_v[l], cache_kidx[l], state_wkv[l],
                              state_shift[l], p, cnt_rhs, chunk=ts, n_grp=min(bs, 256 // ts), tq=ts)
        outs_p.append(rest_p)
        outs_s.append(rest_s)
    stack = lambda outs, i: jnp.stack([o[i] for o in outs])
    return (y_p, y_s,
            *(stack(outs_p, i) for i in range(5)),
            *(stack(outs_s, i) for i in range(5)))
```

```python
import functools

import numpy as np
import jax
import jax.numpy as jnp
from jax import lax
from jax.experimental import pallas as pl
from jax.experimental.pallas import tpu as pltpu

F32 = jnp.float32
BF16 = jnp.bfloat16

D_MODEL = 1024
CHUNK = 64
HEAD_DIM = 64
ATT_WIDTH = D_MODEL // 2
N_HEADS = ATT_WIDTH // HEAD_DIM
N_KV_HEADS = N_HEADS // 2
GQA_GROUP = N_HEADS // N_KV_HEADS
KV_WIDTH = N_KV_HEADS * HEAD_DIM
ROT_DIM = HEAD_DIM // 4
ROPE_THETA = 500000.0
IDX_HEADS = 8
IDX_DIM = 64
IDX_WIDTH = IDX_HEADS * IDX_DIM
TOPK_MAX = 256
RWKV_WIDTH = D_MODEL - ATT_WIDTH
RWKV_HEADS = RWKV_WIDTH // HEAD_DIM
W_LORA = 32
A_LORA = 32
G_LORA = 96
LORA_WIDTH = W_LORA + A_LORA + G_LORA
SHIFT_WIDTH = 3 * RWKV_WIDTH + LORA_WIDTH
D_FF = 4 * D_MODEL
NORM_EPS = 1e-6
GN_EPS = 64e-5
L2_EPS = 1e-12

LANES = 128
SHIFT_PAD = 3 * RWKV_WIDTH + 2 * LANES
LORA_PAD = SHIFT_PAD - 3 * RWKV_WIDTH
QKV_WIDTH = ATT_WIDTH + 2 * KV_WIDTH + IDX_WIDTH
INV_BLOCK = 16
INV_SHIFT = 4
CHUNK_SHIFT = 6
MASK_BIAS = -1e30
ROW_TILE = 256
RWKV_BATCH = 4
STACK_ROWS = 512
BISECT_ITERS = 32
BISECT_CHAINS = 4
BISECT_UNROLL = 2
VMEM_LIMIT = 56 * 1024 * 1024

NN = (((1,), (0,)), ((), ()))
NT = (((1,), (1,)), ((), ()))
TN = (((0,), (0,)), ((), ()))


def _dot(a, b, dims=NN):
    return lax.dot_general(a.astype(BF16), b.astype(BF16), dims, preferred_element_type=F32)


def _group_sum(x, bd):
    w = x.shape[-1]
    return _dot(x, bd[:w, :w])


def _rope(x, cos, sa, sb):
    w = x.shape[-1]
    reps = w // LANES
    if reps > 1:
        cos, sa, sb = (jnp.tile(t, (1, reps)) for t in (cos, sa, sb))
    half = ROT_DIM // 2
    return x * cos + pltpu.roll(x, w - half, 1) * sa + pltpu.roll(x, half, 1) * sb


def _spread_heads(x, fill):
    low = lax.broadcasted_iota(jnp.int32, (x.shape[0], LANES), 1) < HEAD_DIM
    cols = []
    for c in range(x.shape[1] // LANES):
        pair = x[:, c * LANES:(c + 1) * LANES]
        cols.append(jnp.where(low, pair, fill))
        cols.append(jnp.where(low, pltpu.roll(pair, HEAD_DIM, 1), fill))
    return jnp.concatenate(cols, axis=1)


def _pack_heads(slabs):
    low = lax.broadcasted_iota(jnp.int32, slabs[0].shape, 1) < HEAD_DIM
    cols = [jnp.where(low, slabs[i], pltpu.roll(slabs[i + 1], HEAD_DIM, 1)) for i in range(0, len(slabs), 2)]
    return jnp.concatenate(cols, axis=1)


def _inproj_kernel(x_ref, g_ref, wa_ref, wb_ref, wc_ref, qg_ref, kg_ref, lng_ref, lnb_ref, bd_ref,
                   cos_ref, sa_ref, sb_ref,
                   q_ref, k_ref, v_ref, qi_ref, misc_ref, zs_ref, kbf_ref, vbf_ref, kibf_ref):
    x = x_ref[...]
    ms = jnp.mean(x * x, axis=-1, keepdims=True)
    xn = (x * lax.rsqrt(ms + NORM_EPS) * g_ref[...]).astype(BF16)
    za = jnp.dot(xn, wa_ref[...], preferred_element_type=F32)
    zb = jnp.dot(xn, wb_ref[...], preferred_element_type=F32)
    zs_ref[...] = jnp.dot(xn, wc_ref[...], preferred_element_type=F32)

    cos, sa, sb = cos_ref[...], sa_ref[...], sb_ref[...]
    bd = bd_ref[...]

    zq = za[:, :ATT_WIDTH]
    qn = zq * lax.rsqrt(_group_sum(zq * zq, bd) * (1.0 / HEAD_DIM) + NORM_EPS) * qg_ref[...]
    q_ref[...] = _spread_heads(_rope(qn, cos, sa, sb) * (HEAD_DIM ** -0.5), 0.0).astype(BF16)

    zk = za[:, ATT_WIDTH:ATT_WIDTH + KV_WIDTH]
    kn = zk * lax.rsqrt(_group_sum(zk * zk, bd) * (1.0 / HEAD_DIM) + NORM_EPS) * kg_ref[...]
    k = _rope(kn, cos, sa, sb)
    k_ref[...] = k
    kbf_ref[...] = _spread_heads(k, 0.0).astype(BF16)

    v = za[:, ATT_WIDTH + KV_WIDTH:ATT_WIDTH + 2 * KV_WIDTH]
    v_ref[...] = v
    vbf_ref[...] = _spread_heads(v, 1.0).astype(BF16)

    zqi = za[:, ATT_WIDTH + 2 * KV_WIDTH:]
    qi_ref[...] = _spread_heads(_rope(zqi, cos, sa, sb) * (IDX_DIM ** -0.5), 0.0).astype(BF16)

    lane = lax.broadcasted_iota(jnp.int32, zb.shape, 1)
    is_ki = lane < IDX_DIM
    mu = jnp.sum(jnp.where(is_ki, zb, 0.0), axis=-1, keepdims=True) * (1.0 / IDX_DIM)
    d = jnp.where(is_ki, zb - mu, 0.0)
    var = jnp.sum(d * d, axis=-1, keepdims=True) * (1.0 / IDX_DIM)
    kin = d * lax.rsqrt(var + NORM_EPS) * lng_ref[...] + lnb_ref[...]
    ki = _rope(kin, cos, sa, sb)
    misc_ref[...] = jnp.where(is_ki, ki, zb * (IDX_HEADS ** -0.5))
    kibf_ref[...] = jnp.where(is_ki, ki, 0.0).astype(BF16)


def _rope_tables(pos):
    half = ROT_DIM // 2
    inv = ROPE_THETA ** (-jnp.arange(0, ROT_DIM, 2, dtype=F32) / ROT_DIM)
    ang = pos.astype(F32)[:, None] * inv[None, :]
    cos, sin = jnp.cos(ang), jnp.sin(ang)
    n = pos.shape[0]
    ones = jnp.ones((n, HEAD_DIM - ROT_DIM), F32)
    zeros_h = jnp.zeros((n, half), F32)
    zeros_r = jnp.zeros((n, HEAD_DIM - ROT_DIM), F32)
    c64 = jnp.concatenate([cos, cos, ones], axis=1)
    sa64 = jnp.concatenate([-sin, zeros_h, zeros_r], axis=1)
    sb64 = jnp.concatenate([zeros_h, sin, zeros_r], axis=1)
    rep = LANES // HEAD_DIM
    return tuple(jnp.tile(t, (1, rep)) for t in (c64, sa64, sb64))


def _inproj(x2d, seq_len, pos0, p, tm):
    n = x2d.shape[0]
    tab_rows = max(seq_len, tm)
    pos = pos0 + (jnp.arange(tab_rows, dtype=jnp.int32) % seq_len)
    cos, sa, sb = _rope_tables(pos)
    nblk_t = tab_rows // tm

    def row(i):
        return (i, 0)

    def const(i):
        return (0, 0)

    def tab(i):
        return (i % nblk_t, 0)

    full = lambda a: pl.BlockSpec(a.shape, const)
    outs = [
        (N_HEADS * LANES, BF16), (KV_WIDTH, F32), (KV_WIDTH, F32), (IDX_HEADS * LANES, BF16), (LANES, F32),
        (SHIFT_PAD, F32), (N_KV_HEADS * LANES, BF16), (N_KV_HEADS * LANES, BF16), (LANES, BF16),
    ]
    params = [p["norm_mix"], p["wa"], p["wb"], p["wc"], p["q_gain"], p["k_gain"], p["ln_g"], p["ln_b"], p["bd"]]
    return pl.pallas_call(
        _inproj_kernel,
        grid=(n // tm,),
        in_specs=[pl.BlockSpec((tm, D_MODEL), row)] + [full(a) for a in params]
        + [pl.BlockSpec((tm, LANES), tab)] * 3,
        out_specs=[pl.BlockSpec((tm, w), row) for w, _ in outs],
        out_shape=[jax.ShapeDtypeStruct((n, w), dt) for w, dt in outs],
        compiler_params=pltpu.CompilerParams(dimension_semantics=("parallel",), vmem_limit_bytes=VMEM_LIMIT),
        name="inproj",
    )(x2d, *params, cos, sa, sb)


def _solve_unit_lower(lows, rhss, eye, blk):
    c = lows[0].shape[0]
    d = [jnp.where(blk, l, 0.0) for l in lows]
    e = [l - dh for l, dh in zip(lows, d)]
    dk = [_dot(dh, dh) for dh in d]
    p = [_dot(eye - dh, eye + d2) for dh, d2 in zip(d, dk)]
    span = 4
    while span < INV_BLOCK:
        dk = [_dot(m, m) for m in dk]
        p = [_dot(ph, eye + m) for ph, m in zip(p, dk)]
        span *= 2
    x = [_dot(ph, r) for ph, r in zip(p, rhss)]
    nb = c // INV_BLOCK
    if nb > 1:
        nk = [_dot(ph, eh) for ph, eh in zip(p, e)]
        x = [_dot(eye - m, xh) for m, xh in zip(nk, x)]
        order = 2
        while order < nb:
            nk = [_dot(m, m) for m in nk]
            x = [_dot(eye + m, xh) for m, xh in zip(nk, x)]
            order *= 2
    return x


def _rwkv_kernel(zs_ref, shift0_ref, wkv0_ref, mu_ref, w0_ref, w2_ref, a0_ref, a2_ref, g2_ref,
                 kk_ref, ka_ref, rk_ref, lng_ref, lnb_ref, bd_ref,
                 o_ref, sout_ref, s_scr, prev_scr, *, chunk):
    c = chunk
    nb = zs_ref.shape[0]
    ci = pl.program_id(1)
    w = RWKV_WIDTH

    @pl.when(ci == 0)
    def _():
        s_scr[...] = wkv0_ref[...]
        prev_scr[...] = shift0_ref[...]

    row = lax.broadcasted_iota(jnp.int32, (c, c), 0)
    col = lax.broadcasted_iota(jnp.int32, (c, c), 1)
    tri_incl = row >= col
    tri_strict = row > col
    eye = jnp.where(row == col, 1.0, 0.0)
    blk = jnp.right_shift(row, INV_SHIFT) == jnp.right_shift(col, INV_SHIFT)
    tri = jnp.where(tri_incl, 1.0, 0.0).astype(BF16)
    mu = mu_ref[...]
    bd = bd_ref[...]
    sls = [slice(h * HEAD_DIM, (h + 1) * HEAD_DIM) for h in range(RWKV_HEADS)]

    ar, km, bm, vh, g_last, gates, bonus_rk = [], [], [], [], [], [], []
    for j in range(nb):
        z = zs_ref[j]
        prev_row = prev_scr[j]

        def mixed(lo, hi):
            zp = z[:, lo:hi]
            rolled = pltpu.roll(zp, 1, 0)
            first = lax.broadcasted_iota(jnp.int32, zp.shape, 0) == 0
            prv = jnp.where(first, prev_row[:, lo:hi], rolled)
            return zp + mu[:, lo:hi] * (prv - zp)

        r = mixed(0, w)
        k = mixed(w, 2 * w)
        v = mixed(2 * w, 3 * w)
        tail = mixed(3 * w, SHIFT_PAD)
        prev_scr[j] = z[c - 1:c, :]

        lw = w0_ref[...] + _dot(jnp.tanh(tail), w2_ref[...])
        nlw = -lw
        softplus = jnp.maximum(nlw, 0.0) + jnp.log(1.0 + jnp.exp(-jnp.abs(nlw)))
        logd = -jnp.exp(-softplus - 0.5)
        a = 1.0 / (1.0 + jnp.exp(-(a0_ref[...] + _dot(tail, a2_ref[...]))))
        gates.append(_dot(1.0 / (1.0 + jnp.exp(-tail)), g2_ref[...]))

        kk = k * kk_ref[...]
        kk = kk / jnp.maximum(jnp.sqrt(_group_sum(kk * kk, bd)), L2_EPS)
        k = k * (1.0 + (a - 1.0) * ka_ref[...])
        b = kk * a

        hi = logd.astype(BF16)
        rem = logd - hi.astype(F32)
        mid = rem.astype(BF16)
        lo = (rem - mid.astype(F32)).astype(BF16)
        lg = (jnp.dot(tri, hi, preferred_element_type=F32) + jnp.dot(tri, mid, preferred_element_type=F32)
              + jnp.dot(tri, lo, preferred_element_type=F32))
        g = jnp.exp(lg)
        ginv = jnp.exp(-lg)
        a_m = jnp.exp(lg - logd) * kk
        b_m = b * ginv
        k_m = k * ginv
        p_m = g * r
        bonus_rk.append(r * k * rk_ref[...])
        ar += [jnp.concatenate([a_m[:, sl], p_m[:, sl]], axis=0).astype(BF16) for sl in sls]
        km += [k_m[:, sl].astype(BF16) for sl in sls]
        bm += [b_m[:, sl].astype(BF16) for sl in sls]
        vh += [v[:, sl] for sl in sls]
        g_last += [g[c - 1:c, sl] for sl in sls]

    chains = range(nb * RWKV_HEADS)
    s_h = [s_scr[j, h] for j in range(nb) for h in range(RWKV_HEADS)]
    g_k = [_dot(ar[i], km[i], NT) for i in chains]
    g_b = [_dot(ar[i], bm[i], NT) for i in chains]
    a_s = [_dot(ar[i], s_h[i], NT) for i in chains]
    rhs = [a_s[i][:c] + _dot(jnp.where(tri_strict, g_k[i][:c], 0.0), vh[i]) for i in chains]
    u = _solve_unit_lower([jnp.where(tri_strict, g_b[i][:c], 0.0) for i in chains], rhs, eye, blk)
    ys, bonus = [], []
    for i in chains:
        j, h = divmod(i, RWKV_HEADS)
        n_pk = jnp.where(tri_incl, g_k[i][c:], 0.0)
        n_pb = jnp.where(tri_incl, g_b[i][c:], 0.0)
        y = a_s[i][c:] + _dot(n_pk, vh[i]) - _dot(n_pb, u[i])
        s_scr[j, h] = (s_h[i] + _dot(vh[i], km[i], TN) - _dot(u[i], bm[i], TN)) * g_last[i]
        m = jnp.mean(y, axis=-1, keepdims=True)
        yc = y - m
        var = jnp.mean(yc * yc, axis=-1, keepdims=True)
        ys.append(yc * lax.rsqrt(var + GN_EPS))
        bonus.append(jnp.sum(bonus_rk[j][:, sls[h]], axis=-1, keepdims=True) * vh[i])

    for j in range(nb):
        hs = slice(j * RWKV_HEADS, (j + 1) * RWKV_HEADS)
        yn = jnp.concatenate(ys[hs], axis=1) * lng_ref[...] + lnb_ref[...]
        o_ref[j] = ((yn + jnp.concatenate(bonus[hs], axis=1)) * gates[j]).astype(o_ref.dtype)

    @pl.when(ci == pl.num_programs(1) - 1)
    def _():
        sout_ref[...] = s_scr[...]


def _rwkv(zs3, shift0, wkv0, p, chunk, nb):
    b, t, _ = zs3.shape
    const = lambda bi, ci: (0, 0)
    full = lambda a: pl.BlockSpec(a.shape, const)
    params = [p["mu"], p["w0"], p["w2"], p["a0"], p["a2"], p["g2"], p["k_k"], p["k_a"], p["r_k"],
              p["lnx_g"], p["lnx_b"], p["bd"]]
    state_spec = pl.BlockSpec((nb, RWKV_HEADS, HEAD_DIM, HEAD_DIM), lambda bi, ci: (bi, 0, 0, 0))
    return pl.pallas_call(
        functools.partial(_rwkv_kernel, chunk=chunk),
        grid=(b // nb, t // chunk),
        in_specs=[pl.BlockSpec((nb, chunk, SHIFT_PAD), lambda bi, ci: (bi, ci, 0)),
                  pl.BlockSpec((nb, 1, SHIFT_PAD), lambda bi, ci: (bi, 0, 0)),
                  state_spec] + [full(a) for a in params],
        out_specs=[pl.BlockSpec((nb, chunk, RWKV_WIDTH), lambda bi, ci: (bi, ci, 0)), state_spec],
        out_shape=[jax.ShapeDtypeStruct((b, t, RWKV_WIDTH), BF16),
                   jax.ShapeDtypeStruct((b, RWKV_HEADS, HEAD_DIM, HEAD_DIM), F32)],
        scratch_shapes=[pltpu.VMEM((nb, RWKV_HEADS, HEAD_DIM, HEAD_DIM), F32),
                        pltpu.VMEM((nb, 1, SHIFT_PAD), F32)],
        compiler_params=pltpu.CompilerParams(dimension_semantics=("parallel", "arbitrary"),
                                             vmem_limit_bytes=VMEM_LIMIT),
        name="rwkv7",
    )(zs3, shift0, wkv0, *params)


def _attn_kernel(*refs, n_grp, tq, n_main, n_extra, qb, pos0, n_valid, topk):
    if n_extra:
        q_ref, qi_ref, misc_ref, k_ref, v_ref, ki_ref, kx_ref, vx_ref, kix_ref, cnt_ref, o_ref = refs
    else:
        q_ref, qi_ref, misc_ref, k_ref, v_ref, ki_ref, cnt_ref, o_ref = refs
    n_keys = n_main + (LANES if n_extra else 0)
    nl = n_keys // LANES
    rows = n_grp * tq

    def pad_rows(x):
        return jnp.concatenate([x, jnp.zeros((LANES - n_extra, x.shape[1]), x.dtype)], axis=0)

    def dot_keys(lhs, main, extra):
        out = lax.dot_general(lhs, main, NT, preferred_element_type=F32)
        if n_extra:
            out = jnp.concatenate([out, lax.dot_general(lhs, extra, NT, preferred_element_type=F32)], axis=1)
        return out

    def row_limit(r):
        return min(((pos0 + qb * tq + r % tq) // CHUNK + 1) * CHUNK, n_valid)

    pos = pos0 + qb * tq + lax.broadcasted_iota(jnp.int32, (tq, LANES), 0)
    limit = jnp.minimum((jnp.right_shift(pos, CHUNK_SHIFT) + 1) * CHUNK, n_valid)
    if n_grp > 1:
        limit = jnp.concatenate([limit] * n_grp, axis=0)
    key_id = lax.broadcasted_iota(jnp.int32, (rows, n_keys), 1)
    adm = key_id < jnp.tile(limit, (1, nl))

    def topk_bias():
        parts = []
        for g in range(n_grp):
            qi = qi_ref[g]
            ki = ki_ref[g]
            kix = pad_rows(kix_ref[g]) if n_extra else None
            misc = misc_ref[g]
            sc = None
            if tq * IDX_HEADS <= STACK_ROWS:
                d_all = dot_keys(jnp.concatenate([qi[:, h * LANES:(h + 1) * LANES] for h in range(IDX_HEADS)],
                                                 axis=0), ki, kix)
            for h in range(IDX_HEADS):
                if tq * IDX_HEADS <= STACK_ROWS:
                    d = d_all[h * tq:(h + 1) * tq]
                else:
                    d = dot_keys(qi[:, h * LANES:(h + 1) * LANES], ki, kix)
                term = misc[:, IDX_DIM + h:IDX_DIM + h + 1] * jnp.maximum(d, 0.0)
                sc = term if sc is None else sc + term
            parts.append(sc)
        score = parts[0] if n_grp == 1 else jnp.concatenate(parts, axis=0)
        sm = jnp.where(adm, score, -jnp.inf)

        cnt_rhs = cnt_ref[...]
        ones_sq = cnt_ref[:, LANES:]

        def count_ge(sm_c, th, tiles):
            part = None
            for j in range(tiles):
                ind = jnp.where(sm_c[:, j * LANES:(j + 1) * LANES] >= th, 1.0, 0.0)
                part = ind if part is None else part + ind
            return jnp.dot(part.astype(BF16), ones_sq, preferred_element_type=F32)

        row_max = jnp.max(sm, axis=-1, keepdims=True)
        row_min = jnp.min(jnp.where(adm, score, jnp.inf), axis=-1, keepdims=True)
        lo0 = jnp.broadcast_to(row_min, (rows, LANES))
        hi0 = jnp.broadcast_to(2.0 * row_max - row_min + 1.0, (rows, LANES))

        rc = rows // BISECT_CHAINS
        chains = [slice(i * rc, (i + 1) * rc) for i in range(BISECT_CHAINS)]
        tiles = [-(-max(row_limit(r) for r in range(ch.start, ch.stop)) // LANES) for ch in chains]

        def bisect(_, carry):
            out = []
            for ch, nt, (lo, hi) in zip(chains, tiles, carry):
                mid = 0.5 * (lo + hi)
                enough = count_ge(sm[ch], mid, nt) >= topk
                out.append((jnp.where(enough, mid, lo), jnp.where(enough, hi, mid)))
            return tuple(out)

        bounds = lax.fori_loop(0, BISECT_ITERS, bisect, tuple((lo0[ch], hi0[ch]) for ch in chains),
                               unroll=BISECT_UNROLL)
        lo = jnp.concatenate([b[0] for b in bounds], axis=0)
        hi = jnp.concatenate([b[1] for b in bounds], axis=0)

        need = topk - count_ge(sm, hi, nl)
        take_all = limit <= topk
        carry = jnp.zeros((rows, LANES), F32)
        bias = []
        for j in range(nl):
            sm_j = sm[:, j * LANES:(j + 1) * LANES]
            above = sm_j >= hi
            group = jnp.logical_and(sm_j >= lo, jnp.logical_not(above))
            pc = jnp.dot(jnp.where(group, 1.0, 0.0).astype(BF16), cnt_rhs, preferred_element_type=F32)
            fill = carry + pc[:, :LANES] <= need
            carry = carry + pc[:, LANES:]
            picked = jnp.logical_or(above, jnp.logical_and(group, fill))
            sel = jnp.logical_and(sm_j > -jnp.inf, jnp.logical_or(take_all, picked))
            bias.append(jnp.where(sel, 0.0, MASK_BIAS))
        return jnp.concatenate(bias, axis=1)

    if max(row_limit(r) for r in range(rows)) <= topk:
        bias = jnp.where(adm, 0.0, MASK_BIAS)
    else:
        bias = topk_bias()

    for g in range(n_grp):
        q = q_ref[g]
        km, vm = k_ref[g], v_ref[g]
        if km.dtype != BF16:
            km = _spread_heads(km, 0.0).astype(BF16)
            vm = _spread_heads(vm, 1.0).astype(BF16)
        kx, vx = (pad_rows(kx_ref[g]), pad_rows(vx_ref[g])) if n_extra else (None, None)
        bias_g = jnp.concatenate([bias[g * tq:(g + 1) * tq]] * GQA_GROUP, axis=0)
        slabs = []
        for gi in range(N_KV_HEADS):
            gs = slice(gi * LANES, (gi + 1) * LANES)
            qs = jnp.concatenate([q[:, (gi * GQA_GROUP + j) * LANES:(gi * GQA_GROUP + j + 1) * LANES]
                                  for j in range(GQA_GROUP)], axis=0)
            s = dot_keys(qs, km[:, gs], kx[:, gs] if n_extra else None) + bias_g
            m = jnp.max(s, axis=-1, keepdims=True)
            pexp = jnp.exp((s - m).astype(BF16))
            o = jnp.dot(pexp[:, :n_main], vm[:, gs], preferred_element_type=F32)
            if n_extra:
                o = o + jnp.dot(pexp[:, n_main:], vx[:, gs], preferred_element_type=F32)
            o = o / pltpu.roll(o, HEAD_DIM, 1)
            slabs.extend(o[j * tq:(j + 1) * tq] for j in range(GQA_GROUP))
        o_ref[g] = _pack_heads(slabs).astype(o_ref.dtype)


def _attention(q3, qi3, misc3, main, extra, cnt_rhs, *, n_grp, tq, n_main, qb, pos0, n_valid, topk):
    b = q3.shape[0]
    n_extra = extra[0].shape[1] if extra else 0
    qmap = lambda bi: (bi, qb, 0)
    kmap = lambda bi: (bi, 0, 0)
    key_specs = [pl.BlockSpec((n_grp, n_main, a.shape[2]), kmap) for a in main]
    if extra:
        key_specs += [pl.BlockSpec((n_grp, n_extra, a.shape[2]), kmap) for a in extra]
    return pl.pallas_call(
        functools.partial(_attn_kernel, n_grp=n_grp, tq=tq, n_main=n_main, n_extra=n_extra, qb=qb, pos0=pos0,
                          n_valid=n_valid, topk=topk),
        grid=(b // n_grp,),
        in_specs=[pl.BlockSpec((n_grp, tq, N_HEADS * LANES), qmap),
                  pl.BlockSpec((n_grp, tq, IDX_HEADS * LANES), qmap),
                  pl.BlockSpec((n_grp, tq, LANES), qmap)] + key_specs
        + [pl.BlockSpec(cnt_rhs.shape, lambda bi: (0, 0))],
        out_specs=pl.BlockSpec((n_grp, tq, ATT_WIDTH), lambda bi: (bi, 0, 0)),
        out_shape=jax.ShapeDtypeStruct((b, tq, ATT_WIDTH), BF16),
        compiler_params=pltpu.CompilerParams(dimension_semantics=("parallel",), vmem_limit_bytes=VMEM_LIMIT),
        name=f"dsa_attn_{n_main + n_extra}",
    )(q3, qi3, misc3, *main, *(extra or ()), cnt_rhs)


def _outffn_kernel(x_ref, oa_ref, ob_ref, wo_a_ref, wo_b_ref, g_ref, w1_ref, w2_ref, y_ref, *, ff_chunk):
    mix = (jnp.dot(oa_ref[...], wo_a_ref[...], preferred_element_type=F32)
           + jnp.dot(ob_ref[...], wo_b_ref[...], preferred_element_type=F32))
    h = x_ref[...] + mix
    ms = jnp.mean(h * h, axis=-1, keepdims=True)
    hn = (h * lax.rsqrt(ms + NORM_EPS) * g_ref[...]).astype(BF16)
    acc = h
    for j in range(D_FF // ff_chunk):
        act = jnp.dot(hn, w1_ref[:, j * ff_chunk:(j + 1) * ff_chunk], preferred_element_type=F32)
        act = jnp.square(jnp.maximum(act, 0.0)).astype(BF16)
        acc = acc + jnp.dot(act, w2_ref[j * ff_chunk:(j + 1) * ff_chunk, :], preferred_element_type=F32)
    y_ref[...] = acc


def _outffn(x2d, o_att, o_rwkv, p, tm):
    n = x2d.shape[0]
    row = lambda i: (i, 0)
    const = lambda i: (0, 0)
    params = [p["wo_a"], p["wo_b"], p["norm_ffn"], p["w_ff1"], p["w_ff2"]]
    return pl.pallas_call(
        functools.partial(_outffn_kernel, ff_chunk=1024),
        grid=(n // tm,),
        in_specs=[pl.BlockSpec((tm, D_MODEL), row), pl.BlockSpec((tm, ATT_WIDTH), row),
                  pl.BlockSpec((tm, RWKV_WIDTH), row)]
        + [pl.BlockSpec(a.shape, const, pipeline_mode=pl.Buffered(1)) for a in params],
        out_specs=pl.BlockSpec((tm, D_MODEL), row),
        out_shape=jax.ShapeDtypeStruct((n, D_MODEL), F32),
        compiler_params=pltpu.CompilerParams(dimension_semantics=("parallel",), vmem_limit_bytes=VMEM_LIMIT),
        name="outproj_ffn",
    )(x2d, o_att, o_rwkv, *params)


def _block_diag_ones():
    i = np.arange(ATT_WIDTH)
    return jnp.asarray((i[:, None] // HEAD_DIM) == (i[None, :] // HEAD_DIM), BF16)


def _count_rhs():
    i = np.arange(LANES)
    tri = (i[:, None] <= i[None, :]).astype(np.float32)
    return jnp.asarray(np.concatenate([tri, np.ones((LANES, LANES), np.float32)], axis=1), BF16)


def _prep_params(wl):
    (norm_mix, w_in, q_gain, k_gain, kidx_ln_g, kidx_ln_b, mu_shift, w0, w2, a0, a2, g2, k_k, k_a, r_k,
     ln_x_g, ln_x_b, w_out, norm_ffn, w_ff1, w_ff2) = wl
    row = lambda a: a.reshape(1, -1).astype(F32)
    o_ki = QKV_WIDTH
    o_wi = o_ki + IDX_DIM
    o_zs = o_wi + IDX_HEADS
    pad_cols = lambda a, wdt: jnp.pad(a, ((0, 0), (0, wdt - a.shape[1])))
    lora_rows = lambda a, off: jnp.pad(a, ((off, LORA_PAD - off - a.shape[0]), (0, 0))).astype(BF16)
    return {
        "norm_mix": row(norm_mix),
        "wa": w_in[:, :QKV_WIDTH].astype(BF16),
        "wb": pad_cols(w_in[:, o_ki:o_zs], LANES).astype(BF16),
        "wc": pad_cols(w_in[:, o_zs:], SHIFT_PAD).astype(BF16),
        "q_gain": row(jnp.tile(q_gain, N_HEADS)),
        "k_gain": row(jnp.tile(k_gain, N_KV_HEADS)),
        "ln_g": pad_cols(row(kidx_ln_g), LANES),
        "ln_b": pad_cols(row(kidx_ln_b), LANES),
        "bd": _block_diag_ones(),
        "mu": pad_cols(row(mu_shift), SHIFT_PAD),
        "w0": row(w0), "a0": row(a0),
        "w2": lora_rows(w2, 0), "a2": lora_rows(a2, W_LORA), "g2": lora_rows(g2, W_LORA + A_LORA),
        "k_k": row(k_k), "k_a": row(k_a), "r_k": row(r_k), "lnx_g": row(ln_x_g), "lnx_b": row(ln_x_b),
        "wo_a": w_out[:ATT_WIDTH].astype(BF16), "wo_b": w_out[ATT_WIDTH:].astype(BF16),
        "norm_ffn": row(norm_ffn),
        "w_ff1": w_ff1.astype(BF16), "w_ff2": w_ff2.astype(BF16),
    }


def _layer(x, pos0, k_past, v_past, ki_past, wkv0, shift0, p, cnt_rhs, *, chunk, n_grp, tq):
    b, t, _ = x.shape
    n = b * t
    past = k_past.shape[1]
    x2d = x.reshape(n, D_MODEL)
    tm = min(ROW_TILE, n)
    q, k, v, qi, misc, zs, kbf, vbf, kibf = _inproj(x2d, t, pos0, p, tm)

    shift0p = jnp.pad(shift0, ((0, 0), (0, 0), (0, SHIFT_PAD - SHIFT_WIDTH)))
    zs3 = zs.reshape(b, t, SHIFT_PAD)
    o_rwkv, wkv_new = _rwkv(zs3, shift0p, wkv0, p, chunk, min(b, RWKV_BATCH))

    n_valid = past + t
    topk = min(TOPK_MAX, n_valid // 4)
    three = lambda a: a.reshape(b, t, a.shape[-1])
    k_new, v_new, ki_new = three(kbf), three(vbf), three(kibf)
    q3, qi3, misc3 = three(q), three(qi), three(misc)
    if past:
        ki_main = jnp.pad(ki_past.astype(BF16), ((0, 0), (0, 0), (0, LANES - IDX_DIM)))
        main = (k_past.reshape(b, past, KV_WIDTH), v_past.reshape(b, past, KV_WIDTH), ki_main)
        o_att = _attention(q3, qi3, misc3, main, (k_new, v_new, ki_new), cnt_rhs, n_grp=n_grp, tq=tq,
                           n_main=past, qb=0, pos0=pos0, n_valid=n_valid, topk=topk)
    else:
        o_att = jnp.concatenate(
            [_attention(q3, qi3, misc3, (k_new, v_new, ki_new), None, cnt_rhs, n_grp=n_grp, tq=tq,
                        n_main=(j + 1) * tq, qb=j, pos0=pos0, n_valid=n_valid, topk=topk)
             for j in range(t // tq)], axis=1)

    y = _outffn(x2d, o_att.reshape(n, ATT_WIDTH), o_rwkv.reshape(n, RWKV_WIDTH), p, tm)
    return (y.reshape(b, t, D_MODEL), k.reshape(b, t, N_KV_HEADS, HEAD_DIM),
            v.reshape(b, t, N_KV_HEADS, HEAD_DIM), misc3[:, :, :IDX_DIM], wkv_new,
            zs3[:, t - 1:, :SHIFT_WIDTH])


def kernel(x_prompt, x_sample, cache_k, cache_v, cache_kidx, state_wkv, state_shift, norm_mix, w_in, q_gain, k_gain, kidx_ln_g, kidx_ln_b, mu_shift, w0, w2, a0, a2, g2, k_k, k_a, r_k, ln_x_g, ln_x_b, w_out, norm_ffn, w_ff1, w_ff2):
    depth = norm_mix.shape[0]
    bp, tp = x_prompt.shape[0], x_prompt.shape[1]
    bs, ts = x_sample.shape[0], x_sample.shape[1]
    past_len = cache_k.shape[2]
    cnt_rhs = _count_rhs()
    no_kv = jnp.zeros((bp, 0, KV_WIDTH), F32)
    no_ki = jnp.zeros((bp, 0, IDX_DIM), F32)
    wkv_zero = jnp.zeros((bp, RWKV_HEADS, HEAD_DIM, HEAD_DIM), F32)
    shift_zero = jnp.zeros((bp, 1, SHIFT_WIDTH), F32)

    y_p, y_s = x_prompt, x_sample
    outs_p, outs_s = [], []
    for l in range(depth):
        wl = (norm_mix[l], w_in[l], q_gain[l], k_gain[l], kidx_ln_g[l], kidx_ln_b[l], mu_shift[l],
              w0[l], w2[l], a0[l], a2[l], g2[l], k_k[l], k_a[l], r_k[l], ln_x_g[l], ln_x_b[l],
              w_out[l], norm_ffn[l], w_ff1[l], w_ff2[l])
        p = _prep_params(wl)
        y_p, *rest_p = _layer(y_p, 0, no_kv, no_kv, no_ki, wkv_zero, shift_zero, p, cnt_rhs,
                              chunk=64, n_grp=1, tq=256)
        y_s, *rest_s = _layer(y_s, past_len, cache_k[l], cache_v[l], cache_kidx[l], state_wkv[l],
                              state_shift[l], p, cnt_rhs, chunk=ts, n_grp=min(bs, 256 // ts), tq=ts)
        outs_p.append(rest_p)
        outs_s.append(rest_s)
    stack = lambda outs, i: jnp.stack([o[i] for o in outs])
    return (y_p, y_s,
            *(stack(outs_p, i) for i in range(5)),
            *(stack(outs_s, i) for i in range(5)))
```

```python
import functools

import numpy as np
import jax
import jax.numpy as jnp
from jax import lax
from jax.experimental import pallas as pl
from jax.experimental.pallas import tpu as pltpu

F32 = jnp.float32
BF16 = jnp.bfloat16

D_MODEL = 1024
CHUNK = 64
HEAD_DIM = 64
ATT_WIDTH = D_MODEL // 2
N_HEADS = ATT_WIDTH // HEAD_DIM
N_KV_HEADS = N_HEADS // 2
GQA_GROUP = N_HEADS // N_KV_HEADS
KV_WIDTH = N_KV_HEADS * HEAD_DIM
ROT_DIM = HEAD_DIM // 4
ROPE_THETA = 500000.0
IDX_HEADS = 8
IDX_DIM = 64
IDX_WIDTH = IDX_HEADS * IDX_DIM
TOPK_MAX = 256
RWKV_WIDTH = D_MODEL - ATT_WIDTH
RWKV_HEADS = RWKV_WIDTH // HEAD_DIM
W_LORA = 32
A_LORA = 32
G_LORA = 96
LORA_WIDTH = W_LORA + A_LORA + G_LORA
SHIFT_WIDTH = 3 * RWKV_WIDTH + LORA_WIDTH
D_FF = 4 * D_MODEL
NORM_EPS = 1e-6
GN_EPS = 64e-5
L2_EPS = 1e-12

LANES = 128
SHIFT_PAD = 3 * RWKV_WIDTH + 2 * LANES
LORA_PAD = SHIFT_PAD - 3 * RWKV_WIDTH
QKV_WIDTH = ATT_WIDTH + 2 * KV_WIDTH + IDX_WIDTH
INV_BLOCK = 16
INV_SHIFT = 4
CHUNK_SHIFT = 6
MASK_BIAS = -1e30
ROW_TILE = 512
RWKV_BATCH = 4
STACK_ROWS = 512
BISECT_ITERS = 32
BISECT_CHAINS = 4
BISECT_UNROLL = 2
VMEM_LIMIT = 56 * 1024 * 1024

NN = (((1,), (0,)), ((), ()))
NT = (((1,), (1,)), ((), ()))
TN = (((0,), (0,)), ((), ()))


def _dot(a, b, dims=NN):
    return lax.dot_general(a.astype(BF16), b.astype(BF16), dims, preferred_element_type=F32)


def _group_sum(x, bd):
    w = x.shape[-1]
    return _dot(x, bd[:w, :w])


def _rope(x, cos, sa, sb):
    w = x.shape[-1]
    reps = w // LANES
    if reps > 1:
        cos, sa, sb = (jnp.tile(t, (1, reps)) for t in (cos, sa, sb))
    half = ROT_DIM // 2
    return x * cos + pltpu.roll(x, w - half, 1) * sa + pltpu.roll(x, half, 1) * sb


def _spread_heads(x, fill):
    low = lax.broadcasted_iota(jnp.int32, (x.shape[0], LANES), 1) < HEAD_DIM
    cols = []
    for c in range(x.shape[1] // LANES):
        pair = x[:, c * LANES:(c + 1) * LANES]
        cols.append(jnp.where(low, pair, fill))
        cols.append(jnp.where(low, pltpu.roll(pair, HEAD_DIM, 1), fill))
    return jnp.concatenate(cols, axis=1)


def _pack_heads(slabs):
    low = lax.broadcasted_iota(jnp.int32, slabs[0].shape, 1) < HEAD_DIM
    cols = [jnp.where(low, slabs[i], pltpu.roll(slabs[i + 1], HEAD_DIM, 1)) for i in range(0, len(slabs), 2)]
    return jnp.concatenate(cols, axis=1)


def _inproj_kernel(x_ref, g_ref, wa_ref, wb_ref, wc_ref, qg_ref, kg_ref, lng_ref, lnb_ref, bd_ref,
                   cos_ref, sa_ref, sb_ref,
                   q_ref, k_ref, v_ref, qi_ref, misc_ref, zs_ref, kbf_ref, vbf_ref, kibf_ref):
    x = x_ref[...]
    ms = jnp.mean(x * x, axis=-1, keepdims=True)
    xn = (x * lax.rsqrt(ms + NORM_EPS) * g_ref[...]).astype(BF16)
    za = jnp.dot(xn, wa_ref[...], preferred_element_type=F32)
    zb = jnp.dot(xn, wb_ref[...], preferred_element_type=F32)
    zs_ref[...] = jnp.dot(xn, wc_ref[...], preferred_element_type=F32)

    cos, sa, sb = cos_ref[...], sa_ref[...], sb_ref[...]
    bd = bd_ref[...]

    zq = za[:, :ATT_WIDTH]
    qn = zq * lax.rsqrt(_group_sum(zq * zq, bd) * (1.0 / HEAD_DIM) + NORM_EPS) * qg_ref[...]
    q_ref[...] = _spread_heads(_rope(qn, cos, sa, sb) * (HEAD_DIM ** -0.5), 0.0).astype(BF16)

    zk = za[:, ATT_WIDTH:ATT_WIDTH + KV_WIDTH]
    kn = zk * lax.rsqrt(_group_sum(zk * zk, bd) * (1.0 / HEAD_DIM) + NORM_EPS) * kg_ref[...]
    k = _rope(kn, cos, sa, sb)
    k_ref[...] = k
    kbf_ref[...] = _spread_heads(k, 0.0).astype(BF16)

    v = za[:, ATT_WIDTH + KV_WIDTH:ATT_WIDTH + 2 * KV_WIDTH]
    v_ref[...] = v
    vbf_ref[...] = _spread_heads(v, 1.0).astype(BF16)

    zqi = za[:, ATT_WIDTH + 2 * KV_WIDTH:]
    qi_ref[...] = _spread_heads(_rope(zqi, cos, sa, sb) * (IDX_DIM ** -0.5), 0.0).astype(BF16)

    lane = lax.broadcasted_iota(jnp.int32, zb.shape, 1)
    is_ki = lane < IDX_DIM
    mu = jnp.sum(jnp.where(is_ki, zb, 0.0), axis=-1, keepdims=True) * (1.0 / IDX_DIM)
    d = jnp.where(is_ki, zb - mu, 0.0)
    var = jnp.sum(d * d, axis=-1, keepdims=True) * (1.0 / IDX_DIM)
    kin = d * lax.rsqrt(var + NORM_EPS) * lng_ref[...] + lnb_ref[...]
    ki = _rope(kin, cos, sa, sb)
    misc_ref[...] = jnp.where(is_ki, ki, zb * (IDX_HEADS ** -0.5))
    kibf_ref[...] = jnp.where(is_ki, ki, 0.0).astype(BF16)


def _rope_tables(pos):
    half = ROT_DIM // 2
    inv = ROPE_THETA ** (-jnp.arange(0, ROT_DIM, 2, dtype=F32) / ROT_DIM)
    ang = pos.astype(F32)[:, None] * inv[None, :]
    cos, sin = jnp.cos(ang), jnp.sin(ang)
    n = pos.shape[0]
    ones = jnp.ones((n, HEAD_DIM - ROT_DIM), F32)
    zeros_h = jnp.zeros((n, half), F32)
    zeros_r = jnp.zeros((n, HEAD_DIM - ROT_DIM), F32)
    c64 = jnp.concatenate([cos, cos, ones], axis=1)
    sa64 = jnp.concatenate([-sin, zeros_h, zeros_r], axis=1)
    sb64 = jnp.concatenate([zeros_h, sin, zeros_r], axis=1)
    rep = LANES // HEAD_DIM
    return tuple(jnp.tile(t, (1, rep)) for t in (c64, sa64, sb64))


def _inproj(x2d, seq_len, pos0, p, tm):
    n = x2d.shape[0]
    tab_rows = max(seq_len, tm)
    pos = pos0 + (jnp.arange(tab_rows, dtype=jnp.int32) % seq_len)
    cos, sa, sb = _rope_tables(pos)
    nblk_t = tab_rows // tm

    def row(i):
        return (i, 0)

    def const(i):
        return (0, 0)

    def tab(i):
        return (i % nblk_t, 0)

    full = lambda a: pl.BlockSpec(a.shape, const)
    outs = [
        (N_HEADS * LANES, BF16), (KV_WIDTH, F32), (KV_WIDTH, F32), (IDX_HEADS * LANES, BF16), (LANES, F32),
        (SHIFT_PAD, F32), (N_KV_HEADS * LANES, BF16), (N_KV_HEADS * LANES, BF16), (LANES, BF16),
    ]
    params = [p["norm_mix"], p["wa"], p["wb"], p["wc"], p["q_gain"], p["k_gain"], p["ln_g"], p["ln_b"], p["bd"]]
    return pl.pallas_call(
        _inproj_kernel,
        grid=(n // tm,),
        in_specs=[pl.BlockSpec((tm, D_MODEL), row)] + [full(a) for a in params]
        + [pl.BlockSpec((tm, LANES), tab)] * 3,
        out_specs=[pl.BlockSpec((tm, w), row) for w, _ in outs],
        out_shape=[jax.ShapeDtypeStruct((n, w), dt) for w, dt in outs],
        compiler_params=pltpu.CompilerParams(dimension_semantics=("parallel",), vmem_limit_bytes=VMEM_LIMIT),
        name="inproj",
    )(x2d, *params, cos, sa, sb)


def _left_head(shape):
    return lax.broadcasted_iota(jnp.int32, shape, 1) < shape[1] // 2


def _bdiag(x):
    x = x.astype(BF16)
    zero = jnp.zeros_like(x)
    left = _left_head(x.shape)
    return jnp.concatenate([jnp.where(left, x, zero), jnp.where(left, zero, x)], axis=0)


def _pdot(x, y, dims=NN):
    return lax.dot_general(x.astype(BF16), _bdiag(y), dims, preferred_element_type=F32)


def _solve_unit_lower(lows, rhss, eye, blk):
    c = lows[0].shape[0]
    d = [jnp.where(blk, l, 0.0) for l in lows]
    e = [l - dh for l, dh in zip(lows, d)]
    dk = [_pdot(dh, dh) for dh in d]
    p = [_pdot(eye - dh, eye + d2) for dh, d2 in zip(d, dk)]
    span = 4
    while span < INV_BLOCK:
        dk = [_pdot(m, m) for m in dk]
        p = [_pdot(ph, eye + m) for ph, m in zip(p, dk)]
        span *= 2
    x = [_pdot(ph, r) for ph, r in zip(p, rhss)]
    nb = c // INV_BLOCK
    if nb > 1:
        nk = [_pdot(ph, eh) for ph, eh in zip(p, e)]
        x = [_pdot(eye - m, xh) for m, xh in zip(nk, x)]
        order = 2
        while order < nb:
            nk = [_pdot(m, m) for m in nk]
            x = [_pdot(eye + m, xh) for m, xh in zip(nk, x)]
            order *= 2
    return x


def _group_sum2(x, bd):
    hi = x.astype(BF16)
    lo = (x - hi.astype(F32)).astype(BF16)
    return (jnp.dot(hi, bd, preferred_element_type=F32) + jnp.dot(lo, bd, preferred_element_type=F32))


def _rwkv_kernel(zs_ref, shift0_ref, wkv0_ref, mu_ref, w0_ref, w2_ref, a0_ref, a2_ref, g2_ref,
                 kk_ref, ka_ref, rk_ref, lng_ref, lnb_ref, bd_ref,
                 o_ref, sout_ref, s_scr, prev_scr, *, chunk):
    c = chunk
    nb = zs_ref.shape[0]
    ci = pl.program_id(1)
    w = RWKV_WIDTH
    pairs = RWKV_HEADS // 2

    @pl.when(ci == 0)
    def _():
        for j in range(nb):
            for p in range(pairs):
                s_scr[j, p] = jnp.concatenate([wkv0_ref[j, 2 * p], wkv0_ref[j, 2 * p + 1]], axis=1)
        prev_scr[...] = shift0_ref[...]

    row = lax.broadcasted_iota(jnp.int32, (c, 2 * c), 0)
    lane = lax.broadcasted_iota(jnp.int32, (c, 2 * c), 1)
    col = jnp.bitwise_and(lane, c - 1)
    tri_incl = row >= col
    tri_strict = row > col
    eye = jnp.where(row == col, 1.0, 0.0)
    blk = jnp.right_shift(row, INV_SHIFT) == jnp.right_shift(col, INV_SHIFT)
    trow = lax.broadcasted_iota(jnp.int32, (c, c), 0)
    tcol = lax.broadcasted_iota(jnp.int32, (c, c), 1)
    tri = jnp.where(trow >= tcol, 1.0, 0.0).astype(BF16)
    mu = mu_ref[...]
    bd = bd_ref[...]
    sls = [slice(p * LANES, (p + 1) * LANES) for p in range(pairs)]

    ar, km, bm, vh, g_last, gates, bonus_rk = [], [], [], [], [], [], []
    for j in range(nb):
        z = zs_ref[j]
        prev_row = prev_scr[j]

        def mixed(lo, hi):
            zp = z[:, lo:hi]
            rolled = pltpu.roll(zp, 1, 0)
            first = lax.broadcasted_iota(jnp.int32, zp.shape, 0) == 0
            prv = jnp.where(first, prev_row[:, lo:hi], rolled)
            return zp + mu[:, lo:hi] * (prv - zp)

        r = mixed(0, w)
        k = mixed(w, 2 * w)
        v = mixed(2 * w, 3 * w)
        tail = mixed(3 * w, SHIFT_PAD)
        prev_scr[j] = z[c - 1:c, :]

        lw = w0_ref[...] + _dot(jnp.tanh(tail), w2_ref[...])
        nlw = -lw
        softplus = jnp.maximum(nlw, 0.0) + jnp.log(1.0 + jnp.exp(-jnp.abs(nlw)))
        logd = -jnp.exp(-softplus - 0.5)
        a = 1.0 / (1.0 + jnp.exp(-(a0_ref[...] + _dot(tail, a2_ref[...]))))
        gates.append(_dot(1.0 / (1.0 + jnp.exp(-tail)), g2_ref[...]))

        kk = k * kk_ref[...]
        kk = kk / jnp.maximum(jnp.sqrt(_group_sum(kk * kk, bd)), L2_EPS)
        k = k * (1.0 + (a - 1.0) * ka_ref[...])
        b = kk * a

        hi = logd.astype(BF16)
        rem = logd - hi.astype(F32)
        mid = rem.astype(BF16)
        lo = (rem - mid.astype(F32)).astype(BF16)
        lg = (jnp.dot(tri, hi, preferred_element_type=F32) + jnp.dot(tri, mid, preferred_element_type=F32)
              + jnp.dot(tri, lo, preferred_element_type=F32))
        g = jnp.exp(lg)
        ginv = jnp.exp(-lg)
        a_m = jnp.exp(lg - logd) * kk
        b_m = b * ginv
        k_m = k * ginv
        p_m = g * r
        bonus_rk.append(_group_sum2(r * k * rk_ref[...], bd) * v)
        ar += [jnp.concatenate([a_m[:, sl], p_m[:, sl]], axis=0).astype(BF16) for sl in sls]
        km += [k_m[:, sl].astype(BF16) for sl in sls]
        bm += [b_m[:, sl].astype(BF16) for sl in sls]
        vh += [v[:, sl].astype(BF16) for sl in sls]
        g_last += [g[c - 1:c, sl] for sl in sls]

    chains = range(nb * pairs)
    s_h = [s_scr[j, p] for j in range(nb) for p in range(pairs)]
    w2c = 2 * c
    gs = [lax.dot_general(ar[i], jnp.concatenate([_bdiag(km[i]), _bdiag(bm[i]), _bdiag(s_h[i])], axis=0), NT,
                          preferred_element_type=F32) for i in chains]
    g_k = [m[:, :w2c] for m in gs]
    g_b = [m[:, w2c:2 * w2c] for m in gs]
    a_s = [m[:, 2 * w2c:] for m in gs]
    rhs = [a_s[i][:c] + _pdot(jnp.where(tri_strict, g_k[i][:c], 0.0), vh[i]) for i in chains]
    u = _solve_unit_lower([jnp.where(tri_strict, g_b[i][:c], 0.0) for i in chains], rhs, eye, blk)
    ys = []
    for i in chains:
        j, p = divmod(i, pairs)
        n_pk = jnp.where(tri_incl, g_k[i][c:], 0.0)
        n_pb = jnp.where(tri_incl, g_b[i][c:], 0.0)
        ys.append(a_s[i][c:] + _dot(jnp.concatenate([n_pk, -n_pb], axis=1),
                                     jnp.concatenate([_bdiag(vh[i]), _bdiag(u[i])], axis=0)))
        full = _dot(jnp.concatenate([vh[i], -u[i].astype(BF16)], axis=0),
                    jnp.concatenate([km[i], bm[i]], axis=0), TN)
        diag = jnp.where(_left_head((HEAD_DIM, LANES)), full[:HEAD_DIM], full[HEAD_DIM:])
        s_scr[j, p] = (s_h[i] + diag) * g_last[i]

    for j in range(nb):
        y = jnp.concatenate(ys[j * pairs:(j + 1) * pairs], axis=1)
        yc = y - _group_sum2(y, bd) * (1.0 / HEAD_DIM)
        var = _group_sum2(yc * yc, bd) * (1.0 / HEAD_DIM)
        yn = yc * lax.rsqrt(var + GN_EPS) * lng_ref[...] + lnb_ref[...]
        o_ref[j] = ((yn + bonus_rk[j]) * gates[j]).astype(o_ref.dtype)

    @pl.when(ci == pl.num_programs(1) - 1)
    def _():
        for j in range(nb):
            for p in range(pairs):
                sp = s_scr[j, p]
                sout_ref[j, 2 * p] = sp[:, :HEAD_DIM]
                sout_ref[j, 2 * p + 1] = sp[:, HEAD_DIM:]


def _rwkv(zs3, shift0, wkv0, p, chunk, nb):
    b, t, _ = zs3.shape
    const = lambda bi, ci: (0, 0)
    full = lambda a: pl.BlockSpec(a.shape, const)
    params = [p["mu"], p["w0"], p["w2"], p["a0"], p["a2"], p["g2"], p["k_k"], p["k_a"], p["r_k"],
              p["lnx_g"], p["lnx_b"], p["bd"]]
    state_spec = pl.BlockSpec((nb, RWKV_HEADS, HEAD_DIM, HEAD_DIM), lambda bi, ci: (bi, 0, 0, 0))
    return pl.pallas_call(
        functools.partial(_rwkv_kernel, chunk=chunk),
        grid=(b // nb, t // chunk),
        in_specs=[pl.BlockSpec((nb, chunk, SHIFT_PAD), lambda bi, ci: (bi, ci, 0)),
                  pl.BlockSpec((nb, 1, SHIFT_PAD), lambda bi, ci: (bi, 0, 0)),
                  state_spec] + [full(a) for a in params],
        out_specs=[pl.BlockSpec((nb, chunk, RWKV_WIDTH), lambda bi, ci: (bi, ci, 0)), state_spec],
        out_shape=[jax.ShapeDtypeStruct((b, t, RWKV_WIDTH), BF16),
                   jax.ShapeDtypeStruct((b, RWKV_HEADS, HEAD_DIM, HEAD_DIM), F32)],
        scratch_shapes=[pltpu.VMEM((nb, RWKV_HEADS // 2, HEAD_DIM, LANES), F32),
                        pltpu.VMEM((nb, 1, SHIFT_PAD), F32)],
        compiler_params=pltpu.CompilerParams(dimension_semantics=("parallel", "arbitrary"),
                                             vmem_limit_bytes=VMEM_LIMIT),
        name="rwkv7",
    )(zs3, shift0, wkv0, *params)


def _attn_kernel(*refs, n_grp, tq, n_main, n_extra, qb, pos0, n_valid, topk):
    if n_extra:
        q_ref, qi_ref, misc_ref, k_ref, v_ref, ki_ref, kx_ref, vx_ref, kix_ref, cnt_ref, o_ref = refs
    else:
        q_ref, qi_ref, misc_ref, k_ref, v_ref, ki_ref, cnt_ref, o_ref = refs
    n_keys = n_main + (LANES if n_extra else 0)
    nl = n_keys // LANES
    rows = n_grp * tq

    def pad_rows(x):
        return jnp.concatenate([x, jnp.zeros((LANES - n_extra, x.shape[1]), x.dtype)], axis=0)

    def dot_keys(lhs, main, extra):
        out = lax.dot_general(lhs, main, NT, preferred_element_type=F32)
        if n_extra:
            out = jnp.concatenate([out, lax.dot_general(lhs, extra, NT, preferred_element_type=F32)], axis=1)
        return out

    def row_limit(r):
        return min(((pos0 + qb * tq + r % tq) // CHUNK + 1) * CHUNK, n_valid)

    pos = pos0 + qb * tq + lax.broadcasted_iota(jnp.int32, (tq, LANES), 0)
    limit = jnp.minimum((jnp.right_shift(pos, CHUNK_SHIFT) + 1) * CHUNK, n_valid)
    if n_grp > 1:
        limit = jnp.concatenate([limit] * n_grp, axis=0)
    key_id = lax.broadcasted_iota(jnp.int32, (rows, n_keys), 1)
    adm = key_id < jnp.tile(limit, (1, nl))

    def topk_bias():
        parts = []
        for g in range(n_grp):
            qi = qi_ref[g]
            ki = ki_ref[g]
            kix = pad_rows(kix_ref[g]) if n_extra else None
            misc = misc_ref[g]
            sc = None
            if tq * IDX_HEADS <= STACK_ROWS:
                d_all = dot_keys(jnp.concatenate([qi[:, h * LANES:(h + 1) * LANES] for h in range(IDX_HEADS)],
                                                 axis=0), ki, kix)
            for h in range(IDX_HEADS):
                if tq * IDX_HEADS <= STACK_ROWS:
                    d = d_all[h * tq:(h + 1) * tq]
                else:
                    d = dot_keys(qi[:, h * LANES:(h + 1) * LANES], ki, kix)
                term = misc[:, IDX_DIM + h:IDX_DIM + h + 1] * jnp.maximum(d, 0.0)
                sc = term if sc is None else sc + term
            parts.append(sc)
        score = parts[0] if n_grp == 1 else jnp.concatenate(parts, axis=0)
        sm = jnp.where(adm, score, -jnp.inf)

        cnt_rhs = cnt_ref[...]
        ones_sq = cnt_ref[:, LANES:]

        def count_ge(sm_c, th, tiles):
            part = None
            for j in range(tiles):
                ind = jnp.where(sm_c[:, j * LANES:(j + 1) * LANES] >= th, 1.0, 0.0)
                part = ind if part is None else part + ind
            return jnp.dot(part.astype(BF16), ones_sq, preferred_element_type=F32)

        row_max = jnp.max(sm, axis=-1, keepdims=True)
        row_min = jnp.min(jnp.where(adm, score, jnp.inf), axis=-1, keepdims=True)
        lo0 = jnp.broadcast_to(row_min, (rows, LANES))
        hi0 = jnp.broadcast_to(2.0 * row_max - row_min + 1.0, (rows, LANES))

        rc = rows // BISECT_CHAINS
        chains = [slice(i * rc, (i + 1) * rc) for i in range(BISECT_CHAINS)]
        tiles = [-(-max(row_limit(r) for r in range(ch.start, ch.stop)) // LANES) for ch in chains]

        def bisect(_, carry):
            out = []
            for ch, nt, (lo, hi) in zip(chains, tiles, carry):
                mid = 0.5 * (lo + hi)
                enough = count_ge(sm[ch], mid, nt) >= topk
                out.append((jnp.where(enough, mid, lo), jnp.where(enough, hi, mid)))
            return tuple(out)

        bounds = lax.fori_loop(0, BISECT_ITERS, bisect, tuple((lo0[ch], hi0[ch]) for ch in chains),
                               unroll=BISECT_UNROLL)
        lo = jnp.concatenate([b[0] for b in bounds], axis=0)
        hi = jnp.concatenate([b[1] for b in bounds], axis=0)

        need = topk - count_ge(sm, hi, nl)
        take_all = limit <= topk
        carry = jnp.zeros((rows, LANES), F32)
        bias = []
        for j in range(nl):
            sm_j = sm[:, j * LANES:(j + 1) * LANES]
            above = sm_j >= hi
            group = jnp.logical_and(sm_j >= lo, jnp.logical_not(above))
            pc = jnp.dot(jnp.where(group, 1.0, 0.0).astype(BF16), cnt_rhs, preferred_element_type=F32)
            fill = carry + pc[:, :LANES] <= need
            carry = carry + pc[:, LANES:]
            picked = jnp.logical_or(above, jnp.logical_and(group, fill))
            sel = jnp.logical_and(sm_j > -jnp.inf, jnp.logical_or(take_all, picked))
            bias.append(jnp.where(sel, 0.0, MASK_BIAS))
        return jnp.concatenate(bias, axis=1)

    if max(row_limit(r) for r in range(rows)) <= topk:
        bias = jnp.where(adm, 0.0, MASK_BIAS)
    else:
        bias = topk_bias()

    for g in range(n_grp):
        q = q_ref[g]
        km, vm = k_ref[g], v_ref[g]
        if km.dtype != BF16:
            km = _spread_heads(km, 0.0).astype(BF16)
            vm = _spread_heads(vm, 1.0).astype(BF16)
        kx, vx = (pad_rows(kx_ref[g]), pad_rows(vx_ref[g])) if n_extra else (None, None)
        bias_g = jnp.concatenate([bias[g * tq:(g + 1) * tq]] * GQA_GROUP, axis=0)
        slabs = []
        for gi in range(N_KV_HEADS):
            gs = slice(gi * LANES, (gi + 1) * LANES)
            qs = jnp.concatenate([q[:, (gi * GQA_GROUP + j) * LANES:(gi * GQA_GROUP + j + 1) * LANES]
                                  for j in range(GQA_GROUP)], axis=0)
            s = dot_keys(qs, km[:, gs], kx[:, gs] if n_extra else None) + bias_g
            m = jnp.max(s, axis=-1, keepdims=True)
            pexp = jnp.exp((s - m).astype(BF16))
            o = jnp.dot(pexp[:, :n_main], vm[:, gs], preferred_element_type=F32)
            if n_extra:
                o = o + jnp.dot(pexp[:, n_main:], vx[:, gs], preferred_element_type=F32)
            o = o / pltpu.roll(o, HEAD_DIM, 1)
            slabs.extend(o[j * tq:(j + 1) * tq] for j in range(GQA_GROUP))
        o_ref[g] = _pack_heads(slabs).astype(o_ref.dtype)


def _attention(q3, qi3, misc3, main, extra, cnt_rhs, *, n_grp, tq, n_main, qb, pos0, n_valid, topk):
    b = q3.shape[0]
    n_extra = extra[0].shape[1] if extra else 0
    qmap = lambda bi: (bi, qb, 0)
    kmap = lambda bi: (bi, 0, 0)
    key_specs = [pl.BlockSpec((n_grp, n_main, a.shape[2]), kmap) for a in main]
    if extra:
        key_specs += [pl.BlockSpec((n_grp, n_extra, a.shape[2]), kmap) for a in extra]
    return pl.pallas_call(
        functools.partial(_attn_kernel, n_grp=n_grp, tq=tq, n_main=n_main, n_extra=n_extra, qb=qb, pos0=pos0,
                          n_valid=n_valid, topk=topk),
        grid=(b // n_grp,),
        in_specs=[pl.BlockSpec((n_grp, tq, N_HEADS * LANES), qmap),
                  pl.BlockSpec((n_grp, tq, IDX_HEADS * LANES), qmap),
                  pl.BlockSpec((n_grp, tq, LANES), qmap)] + key_specs
        + [pl.BlockSpec(cnt_rhs.shape, lambda bi: (0, 0))],
        out_specs=pl.BlockSpec((n_grp, tq, ATT_WIDTH), lambda bi: (bi, 0, 0)),
        out_shape=jax.ShapeDtypeStruct((b, tq, ATT_WIDTH), BF16),
        compiler_params=pltpu.CompilerParams(dimension_semantics=("parallel",), vmem_limit_bytes=VMEM_LIMIT),
        name=f"dsa_attn_{n_main + n_extra}",
    )(q3, qi3, misc3, *main, *(extra or ()), cnt_rhs)


def _outffn_kernel(x_ref, oa_ref, ob_ref, wo_a_ref, wo_b_ref, g_ref, w1_ref, w2_ref, y_ref, *, ff_chunk):
    mix = (jnp.dot(oa_ref[...], wo_a_ref[...], preferred_element_type=F32)
           + jnp.dot(ob_ref[...], wo_b_ref[...], preferred_element_type=F32))
    h = x_ref[...] + mix
    ms = jnp.mean(h * h, axis=-1, keepdims=True)
    hn = (h * lax.rsqrt(ms + NORM_EPS) * g_ref[...]).astype(BF16)
    acc = h
    for j in range(D_FF // ff_chunk):
        act = jnp.dot(hn, w1_ref[:, j * ff_chunk:(j + 1) * ff_chunk], preferred_element_type=F32)
        act = jnp.square(jnp.maximum(act, 0.0)).astype(BF16)
        acc = acc + jnp.dot(act, w2_ref[j * ff_chunk:(j + 1) * ff_chunk, :], preferred_element_type=F32)
    y_ref[...] = acc


def _outffn(x2d, o_att, o_rwkv, p, tm):
    n = x2d.shape[0]
    row = lambda i: (i, 0)
    const = lambda i: (0, 0)
    params = [p["wo_a"], p["wo_b"], p["norm_ffn"], p["w_ff1"], p["w_ff2"]]
    return pl.pallas_call(
        functools.partial(_outffn_kernel, ff_chunk=1024),
        grid=(n // tm,),
        in_specs=[pl.BlockSpec((tm, D_MODEL), row), pl.BlockSpec((tm, ATT_WIDTH), row),
                  pl.BlockSpec((tm, RWKV_WIDTH), row)]
        + [pl.BlockSpec(a.shape, const, pipeline_mode=pl.Buffered(1)) for a in params],
        out_specs=pl.BlockSpec((tm, D_MODEL), row),
        out_shape=jax.ShapeDtypeStruct((n, D_MODEL), F32),
        compiler_params=pltpu.CompilerParams(dimension_semantics=("parallel",), vmem_limit_bytes=VMEM_LIMIT),
        name="outproj_ffn",
    )(x2d, o_att, o_rwkv, *params)


def _block_diag_ones():
    i = np.arange(ATT_WIDTH)
    return jnp.asarray((i[:, None] // HEAD_DIM) == (i[None, :] // HEAD_DIM), BF16)


def _count_rhs():
    i = np.arange(LANES)
    tri = (i[:, None] <= i[None, :]).astype(np.float32)
    return jnp.asarray(np.concatenate([tri, np.ones((LANES, LANES), np.float32)], axis=1), BF16)


def _prep_params(wl):
    (norm_mix, w_in, q_gain, k_gain, kidx_ln_g, kidx_ln_b, mu_shift, w0, w2, a0, a2, g2, k_k, k_a, r_k,
     ln_x_g, ln_x_b, w_out, norm_ffn, w_ff1, w_ff2) = wl
    row = lambda a: a.reshape(1, -1).astype(F32)
    o_ki = QKV_WIDTH
    o_wi = o_ki + IDX_DIM
    o_zs = o_wi + IDX_HEADS
    pad_cols = lambda a, wdt: jnp.pad(a, ((0, 0), (0, wdt - a.shape[1])))
    lora_rows = lambda a, off: jnp.pad(a, ((off, LORA_PAD - off - a.shape[0]), (0, 0))).astype(BF16)
    return {
        "norm_mix": row(norm_mix),
        "wa": w_in[:, :QKV_WIDTH].astype(BF16),
        "wb": pad_cols(w_in[:, o_ki:o_zs], LANES).astype(BF16),
        "wc": pad_cols(w_in[:, o_zs:], SHIFT_PAD).astype(BF16),
        "q_gain": row(jnp.tile(q_gain, N_HEADS)),
        "k_gain": row(jnp.tile(k_gain, N_KV_HEADS)),
        "ln_g": pad_cols(row(kidx_ln_g), LANES),
        "ln_b": pad_cols(row(kidx_ln_b), LANES),
        "bd": _block_diag_ones(),
        "mu": pad_cols(row(mu_shift), SHIFT_PAD),
        "w0": row(w0), "a0": row(a0),
        "w2": lora_rows(w2, 0), "a2": lora_rows(a2, W_LORA), "g2": lora_rows(g2, W_LORA + A_LORA),
        "k_k": row(k_k), "k_a": row(k_a), "r_k": row(r_k), "lnx_g": row(ln_x_g), "lnx_b": row(ln_x_b),
        "wo_a": w_out[:ATT_WIDTH].astype(BF16), "wo_b": w_out[ATT_WIDTH:].astype(BF16),
        "norm_ffn": row(norm_ffn),
        "w_ff1": w_ff1.astype(BF16), "w_ff2": w_ff2.astype(BF16),
    }


def _layer(x, pos0, k_past, v_past, ki_past, wkv0, shift0, p, cnt_rhs, *, chunk, n_grp, tq):
    b, t, _ = x.shape
    n = b * t
    past = k_past.shape[1]
    x2d = x.reshape(n, D_MODEL)
    tm = min(ROW_TILE, n)
    q, k, v, qi, misc, zs, kbf, vbf, kibf = _inproj(x2d, t, pos0, p, tm)

    shift0p = jnp.pad(shift0, ((0, 0), (0, 0), (0, SHIFT_PAD - SHIFT_WIDTH)))
    zs3 = zs.reshape(b, t, SHIFT_PAD)
    o_rwkv, wkv_new = _rwkv(zs3, shift0p, wkv0, p, chunk, min(b, RWKV_BATCH))

    n_valid = past + t
    topk = min(TOPK_MAX, n_valid // 4)
    three = lambda a: a.reshape(b, t, a.shape[-1])
    k_new, v_new, ki_new = three(kbf), three(vbf), three(kibf)
    q3, qi3, misc3 = three(q), three(qi), three(misc)
    if past:
        ki_main = jnp.pad(ki_past.astype(BF16), ((0, 0), (0, 0), (0, LANES - IDX_DIM)))
        main = (k_past.reshape(b, past, KV_WIDTH), v_past.reshape(b, past, KV_WIDTH), ki_main)
        o_att = _attention(q3, qi3, misc3, main, (k_new, v_new, ki_new), cnt_rhs, n_grp=n_grp, tq=tq,
                           n_main=past, qb=0, pos0=pos0, n_valid=n_valid, topk=topk)
    else:
        o_att = jnp.concatenate(
            [_attention(q3, qi3, misc3, (k_new, v_new, ki_new), None, cnt_rhs, n_grp=n_grp, tq=tq,
                        n_main=(j + 1) * tq, qb=j, pos0=pos0, n_valid=n_valid, topk=topk)
             for j in range(t // tq)], axis=1)

    y = _outffn(x2d, o_att.reshape(n, ATT_WIDTH), o_rwkv.reshape(n, RWKV_WIDTH), p, tm)
    return (y.reshape(b, t, D_MODEL), k.reshape(b, t, N_KV_HEADS, HEAD_DIM),
            v.reshape(b, t, N_KV_HEADS, HEAD_DIM), misc3[:, :, :IDX_DIM], wkv_new,
            zs3[:, t - 1:, :SHIFT_WIDTH])


def kernel(x_prompt, x_sample, cache_k, cache_v, cache_kidx, state_wkv, state_shift, norm_mix, w_in, q_gain, k_gain, kidx_ln_g, kidx_ln_b, mu_shift, w0, w2, a0, a2, g2, k_k, k_a, r_k, ln_x_g, ln_x_b, w_out, norm_ffn, w_ff1, w_ff2):
    depth = norm_mix.shape[0]
    bp, tp = x_prompt.shape[0], x_prompt.shape[1]
    bs, ts = x_sample.shape[0], x_sample.shape[1]
    past_len = cache_k.shape[2]
    cnt_rhs = _count_rhs()
    no_kv = jnp.zeros((bp, 0, KV_WIDTH), F32)
    no_ki = jnp.zeros((bp, 0, IDX_DIM), F32)
    wkv_zero = jnp.zeros((bp, RWKV_HEADS, HEAD_DIM, HEAD_DIM), F32)
    shift_zero = jnp.zeros((bp, 1, SHIFT_WIDTH), F32)

    y_p, y_s = x_prompt, x_sample
    outs_p, outs_s = [], []
    for l in range(depth):
        wl = (norm_mix[l], w_in[l], q_gain[l], k_gain[l], kidx_ln_g[l], kidx_ln_b[l], mu_shift[l],
              w0[l], w2[l], a0[l], a2[l], g2[l], k_k[l], k_a[l], r_k[l], ln_x_g[l], ln_x_b[l],
              w_out[l], norm_ffn[l], w_ff1[l], w_ff2[l])
        p = _prep_params(wl)
        y_p, *rest_p = _layer(y_p, 0, no_kv, no_kv, no_ki, wkv_zero, shift_zero, p, cnt_rhs,
                              chunk=64, n_grp=1, tq=256)
        y_s, *rest_s = _layer(y_s, past_len, cache_k[l], cache_v[l], cache_kidx[l], state_wkv[l],
                              state_shift[l], p, cnt_rhs, chunk=ts, n_grp=min(bs, 256 // ts), tq=ts)
        outs_p.append(rest_p)
        outs_s.append(rest_s)
    stack = lambda outs, i: jnp.stack([o[i] for o in outs])
    return (y_p, y_s,
            *(stack(outs_p, i) for i in range(5)),
            *(stack(outs_s, i) for i in range(5)))
```

```python
import functools

import numpy as np
import jax
import jax.numpy as jnp
from jax import lax
from jax.experimental import pallas as pl
from jax.experimental.pallas import tpu as pltpu

F32 = jnp.float32
BF16 = jnp.bfloat16

D_MODEL = 1024
CHUNK = 64
HEAD_DIM = 64
ATT_WIDTH = D_MODEL // 2
N_HEADS = ATT_WIDTH // HEAD_DIM
N_KV_HEADS = N_HEADS // 2
GQA_GROUP = N_HEADS // N_KV_HEADS
KV_WIDTH = N_KV_HEADS * HEAD_DIM
ROT_DIM = HEAD_DIM // 4
ROPE_THETA = 500000.0
IDX_HEADS = 8
IDX_DIM = 64
IDX_WIDTH = IDX_HEADS * IDX_DIM
TOPK_MAX = 256
RWKV_WIDTH = D_MODEL - ATT_WIDTH
RWKV_HEADS = RWKV_WIDTH // HEAD_DIM
W_LORA = 32
A_LORA = 32
G_LORA = 96
LORA_WIDTH = W_LORA + A_LORA + G_LORA
SHIFT_WIDTH = 3 * RWKV_WIDTH + LORA_WIDTH
D_FF = 4 * D_MODEL
NORM_EPS = 1e-6
GN_EPS = 64e-5
L2_EPS = 1e-12

LANES = 128
SHIFT_PAD = 3 * RWKV_WIDTH + 2 * LANES
LORA_PAD = SHIFT_PAD - 3 * RWKV_WIDTH
QKV_WIDTH = ATT_WIDTH + 2 * KV_WIDTH + IDX_WIDTH
INV_BLOCK = 16
INV_SHIFT = 4
CHUNK_SHIFT = 6
MASK_BIAS = -1e30
ROW_TILE = 512
RWKV_BATCH = 4
STACK_ROWS = 512
BISECT_ITERS = 32
BISECT_CHAINS = 4
BISECT_UNROLL = 2
VMEM_LIMIT = 56 * 1024 * 1024

NN = (((1,), (0,)), ((), ()))
NT = (((1,), (1,)), ((), ()))
TN = (((0,), (0,)), ((), ()))


def _dot(a, b, dims=NN):
    return lax.dot_general(a.astype(BF16), b.astype(BF16), dims, preferred_element_type=F32)


def _group_sum(x, bd):
    w = x.shape[-1]
    return _dot(x, bd[:w, :w])


def _rope(x, cos, sa, sb):
    w = x.shape[-1]
    reps = w // LANES
    if reps > 1:
        cos, sa, sb = (jnp.tile(t, (1, reps)) for t in (cos, sa, sb))
    half = ROT_DIM // 2
    return x * cos + pltpu.roll(x, w - half, 1) * sa + pltpu.roll(x, half, 1) * sb


def _spread_heads(x, fill):
    low = lax.broadcasted_iota(jnp.int32, (x.shape[0], LANES), 1) < HEAD_DIM
    cols = []
    for c in range(x.shape[1] // LANES):
        pair = x[:, c * LANES:(c + 1) * LANES]
        cols.append(jnp.where(low, pair, fill))
        cols.append(jnp.where(low, pltpu.roll(pair, HEAD_DIM, 1), fill))
    return jnp.concatenate(cols, axis=1)


def _pack_heads(slabs):
    low = lax.broadcasted_iota(jnp.int32, slabs[0].shape, 1) < HEAD_DIM
    cols = [jnp.where(low, slabs[i], pltpu.roll(slabs[i + 1], HEAD_DIM, 1)) for i in range(0, len(slabs), 2)]
    return jnp.concatenate(cols, axis=1)


def _inproj_kernel(x_ref, g_ref, wa_ref, wb_ref, wc_ref, qg_ref, kg_ref, lng_ref, lnb_ref, bd_ref,
                   cos_ref, sa_ref, sb_ref,
                   q_ref, k_ref, v_ref, qi_ref, misc_ref, zs_ref, kbf_ref, vbf_ref, kibf_ref):
    x = x_ref[...]
    ms = jnp.mean(x * x, axis=-1, keepdims=True)
    xn = (x * lax.rsqrt(ms + NORM_EPS) * g_ref[...]).astype(BF16)
    za = jnp.dot(xn, wa_ref[...], preferred_element_type=F32)
    zb = jnp.dot(xn, wb_ref[...], preferred_element_type=F32)
    zs_ref[...] = jnp.dot(xn, wc_ref[...], preferred_element_type=F32)

    cos, sa, sb = cos_ref[...], sa_ref[...], sb_ref[...]
    bd = bd_ref[...]

    zq = za[:, :ATT_WIDTH]
    qn = zq * lax.rsqrt(_group_sum(zq * zq, bd) * (1.0 / HEAD_DIM) + NORM_EPS) * qg_ref[...]
    q_ref[...] = _spread_heads(_rope(qn, cos, sa, sb) * (HEAD_DIM ** -0.5), 0.0).astype(BF16)

    zk = za[:, ATT_WIDTH:ATT_WIDTH + KV_WIDTH]
    kn = zk * lax.rsqrt(_group_sum(zk * zk, bd) * (1.0 / HEAD_DIM) + NORM_EPS) * kg_ref[...]
    k = _rope(kn, cos, sa, sb)
    k_ref[...] = k
    kbf_ref[...] = _spread_heads(k, 0.0).astype(BF16)

    v = za[:, ATT_WIDTH + KV_WIDTH:ATT_WIDTH + 2 * KV_WIDTH]
    v_ref[...] = v
    vbf_ref[...] = _spread_heads(v, 1.0).astype(BF16)

    zqi = za[:, ATT_WIDTH + 2 * KV_WIDTH:]
    qi_ref[...] = _spread_heads(_rope(zqi, cos, sa, sb) * (IDX_DIM ** -0.5), 0.0).astype(BF16)

    lane = lax.broadcasted_iota(jnp.int32, zb.shape, 1)
    is_ki = lane < IDX_DIM
    mu = jnp.sum(jnp.where(is_ki, zb, 0.0), axis=-1, keepdims=True) * (1.0 / IDX_DIM)
    d = jnp.where(is_ki, zb - mu, 0.0)
    var = jnp.sum(d * d, axis=-1, keepdims=True) * (1.0 / IDX_DIM)
    kin = d * lax.rsqrt(var + NORM_EPS) * lng_ref[...] + lnb_ref[...]
    ki = _rope(kin, cos, sa, sb)
    misc_ref[...] = jnp.where(is_ki, ki, zb * (IDX_HEADS ** -0.5))
    kibf_ref[...] = jnp.where(is_ki, ki, 0.0).astype(BF16)


def _rope_tables(pos):
    half = ROT_DIM // 2
    inv = ROPE_THETA ** (-jnp.arange(0, ROT_DIM, 2, dtype=F32) / ROT_DIM)
    ang = pos.astype(F32)[:, None] * inv[None, :]
    cos, sin = jnp.cos(ang), jnp.sin(ang)
    n = pos.shape[0]
    ones = jnp.ones((n, HEAD_DIM - ROT_DIM), F32)
    zeros_h = jnp.zeros((n, half), F32)
    zeros_r = jnp.zeros((n, HEAD_DIM - ROT_DIM), F32)
    c64 = jnp.concatenate([cos, cos, ones], axis=1)
    sa64 = jnp.concatenate([-sin, zeros_h, zeros_r], axis=1)
    sb64 = jnp.concatenate([zeros_h, sin, zeros_r], axis=1)
    rep = LANES // HEAD_DIM
    return tuple(jnp.tile(t, (1, rep)) for t in (c64, sa64, sb64))


def _inproj(x2d, seq_len, pos0, p, tm):
    n = x2d.shape[0]
    tab_rows = max(seq_len, tm)
    pos = pos0 + (jnp.arange(tab_rows, dtype=jnp.int32) % seq_len)
    cos, sa, sb = _rope_tables(pos)
    nblk_t = tab_rows // tm

    def row(i):
        return (i, 0)

    def const(i):
        return (0, 0)

    def tab(i):
        return (i % nblk_t, 0)

    full = lambda a: pl.BlockSpec(a.shape, const)
    outs = [
        (N_HEADS * LANES, BF16), (KV_WIDTH, F32), (KV_WIDTH, F32), (IDX_HEADS * LANES, BF16), (LANES, F32),
        (SHIFT_PAD, F32), (N_KV_HEADS * LANES, BF16), (N_KV_HEADS * LANES, BF16), (LANES, BF16),
    ]
    params = [p["norm_mix"], p["wa"], p["wb"], p["wc"], p["q_gain"], p["k_gain"], p["ln_g"], p["ln_b"], p["bd"]]
    return pl.pallas_call(
        _inproj_kernel,
        grid=(n // tm,),
        in_specs=[pl.BlockSpec((tm, D_MODEL), row)] + [full(a) for a in params]
        + [pl.BlockSpec((tm, LANES), tab)] * 3,
        out_specs=[pl.BlockSpec((tm, w), row) for w, _ in outs],
        out_shape=[jax.ShapeDtypeStruct((n, w), dt) for w, dt in outs],
        compiler_params=pltpu.CompilerParams(dimension_semantics=("parallel",), vmem_limit_bytes=VMEM_LIMIT),
        name="inproj",
    )(x2d, *params, cos, sa, sb)


def _left_head(shape):
    return lax.broadcasted_iota(jnp.int32, shape, 1) < shape[1] // 2


def _bdiag(x):
    x = x.astype(BF16)
    zero = jnp.zeros_like(x)
    left = _left_head(x.shape)
    return jnp.concatenate([jnp.where(left, x, zero), jnp.where(left, zero, x)], axis=0)


def _pdot(x, y, dims=NN):
    return lax.dot_general(x.astype(BF16), _bdiag(y), dims, preferred_element_type=F32)


def _solve_unit_lower(lows, rhss, eye, blk):
    c = lows[0].shape[0]
    d = [jnp.where(blk, l, 0.0) for l in lows]
    e = [l - dh for l, dh in zip(lows, d)]
    dk = [_pdot(dh, dh) for dh in d]
    p = [_pdot(eye - dh, eye + d2) for dh, d2 in zip(d, dk)]
    span = 4
    while span < INV_BLOCK:
        dk = [_pdot(m, m) for m in dk]
        p = [_pdot(ph, eye + m) for ph, m in zip(p, dk)]
        span *= 2
    x = [_pdot(ph, r) for ph, r in zip(p, rhss)]
    nb = c // INV_BLOCK
    if nb > 1:
        nk = [_pdot(ph, eh) for ph, eh in zip(p, e)]
        x = [_pdot(eye - m, xh) for m, xh in zip(nk, x)]
        order = 2
        while order < nb:
            nk = [_pdot(m, m) for m in nk]
            x = [_pdot(eye + m, xh) for m, xh in zip(nk, x)]
            order *= 2
    return x


def _group_sum2(x, bd):
    hi = x.astype(BF16)
    lo = (x - hi.astype(F32)).astype(BF16)
    return (jnp.dot(hi, bd, preferred_element_type=F32) + jnp.dot(lo, bd, preferred_element_type=F32))


def _rwkv_kernel(zs_ref, shift0_ref, wkv0_ref, mu_ref, w0_ref, w2_ref, a0_ref, a2_ref, g2_ref,
                 kk_ref, ka_ref, rk_ref, lng_ref, lnb_ref, bd_ref,
                 o_ref, sout_ref, s_scr, prev_scr, *, chunk):
    c = chunk
    nb = zs_ref.shape[0]
    ci = pl.program_id(1)
    w = RWKV_WIDTH
    pairs = RWKV_HEADS // 2

    @pl.when(ci == 0)
    def _():
        for j in range(nb):
            for p in range(pairs):
                s_scr[j, p] = jnp.concatenate([wkv0_ref[j, 2 * p], wkv0_ref[j, 2 * p + 1]], axis=1)
        prev_scr[...] = shift0_ref[...]

    row = lax.broadcasted_iota(jnp.int32, (c, 2 * c), 0)
    lane = lax.broadcasted_iota(jnp.int32, (c, 2 * c), 1)
    col = jnp.bitwise_and(lane, c - 1)
    tri_incl = row >= col
    tri_strict = row > col
    eye = jnp.where(row == col, 1.0, 0.0)
    blk = jnp.right_shift(row, INV_SHIFT) == jnp.right_shift(col, INV_SHIFT)
    trow = lax.broadcasted_iota(jnp.int32, (c, c), 0)
    tcol = lax.broadcasted_iota(jnp.int32, (c, c), 1)
    tri = jnp.where(trow >= tcol, 1.0, 0.0).astype(BF16)
    mu = mu_ref[...]
    bd = bd_ref[...]
    sls = [slice(p * LANES, (p + 1) * LANES) for p in range(pairs)]

    ar, km, bm, vh, g_last, gates, bonus_rk = [], [], [], [], [], [], []
    for j in range(nb):
        z = zs_ref[j]
        prev_row = prev_scr[j]

        def mixed(lo, hi):
            zp = z[:, lo:hi]
            rolled = pltpu.roll(zp, 1, 0)
            first = lax.broadcasted_iota(jnp.int32, zp.shape, 0) == 0
            prv = jnp.where(first, prev_row[:, lo:hi], rolled)
            return zp + mu[:, lo:hi] * (prv - zp)

        r = mixed(0, w)
        k = mixed(w, 2 * w)
        v = mixed(2 * w, 3 * w)
        tail = mixed(3 * w, SHIFT_PAD)
        prev_scr[j] = z[c - 1:c, :]

        lw = w0_ref[...] + _dot(jnp.tanh(tail), w2_ref[...])
        nlw = -lw
        softplus = jnp.maximum(nlw, 0.0) + jnp.log(1.0 + jnp.exp(-jnp.abs(nlw)))
        logd = -jnp.exp(-softplus - 0.5)
        a = 1.0 / (1.0 + jnp.exp(-(a0_ref[...] + _dot(tail, a2_ref[...]))))
        gates.append(_dot(1.0 / (1.0 + jnp.exp(-tail)), g2_ref[...]))

        kk = k * kk_ref[...]
        kk = kk / jnp.maximum(jnp.sqrt(_group_sum(kk * kk, bd)), L2_EPS)
        k = k * (1.0 + (a - 1.0) * ka_ref[...])
        b = kk * a

        hi = logd.astype(BF16)
        rem = logd - hi.astype(F32)
        mid = rem.astype(BF16)
        lo = (rem - mid.astype(F32)).astype(BF16)
        lg = (jnp.dot(tri, hi, preferred_element_type=F32) + jnp.dot(tri, mid, preferred_element_type=F32)
              + jnp.dot(tri, lo, preferred_element_type=F32))
        g = jnp.exp(lg)
        ginv = jnp.exp(-lg)
        a_m = jnp.exp(lg - logd) * kk
        b_m = b * ginv
        k_m = k * ginv
        p_m = g * r
        bonus_rk.append(_group_sum2(r * k * rk_ref[...], bd) * v)
        ar += [jnp.concatenate([a_m[:, sl], p_m[:, sl]], axis=0).astype(BF16) for sl in sls]
        km += [k_m[:, sl].astype(BF16) for sl in sls]
        bm += [b_m[:, sl].astype(BF16) for sl in sls]
        vh += [v[:, sl].astype(BF16) for sl in sls]
        g_last += [g[c - 1:c, sl] for sl in sls]

    chains = range(nb * pairs)
    s_h = [s_scr[j, p] for j in range(nb) for p in range(pairs)]
    w2c = 2 * c
    gs = [lax.dot_general(ar[i], jnp.concatenate([_bdiag(km[i]), _bdiag(bm[i]), _bdiag(s_h[i])], axis=0), NT,
                          preferred_element_type=F32) for i in chains]
    g_k = [m[:, :w2c] for m in gs]
    g_b = [m[:, w2c:2 * w2c] for m in gs]
    a_s = [m[:, 2 * w2c:] for m in gs]
    rhs = [a_s[i][:c] + _pdot(jnp.where(tri_strict, g_k[i][:c], 0.0), vh[i]) for i in chains]
    u = _solve_unit_lower([jnp.where(tri_strict, g_b[i][:c], 0.0) for i in chains], rhs, eye, blk)
    ys = []
    for i in chains:
        j, p = divmod(i, pairs)
        n_pk = jnp.where(tri_incl, g_k[i][c:], 0.0)
        n_pb = jnp.where(tri_incl, g_b[i][c:], 0.0)
        ys.append(a_s[i][c:] + _dot(jnp.concatenate([n_pk, -n_pb], axis=1),
                                     jnp.concatenate([_bdiag(vh[i]), _bdiag(u[i])], axis=0)))
        full = _dot(jnp.concatenate([vh[i], -u[i].astype(BF16)], axis=0),
                    jnp.concatenate([km[i], bm[i]], axis=0), TN)
        diag = jnp.where(_left_head((HEAD_DIM, LANES)), full[:HEAD_DIM], full[HEAD_DIM:])
        s_scr[j, p] = (s_h[i] + diag) * g_last[i]

    for j in range(nb):
        y = jnp.concatenate(ys[j * pairs:(j + 1) * pairs], axis=1)
        yc = y - _group_sum2(y, bd) * (1.0 / HEAD_DIM)
        var = _group_sum2(yc * yc, bd) * (1.0 / HEAD_DIM)
        yn = yc * lax.rsqrt(var + GN_EPS) * lng_ref[...] + lnb_ref[...]
        o_ref[j] = ((yn + bonus_rk[j]) * gates[j]).astype(o_ref.dtype)

    @pl.when(ci == pl.num_programs(1) - 1)
    def _():
        for j in range(nb):
            for p in range(pairs):
                sp = s_scr[j, p]
                sout_ref[j, 2 * p] = sp[:, :HEAD_DIM]
                sout_ref[j, 2 * p + 1] = sp[:, HEAD_DIM:]


def _rwkv(zs3, shift0, wkv0, p, chunk, nb):
    b, t, _ = zs3.shape
    const = lambda bi, ci: (0, 0)
    full = lambda a: pl.BlockSpec(a.shape, const)
    params = [p["mu"], p["w0"], p["w2"], p["a0"], p["a2"], p["g2"], p["k_k"], p["k_a"], p["r_k"],
              p["lnx_g"], p["lnx_b"], p["bd"]]
    state_spec = pl.BlockSpec((nb, RWKV_HEADS, HEAD_DIM, HEAD_DIM), lambda bi, ci: (bi, 0, 0, 0))
    return pl.pallas_call(
        functools.partial(_rwkv_kernel, chunk=chunk),
        grid=(b // nb, t // chunk),
        in_specs=[pl.BlockSpec((nb, chunk, SHIFT_PAD), lambda bi, ci: (bi, ci, 0)),
                  pl.BlockSpec((nb, 1, SHIFT_PAD), lambda bi, ci: (bi, 0, 0)),
                  state_spec] + [full(a) for a in params],
        out_specs=[pl.BlockSpec((nb, chunk, RWKV_WIDTH), lambda bi, ci: (bi, ci, 0)), state_spec],
        out_shape=[jax.ShapeDtypeStruct((b, t, RWKV_WIDTH), BF16),
                   jax.ShapeDtypeStruct((b, RWKV_HEADS, HEAD_DIM, HEAD_DIM), F32)],
        scratch_shapes=[pltpu.VMEM((nb, RWKV_HEADS // 2, HEAD_DIM, LANES), F32),
                        pltpu.VMEM((nb, 1, SHIFT_PAD), F32)],
        compiler_params=pltpu.CompilerParams(dimension_semantics=("parallel", "arbitrary"),
                                             vmem_limit_bytes=VMEM_LIMIT),
        name="rwkv7",
    )(zs3, shift0, wkv0, *params)


def _attn_kernel(*refs, n_grp, tq, n_main, n_extra, qb, pos0, n_valid, topk):
    if n_extra:
        q_ref, qi_ref, misc_ref, k_ref, v_ref, ki_ref, kx_ref, vx_ref, kix_ref, cnt_ref, o_ref = refs
    else:
        q_ref, qi_ref, misc_ref, k_ref, v_ref, ki_ref, cnt_ref, o_ref, logit_scr = refs
    n_keys = n_main + (LANES if n_extra else 0)
    nl = n_keys // LANES
    rows = n_grp * tq

    def pad_rows(x):
        return jnp.concatenate([x, jnp.zeros((LANES - n_extra, x.shape[1]), x.dtype)], axis=0)

    def dot_keys(lhs, main, extra):
        out = lax.dot_general(lhs, main, NT, preferred_element_type=F32)
        if n_extra:
            out = jnp.concatenate([out, lax.dot_general(lhs, extra, NT, preferred_element_type=F32)], axis=1)
        return out

    def row_limit(r):
        return min(((pos0 + qb * tq + r % tq) // CHUNK + 1) * CHUNK, n_valid)

    pos = pos0 + qb * tq + lax.broadcasted_iota(jnp.int32, (tq, LANES), 0)
    limit = jnp.minimum((jnp.right_shift(pos, CHUNK_SHIFT) + 1) * CHUNK, n_valid)
    if n_grp > 1:
        limit = jnp.concatenate([limit] * n_grp, axis=0)
    key_id = lax.broadcasted_iota(jnp.int32, (rows, n_keys), 1)
    adm = key_id < jnp.tile(limit, (1, nl))

    def topk_bias():
        parts = []
        for g in range(n_grp):
            qi = qi_ref[g]
            ki = ki_ref[g]
            kix = pad_rows(kix_ref[g]) if n_extra else None
            misc = misc_ref[g]
            sc = None
            if tq * IDX_HEADS <= STACK_ROWS:
                d_all = dot_keys(jnp.concatenate([qi[:, h * LANES:(h + 1) * LANES] for h in range(IDX_HEADS)],
                                                 axis=0), ki, kix)
            for h in range(IDX_HEADS):
                if tq * IDX_HEADS <= STACK_ROWS:
                    d = d_all[h * tq:(h + 1) * tq]
                else:
                    d = dot_keys(qi[:, h * LANES:(h + 1) * LANES], ki, kix)
                term = misc[:, IDX_DIM + h:IDX_DIM + h + 1] * jnp.maximum(d, 0.0)
                sc = term if sc is None else sc + term
            parts.append(sc)
        score = parts[0] if n_grp == 1 else jnp.concatenate(parts, axis=0)
        sm = jnp.where(adm, score, -jnp.inf)

        cnt_rhs = cnt_ref[...]
        ones_sq = cnt_ref[:, LANES:]

        def count_ge(sm_c, th, tiles):
            part = None
            for j in range(tiles):
                ind = jnp.where(sm_c[:, j * LANES:(j + 1) * LANES] >= th, 1.0, 0.0)
                part = ind if part is None else part + ind
            return jnp.dot(part.astype(BF16), ones_sq, preferred_element_type=F32)

        row_max = jnp.max(sm, axis=-1, keepdims=True)
        row_min = jnp.min(jnp.where(adm, score, jnp.inf), axis=-1, keepdims=True)
        lo0 = jnp.broadcast_to(row_min, (rows, LANES))
        hi0 = jnp.broadcast_to(2.0 * row_max - row_min + 1.0, (rows, LANES))

        rc = rows // BISECT_CHAINS
        chains = [slice(i * rc, (i + 1) * rc) for i in range(BISECT_CHAINS)]
        tiles = [-(-max(row_limit(r) for r in range(ch.start, ch.stop)) // LANES) for ch in chains]

        def bisect(_, carry):
            out = []
            for ch, nt, (lo, hi) in zip(chains, tiles, carry):
                mid = 0.5 * (lo + hi)
                enough = count_ge(sm[ch], mid, nt) >= topk
                out.append((jnp.where(enough, mid, lo), jnp.where(enough, hi, mid)))
            return tuple(out)

        bounds = lax.fori_loop(0, BISECT_ITERS, bisect, tuple((lo0[ch], hi0[ch]) for ch in chains),
                               unroll=BISECT_UNROLL)
        lo = jnp.concatenate([b[0] for b in bounds], axis=0)
        hi = jnp.concatenate([b[1] for b in bounds], axis=0)

        need = topk - count_ge(sm, hi, nl)
        take_all = limit <= topk
        carry = jnp.zeros((rows, LANES), F32)
        bias = []
        for j in range(nl):
            sm_j = sm[:, j * LANES:(j + 1) * LANES]
            above = sm_j >= hi
            group = jnp.logical_and(sm_j >= lo, jnp.logical_not(above))
            pc = jnp.dot(jnp.where(group, 1.0, 0.0).astype(BF16), cnt_rhs, preferred_element_type=F32)
            fill = carry + pc[:, :LANES] <= need
            carry = carry + pc[:, LANES:]
            picked = jnp.logical_or(above, jnp.logical_and(group, fill))
            sel = jnp.logical_and(sm_j > -jnp.inf, jnp.logical_or(take_all, picked))
            bias.append(jnp.where(sel, 0.0, MASK_BIAS))
        return jnp.concatenate(bias, axis=1)

    def keys_values(g):
        km, vm = k_ref[g], v_ref[g]
        if km.dtype != BF16:
            km = _spread_heads(km, 0.0).astype(BF16)
            vm = _spread_heads(vm, 1.0).astype(BF16)
        kx, vx = (pad_rows(kx_ref[g]), pad_rows(vx_ref[g])) if n_extra else (None, None)
        return km, vm, kx, vx

    def qk_logits(g, gi, km, kx):
        q = q_ref[g]
        gs = slice(gi * LANES, (gi + 1) * LANES)
        qs = jnp.concatenate([q[:, (gi * GQA_GROUP + j) * LANES:(gi * GQA_GROUP + j + 1) * LANES]
                              for j in range(GQA_GROUP)], axis=0)
        return dot_keys(qs, km[:, gs], kx[:, gs] if n_extra else None)

    early = not n_extra
    if early:
        kv = [keys_values(g) for g in range(n_grp)]
        for g in range(n_grp):
            for gi in range(N_KV_HEADS):
                logit_scr[g * N_KV_HEADS + gi] = qk_logits(g, gi, kv[g][0], kv[g][2])

    if max(row_limit(r) for r in range(rows)) <= topk:
        bias = jnp.where(adm, 0.0, MASK_BIAS)
    else:
        bias = topk_bias()

    for g in range(n_grp):
        km, vm, kx, vx = kv[g] if early else keys_values(g)
        bias_g = jnp.concatenate([bias[g * tq:(g + 1) * tq]] * GQA_GROUP, axis=0)
        slabs = []
        for gi in range(N_KV_HEADS):
            gs = slice(gi * LANES, (gi + 1) * LANES)
            s = (logit_scr[g * N_KV_HEADS + gi] if early else qk_logits(g, gi, km, kx)) + bias_g
            m = jnp.max(s, axis=-1, keepdims=True)
            pexp = jnp.exp((s - m).astype(BF16))
            o = jnp.dot(pexp[:, :n_main], vm[:, gs], preferred_element_type=F32)
            if n_extra:
                o = o + jnp.dot(pexp[:, n_main:], vx[:, gs], preferred_element_type=F32)
            o = o / pltpu.roll(o, HEAD_DIM, 1)
            slabs.extend(o[j * tq:(j + 1) * tq] for j in range(GQA_GROUP))
        o_ref[g] = _pack_heads(slabs).astype(o_ref.dtype)


def _attention(q3, qi3, misc3, main, extra, cnt_rhs, *, n_grp, tq, n_main, qb, pos0, n_valid, topk):
    b = q3.shape[0]
    n_extra = extra[0].shape[1] if extra else 0
    qmap = lambda bi: (bi, qb, 0)
    kmap = lambda bi: (bi, 0, 0)
    key_specs = [pl.BlockSpec((n_grp, n_main, a.shape[2]), kmap) for a in main]
    if extra:
        key_specs += [pl.BlockSpec((n_grp, n_extra, a.shape[2]), kmap) for a in extra]
    return pl.pallas_call(
        functools.partial(_attn_kernel, n_grp=n_grp, tq=tq, n_main=n_main, n_extra=n_extra, qb=qb, pos0=pos0,
                          n_valid=n_valid, topk=topk),
        grid=(b // n_grp,),
        in_specs=[pl.BlockSpec((n_grp, tq, N_HEADS * LANES), qmap),
                  pl.BlockSpec((n_grp, tq, IDX_HEADS * LANES), qmap),
                  pl.BlockSpec((n_grp, tq, LANES), qmap)] + key_specs
        + [pl.BlockSpec(cnt_rhs.shape, lambda bi: (0, 0))],
        out_specs=pl.BlockSpec((n_grp, tq, ATT_WIDTH), lambda bi: (bi, 0, 0)),
        out_shape=jax.ShapeDtypeStruct((b, tq, ATT_WIDTH), BF16),
        scratch_shapes=[] if extra else [pltpu.VMEM((n_grp * N_KV_HEADS, GQA_GROUP * tq, n_main), F32)],
        compiler_params=pltpu.CompilerParams(dimension_semantics=("parallel",), vmem_limit_bytes=VMEM_LIMIT),
        name=f"dsa_attn_{n_main + n_extra}",
    )(q3, qi3, misc3, *main, *(extra or ()), cnt_rhs)


def _outffn_kernel(x_ref, oa_ref, ob_ref, wo_a_ref, wo_b_ref, g_ref, w1_ref, w2_ref, y_ref, *, ff_chunk):
    mix = (jnp.dot(oa_ref[...], wo_a_ref[...], preferred_element_type=F32)
           + jnp.dot(ob_ref[...], wo_b_ref[...], preferred_element_type=F32))
    h = x_ref[...] + mix
    ms = jnp.mean(h * h, axis=-1, keepdims=True)
    hn = (h * lax.rsqrt(ms + NORM_EPS) * g_ref[...]).astype(BF16)
    acc = h
    for j in range(D_FF // ff_chunk):
        act = jnp.dot(hn, w1_ref[:, j * ff_chunk:(j + 1) * ff_chunk], preferred_element_type=F32)
        act = jnp.square(jnp.maximum(act, 0.0)).astype(BF16)
        acc = acc + jnp.dot(act, w2_ref[j * ff_chunk:(j + 1) * ff_chunk, :], preferred_element_type=F32)
    y_ref[...] = acc


def _outffn(x2d, o_att, o_rwkv, p, tm):
    n = x2d.shape[0]
    row = lambda i: (i, 0)
    const = lambda i: (0, 0)
    params = [p["wo_a"], p["wo_b"], p["norm_ffn"], p["w_ff1"], p["w_ff2"]]
    return pl.pallas_call(
        functools.partial(_outffn_kernel, ff_chunk=1024),
        grid=(n // tm,),
        in_specs=[pl.BlockSpec((tm, D_MODEL), row), pl.BlockSpec((tm, ATT_WIDTH), row),
                  pl.BlockSpec((tm, RWKV_WIDTH), row)]
        + [pl.BlockSpec(a.shape, const, pipeline_mode=pl.Buffered(1)) for a in params],
        out_specs=pl.BlockSpec((tm, D_MODEL), row),
        out_shape=jax.ShapeDtypeStruct((n, D_MODEL), F32),
        compiler_params=pltpu.CompilerParams(dimension_semantics=("parallel",), vmem_limit_bytes=VMEM_LIMIT),
        name="outproj_ffn",
    )(x2d, o_att, o_rwkv, *params)


def _block_diag_ones():
    i = np.arange(ATT_WIDTH)
    return jnp.asarray((i[:, None] // HEAD_DIM) == (i[None, :] // HEAD_DIM), BF16)


def _count_rhs():
    i = np.arange(LANES)
    tri = (i[:, None] <= i[None, :]).astype(np.float32)
    return jnp.asarray(np.concatenate([tri, np.ones((LANES, LANES), np.float32)], axis=1), BF16)


def _prep_params(wl):
    (norm_mix, w_in, q_gain, k_gain, kidx_ln_g, kidx_ln_b, mu_shift, w0, w2, a0, a2, g2, k_k, k_a, r_k,
     ln_x_g, ln_x_b, w_out, norm_ffn, w_ff1, w_ff2) = wl
    row = lambda a: a.reshape(1, -1).astype(F32)
    o_ki = QKV_WIDTH
    o_wi = o_ki + IDX_DIM
    o_zs = o_wi + IDX_HEADS
    pad_cols = lambda a, wdt: jnp.pad(a, ((0, 0), (0, wdt - a.shape[1])))
    lora_rows = lambda a, off: jnp.pad(a, ((off, LORA_PAD - off - a.shape[0]), (0, 0))).astype(BF16)
    return {
        "norm_mix": row(norm_mix),
        "wa": w_in[:, :QKV_WIDTH].astype(BF16),
        "wb": pad_cols(w_in[:, o_ki:o_zs], LANES).astype(BF16),
        "wc": pad_cols(w_in[:, o_zs:], SHIFT_PAD).astype(BF16),
        "q_gain": row(jnp.tile(q_gain, N_HEADS)),
        "k_gain": row(jnp.tile(k_gain, N_KV_HEADS)),
        "ln_g": pad_cols(row(kidx_ln_g), LANES),
        "ln_b": pad_cols(row(kidx_ln_b), LANES),
        "bd": _block_diag_ones(),
        "mu": pad_cols(row(mu_shift), SHIFT_PAD),
        "w0": row(w0), "a0": row(a0),
        "w2": lora_rows(w2, 0), "a2": lora_rows(a2, W_LORA), "g2": lora_rows(g2, W_LORA + A_LORA),
        "k_k": row(k_k), "k_a": row(k_a), "r_k": row(r_k), "lnx_g": row(ln_x_g), "lnx_b": row(ln_x_b),
        "wo_a": w_out[:ATT_WIDTH].astype(BF16), "wo_b": w_out[ATT_WIDTH:].astype(BF16),
        "norm_ffn": row(norm_ffn),
        "w_ff1": w_ff1.astype(BF16), "w_ff2": w_ff2.astype(BF16),
    }


def _layer(x, pos0, k_past, v_past, ki_past, wkv0, shift0, p, cnt_rhs, *, chunk, n_grp, tq):
    b, t, _ = x.shape
    n = b * t
    past = k_past.shape[1]
    x2d = x.reshape(n, D_MODEL)
    tm = min(ROW_TILE, n)
    q, k, v, qi, misc, zs, kbf, vbf, kibf = _inproj(x2d, t, pos0, p, tm)

    shift0p = jnp.pad(shift0, ((0, 0), (0, 0), (0, SHIFT_PAD - SHIFT_WIDTH)))
    zs3 = zs.reshape(b, t, SHIFT_PAD)
    o_rwkv, wkv_new = _rwkv(zs3, shift0p, wkv0, p, chunk, min(b, RWKV_BATCH))

    n_valid = past + t
    topk = min(TOPK_MAX, n_valid // 4)
    three = lambda a: a.reshape(b, t, a.shape[-1])
    k_new, v_new, ki_new = three(kbf), three(vbf), three(kibf)
    q3, qi3, misc3 = three(q), three(qi), three(misc)
    if past:
        ki_main = jnp.pad(ki_past.astype(BF16), ((0, 0), (0, 0), (0, LANES - IDX_DIM)))
        main = (k_past.reshape(b, past, KV_WIDTH), v_past.reshape(b, past, KV_WIDTH), ki_main)
        o_att = _attention(q3, qi3, misc3, main, (k_new, v_new, ki_new), cnt_rhs, n_grp=n_grp, tq=tq,
                           n_main=past, qb=0, pos0=pos0, n_valid=n_valid, topk=topk)
    else:
        o_att = jnp.concatenate(
            [_attention(q3, qi3, misc3, (k_new, v_new, ki_new), None, cnt_rhs, n_grp=n_grp, tq=tq,
                        n_main=(j + 1) * tq, qb=j, pos0=pos0, n_valid=n_valid, topk=topk)
             for j in range(t // tq)], axis=1)

    y = _outffn(x2d, o_att.reshape(n, ATT_WIDTH), o_rwkv.reshape(n, RWKV_WIDTH), p, tm)
    return (y.reshape(b, t, D_MODEL), k.reshape(b, t, N_KV_HEADS, HEAD_DIM),
            v.reshape(b, t, N_KV_HEADS, HEAD_DIM), misc3[:, :, :IDX_DIM], wkv_new,
            zs3[:, t - 1:, :SHIFT_WIDTH])


def kernel(x_prompt, x_sample, cache_k, cache_v, cache_kidx, state_wkv, state_shift, norm_mix, w_in, q_gain, k_gain, kidx_ln_g, kidx_ln_b, mu_shift, w0, w2, a0, a2, g2, k_k, k_a, r_k, ln_x_g, ln_x_b, w_out, norm_ffn, w_ff1, w_ff2):
    depth = norm_mix.shape[0]
    bp, tp = x_prompt.shape[0], x_prompt.shape[1]
    bs, ts = x_sample.shape[0], x_sample.shape[1]
    past_len = cache_k.shape[2]
    cnt_rhs = _count_rhs()
    no_kv = jnp.zeros((bp, 0, KV_WIDTH), F32)
    no_ki = jnp.zeros((bp, 0, IDX_DIM), F32)
    wkv_zero = jnp.zeros((bp, RWKV_HEADS, HEAD_DIM, HEAD_DIM), F32)
    shift_zero = jnp.zeros((bp, 1, SHIFT_WIDTH), F32)

    y_p, y_s = x_prompt, x_sample
    outs_p, outs_s = [], []
    for l in range(depth):
        wl = (norm_mix[l], w_in[l], q_gain[l], k_gain[l], kidx_ln_g[l], kidx_ln_b[l], mu_shift[l],
              w0[l], w2[l], a0[l], a2[l], g2[l], k_k[l], k_a[l], r_k[l], ln_x_g[l], ln_x_b[l],
              w_out[l], norm_ffn[l], w_ff1[l], w_ff2[l])
        p = _prep_params(wl)
        y_p, *rest_p = _layer(y_p, 0, no_kv, no_kv, no_ki, wkv_zero, shift_zero, p, cnt_rhs,
                              chunk=64, n_grp=1, tq=256)
        y_s, *rest_s = _layer(y_s, past_len, cache_k[l], cache_v[l], cache_kidx[l], state_wkv[l],
                              state_shift[l], p, cnt_rhs, chunk=ts, n_grp=min(bs, 256 // ts), tq=ts)
        outs_p.append(rest_p)
        outs_s.append(rest_s)
    stack = lambda outs, i: jnp.stack([o[i] for o in outs])
    return (y_p, y_s,
            *(stack(outs_p, i) for i in range(5)),
            *(stack(outs_s, i) for i in range(5)))
```

```python
import functools

import numpy as np
import jax
import jax.numpy as jnp
from jax import lax
from jax.experimental import pallas as pl
from jax.experimental.pallas import tpu as pltpu

F32 = jnp.float32
BF16 = jnp.bfloat16

D_MODEL = 1024
CHUNK = 64
HEAD_DIM = 64
ATT_WIDTH = D_MODEL // 2
N_HEADS = ATT_WIDTH // HEAD_DIM
N_KV_HEADS = N_HEADS // 2
GQA_GROUP = N_HEADS // N_KV_HEADS
KV_WIDTH = N_KV_HEADS * HEAD_DIM
ROT_DIM = HEAD_DIM // 4
ROPE_THETA = 500000.0
IDX_HEADS = 8
IDX_DIM = 64
IDX_WIDTH = IDX_HEADS * IDX_DIM
TOPK_MAX = 256
RWKV_WIDTH = D_MODEL - ATT_WIDTH
RWKV_HEADS = RWKV_WIDTH // HEAD_DIM
W_LORA = 32
A_LORA = 32
G_LORA = 96
LORA_WIDTH = W_LORA + A_LORA + G_LORA
SHIFT_WIDTH = 3 * RWKV_WIDTH + LORA_WIDTH
D_FF = 4 * D_MODEL
NORM_EPS = 1e-6
GN_EPS = 64e-5
L2_EPS = 1e-12

LANES = 128
SHIFT_PAD = 3 * RWKV_WIDTH + 2 * LANES
LORA_PAD = SHIFT_PAD - 3 * RWKV_WIDTH
QKV_WIDTH = ATT_WIDTH + 2 * KV_WIDTH + IDX_WIDTH
INV_BLOCK = 16
INV_SHIFT = 4
CHUNK_SHIFT = 6
MASK_BIAS = -1e30
ROW_TILE = 512
RWKV_BATCH = 4
STACK_ROWS = 512
BISECT_ITERS = 32
BISECT_CHAINS = 4
BISECT_UNROLL = 16
VMEM_LIMIT = 56 * 1024 * 1024

NN = (((1,), (0,)), ((), ()))
NT = (((1,), (1,)), ((), ()))
TN = (((0,), (0,)), ((), ()))


def _dot(a, b, dims=NN):
    return lax.dot_general(a.astype(BF16), b.astype(BF16), dims, preferred_element_type=F32)


def _group_sum(x, bd):
    w = x.shape[-1]
    return _dot(x, bd[:w, :w])


def _rope(x, cos, sa, sb):
    w = x.shape[-1]
    reps = w // LANES
    if reps > 1:
        cos, sa, sb = (jnp.tile(t, (1, reps)) for t in (cos, sa, sb))
    half = ROT_DIM // 2
    return x * cos + pltpu.roll(x, w - half, 1) * sa + pltpu.roll(x, half, 1) * sb


def _spread_heads(x, fill):
    low = lax.broadcasted_iota(jnp.int32, (x.shape[0], LANES), 1) < HEAD_DIM
    cols = []
    for c in range(x.shape[1] // LANES):
        pair = x[:, c * LANES:(c + 1) * LANES]
        cols.append(jnp.where(low, pair, fill))
        cols.append(jnp.where(low, pltpu.roll(pair, HEAD_DIM, 1), fill))
    return jnp.concatenate(cols, axis=1)


def _pack_heads(slabs):
    low = lax.broadcasted_iota(jnp.int32, slabs[0].shape, 1) < HEAD_DIM
    cols = [jnp.where(low, slabs[i], pltpu.roll(slabs[i + 1], HEAD_DIM, 1)) for i in range(0, len(slabs), 2)]
    return jnp.concatenate(cols, axis=1)


def _inproj_kernel(x_ref, g_ref, wa_ref, wb_ref, wc_ref, qg_ref, kg_ref, lng_ref, lnb_ref, bd_ref,
                   cos_ref, sa_ref, sb_ref,
                   q_ref, k_ref, v_ref, qi_ref, misc_ref, zs_ref, kbf_ref, vbf_ref, kibf_ref):
    x = x_ref[...]
    ms = jnp.mean(x * x, axis=-1, keepdims=True)
    xn = (x * lax.rsqrt(ms + NORM_EPS) * g_ref[...]).astype(BF16)
    za = jnp.dot(xn, wa_ref[...], preferred_element_type=F32)
    zb = jnp.dot(xn, wb_ref[...], preferred_element_type=F32)
    zs_ref[...] = jnp.dot(xn, wc_ref[...], preferred_element_type=F32)

    cos, sa, sb = cos_ref[...], sa_ref[...], sb_ref[...]
    bd = bd_ref[...]

    zq = za[:, :ATT_WIDTH]
    qn = zq * lax.rsqrt(_group_sum(zq * zq, bd) * (1.0 / HEAD_DIM) + NORM_EPS) * qg_ref[...]
    q_ref[...] = _spread_heads(_rope(qn, cos, sa, sb) * (HEAD_DIM ** -0.5), 0.0).astype(BF16)

    zk = za[:, ATT_WIDTH:ATT_WIDTH + KV_WIDTH]
    kn = zk * lax.rsqrt(_group_sum(zk * zk, bd) * (1.0 / HEAD_DIM) + NORM_EPS) * kg_ref[...]
    k = _rope(kn, cos, sa, sb)
    k_ref[...] = k
    kbf_ref[...] = _spread_heads(k, 0.0).astype(BF16)

    v = za[:, ATT_WIDTH + KV_WIDTH:ATT_WIDTH + 2 * KV_WIDTH]
    v_ref[...] = v
    vbf_ref[...] = _spread_heads(v, 1.0).astype(BF16)

    zqi = za[:, ATT_WIDTH + 2 * KV_WIDTH:]
    qi_ref[...] = _spread_heads(_rope(zqi, cos, sa, sb) * (IDX_DIM ** -0.5), 0.0).astype(BF16)

    lane = lax.broadcasted_iota(jnp.int32, zb.shape, 1)
    is_ki = lane < IDX_DIM
    mu = jnp.sum(jnp.where(is_ki, zb, 0.0), axis=-1, keepdims=True) * (1.0 / IDX_DIM)
    d = jnp.where(is_ki, zb - mu, 0.0)
    var = jnp.sum(d * d, axis=-1, keepdims=True) * (1.0 / IDX_DIM)
    kin = d * lax.rsqrt(var + NORM_EPS) * lng_ref[...] + lnb_ref[...]
    ki = _rope(kin, cos, sa, sb)
    misc_ref[...] = jnp.where(is_ki, ki, zb * (IDX_HEADS ** -0.5))
    kibf_ref[...] = jnp.where(is_ki, ki, 0.0).astype(BF16)


def _rope_tables(pos):
    half = ROT_DIM // 2
    inv = ROPE_THETA ** (-jnp.arange(0, ROT_DIM, 2, dtype=F32) / ROT_DIM)
    ang = pos.astype(F32)[:, None] * inv[None, :]
    cos, sin = jnp.cos(ang), jnp.sin(ang)
    n = pos.shape[0]
    ones = jnp.ones((n, HEAD_DIM - ROT_DIM), F32)
    zeros_h = jnp.zeros((n, half), F32)
    zeros_r = jnp.zeros((n, HEAD_DIM - ROT_DIM), F32)
    c64 = jnp.concatenate([cos, cos, ones], axis=1)
    sa64 = jnp.concatenate([-sin, zeros_h, zeros_r], axis=1)
    sb64 = jnp.concatenate([zeros_h, sin, zeros_r], axis=1)
    rep = LANES // HEAD_DIM
    return tuple(jnp.tile(t, (1, rep)) for t in (c64, sa64, sb64))


def _inproj(x2d, seq_len, pos0, p, tm):
    n = x2d.shape[0]
    tab_rows = max(seq_len, tm)
    pos = pos0 + (jnp.arange(tab_rows, dtype=jnp.int32) % seq_len)
    cos, sa, sb = _rope_tables(pos)
    nblk_t = tab_rows // tm

    def row(i):
        return (i, 0)

    def const(i):
        return (0, 0)

    def tab(i):
        return (i % nblk_t, 0)

    full = lambda a: pl.BlockSpec(a.shape, const)
    outs = [
        (N_HEADS * LANES, BF16), (KV_WIDTH, F32), (KV_WIDTH, F32), (IDX_HEADS * LANES, BF16), (LANES, F32),
        (SHIFT_PAD, F32), (N_KV_HEADS * LANES, BF16), (N_KV_HEADS * LANES, BF16), (LANES, BF16),
    ]
    params = [p["norm_mix"], p["wa"], p["wb"], p["wc"], p["q_gain"], p["k_gain"], p["ln_g"], p["ln_b"], p["bd"]]
    return pl.pallas_call(
        _inproj_kernel,
        grid=(n // tm,),
        in_specs=[pl.BlockSpec((tm, D_MODEL), row)] + [full(a) for a in params]
        + [pl.BlockSpec((tm, LANES), tab)] * 3,
        out_specs=[pl.BlockSpec((tm, w), row) for w, _ in outs],
        out_shape=[jax.ShapeDtypeStruct((n, w), dt) for w, dt in outs],
        compiler_params=pltpu.CompilerParams(dimension_semantics=("parallel",), vmem_limit_bytes=VMEM_LIMIT),
        name="inproj",
    )(x2d, *params, cos, sa, sb)


def _left_head(shape):
    return lax.broadcasted_iota(jnp.int32, shape, 1) < shape[1] // 2


def _bdiag(x):
    x = x.astype(BF16)
    zero = jnp.zeros_like(x)
    left = _left_head(x.shape)
    return jnp.concatenate([jnp.where(left, x, zero), jnp.where(left, zero, x)], axis=0)


def _pdot(x, y, dims=NN):
    return lax.dot_general(x.astype(BF16), _bdiag(y), dims, preferred_element_type=F32)


def _solve_unit_lower(lows, rhss, eye, blk):
    c = lows[0].shape[0]
    d = [jnp.where(blk, l, 0.0) for l in lows]
    e = [l - dh for l, dh in zip(lows, d)]
    dk = [_pdot(dh, dh) for dh in d]
    p = [_pdot(eye - dh, eye + d2) for dh, d2 in zip(d, dk)]
    span = 4
    while span < INV_BLOCK:
        dk = [_pdot(m, m) for m in dk]
        p = [_pdot(ph, eye + m) for ph, m in zip(p, dk)]
        span *= 2
    x = [_pdot(ph, r) for ph, r in zip(p, rhss)]
    nb = c // INV_BLOCK
    if nb > 1:
        nk = [_pdot(ph, eh) for ph, eh in zip(p, e)]
        x = [_pdot(eye - m, xh) for m, xh in zip(nk, x)]
        order = 2
        while order < nb:
            nk = [_pdot(m, m) for m in nk]
            x = [_pdot(eye + m, xh) for m, xh in zip(nk, x)]
            order *= 2
    return x


def _group_sum2(x, bd):
    hi = x.astype(BF16)
    lo = (x - hi.astype(F32)).astype(BF16)
    return (jnp.dot(hi, bd, preferred_element_type=F32) + jnp.dot(lo, bd, preferred_element_type=F32))


def _rwkv_kernel(zs_ref, shift0_ref, wkv0_ref, mu_ref, w0_ref, w2_ref, a0_ref, a2_ref, g2_ref,
                 kk_ref, ka_ref, rk_ref, lng_ref, lnb_ref, bd_ref,
                 o_ref, sout_ref, s_scr, prev_scr, *, chunk):
    c = chunk
    nb = zs_ref.shape[0]
    ci = pl.program_id(1)
    w = RWKV_WIDTH
    pairs = RWKV_HEADS // 2

    @pl.when(ci == 0)
    def _():
        for j in range(nb):
            for p in range(pairs):
                s_scr[j, p] = jnp.concatenate([wkv0_ref[j, 2 * p], wkv0_ref[j, 2 * p + 1]], axis=1)
        prev_scr[...] = shift0_ref[...]

    row = lax.broadcasted_iota(jnp.int32, (c, 2 * c), 0)
    lane = lax.broadcasted_iota(jnp.int32, (c, 2 * c), 1)
    col = jnp.bitwise_and(lane, c - 1)
    tri_incl = row >= col
    tri_strict = row > col
    eye = jnp.where(row == col, 1.0, 0.0)
    blk = jnp.right_shift(row, INV_SHIFT) == jnp.right_shift(col, INV_SHIFT)
    trow = lax.broadcasted_iota(jnp.int32, (c, c), 0)
    tcol = lax.broadcasted_iota(jnp.int32, (c, c), 1)
    tri = jnp.where(trow >= tcol, 1.0, 0.0).astype(BF16)
    mu = mu_ref[...]
    bd = bd_ref[...]
    sls = [slice(p * LANES, (p + 1) * LANES) for p in range(pairs)]

    ar, km, bm, vh, g_last, gates, bonus_rk = [], [], [], [], [], [], []
    for j in range(nb):
        z = zs_ref[j]
        prev_row = prev_scr[j]

        def mixed(lo, hi):
            zp = z[:, lo:hi]
            rolled = pltpu.roll(zp, 1, 0)
            first = lax.broadcasted_iota(jnp.int32, zp.shape, 0) == 0
            prv = jnp.where(first, prev_row[:, lo:hi], rolled)
            return zp + mu[:, lo:hi] * (prv - zp)

        r = mixed(0, w)
        k = mixed(w, 2 * w)
        v = mixed(2 * w, 3 * w)
        tail = mixed(3 * w, SHIFT_PAD)
        prev_scr[j] = z[c - 1:c, :]

        lw = w0_ref[...] + _dot(jnp.tanh(tail), w2_ref[...])
        nlw = -lw
        softplus = jnp.maximum(nlw, 0.0) + jnp.log(1.0 + jnp.exp(-jnp.abs(nlw)))
        logd = -jnp.exp(-softplus - 0.5)
        a = 1.0 / (1.0 + jnp.exp(-(a0_ref[...] + _dot(tail, a2_ref[...]))))
        gates.append(_dot(1.0 / (1.0 + jnp.exp(-tail)), g2_ref[...]))

        kk = k * kk_ref[...]
        kk = kk / jnp.maximum(jnp.sqrt(_group_sum(kk * kk, bd)), L2_EPS)
        k = k * (1.0 + (a - 1.0) * ka_ref[...])
        b = kk * a

        hi = logd.astype(BF16)
        rem = logd - hi.astype(F32)
        mid = rem.astype(BF16)
        lo = (rem - mid.astype(F32)).astype(BF16)
        lg = (jnp.dot(tri, hi, preferred_element_type=F32) + jnp.dot(tri, mid, preferred_element_type=F32)
              + jnp.dot(tri, lo, preferred_element_type=F32))
        g = jnp.exp(lg)
        ginv = jnp.exp(-lg)
        a_m = jnp.exp(lg - logd) * kk
        b_m = b * ginv
        k_m = k * ginv
        p_m = g * r
        bonus_rk.append(_group_sum2(r * k * rk_ref[...], bd) * v)
        ar += [jnp.concatenate([a_m[:, sl], p_m[:, sl]], axis=0).astype(BF16) for sl in sls]
        km += [k_m[:, sl].astype(BF16) for sl in sls]
        bm += [b_m[:, sl].astype(BF16) for sl in sls]
        vh += [v[:, sl].astype(BF16) for sl in sls]
        g_last += [g[c - 1:c, sl] for sl in sls]

    chains = range(nb * pairs)
    s_h = [s_scr[j, p] for j in range(nb) for p in range(pairs)]
    w2c = 2 * c
    gs = [lax.dot_general(ar[i], jnp.concatenate([_bdiag(km[i]), _bdiag(bm[i]), _bdiag(s_h[i])], axis=0), NT,
                          preferred_element_type=F32) for i in chains]
    g_k = [m[:, :w2c] for m in gs]
    g_b = [m[:, w2c:2 * w2c] for m in gs]
    a_s = [m[:, 2 * w2c:] for m in gs]
    rhs = [a_s[i][:c] + _pdot(jnp.where(tri_strict, g_k[i][:c], 0.0), vh[i]) for i in chains]
    u = _solve_unit_lower([jnp.where(tri_strict, g_b[i][:c], 0.0) for i in chains], rhs, eye, blk)
    ys = []
    for i in chains:
        j, p = divmod(i, pairs)
        n_pk = jnp.where(tri_incl, g_k[i][c:], 0.0)
        n_pb = jnp.where(tri_incl, g_b[i][c:], 0.0)
        ys.append(a_s[i][c:] + _dot(jnp.concatenate([n_pk, -n_pb], axis=1),
                                     jnp.concatenate([_bdiag(vh[i]), _bdiag(u[i])], axis=0)))
        full = _dot(jnp.concatenate([vh[i], -u[i].astype(BF16)], axis=0),
                    jnp.concatenate([km[i], bm[i]], axis=0), TN)
        diag = jnp.where(_left_head((HEAD_DIM, LANES)), full[:HEAD_DIM], full[HEAD_DIM:])
        s_scr[j, p] = (s_h[i] + diag) * g_last[i]

    for j in range(nb):
        y = jnp.concatenate(ys[j * pairs:(j + 1) * pairs], axis=1)
        yc = y - _group_sum2(y, bd) * (1.0 / HEAD_DIM)
        var = _group_sum2(yc * yc, bd) * (1.0 / HEAD_DIM)
        yn = yc * lax.rsqrt(var + GN_EPS) * lng_ref[...] + lnb_ref[...]
        o_ref[j] = ((yn + bonus_rk[j]) * gates[j]).astype(o_ref.dtype)

    @pl.when(ci == pl.num_programs(1) - 1)
    def _():
        for j in range(nb):
            for p in range(pairs):
                sp = s_scr[j, p]
                sout_ref[j, 2 * p] = sp[:, :HEAD_DIM]
                sout_ref[j, 2 * p + 1] = sp[:, HEAD_DIM:]


def _rwkv(zs3, shift0, wkv0, p, chunk, nb):
    b, t, _ = zs3.shape
    const = lambda bi, ci: (0, 0)
    full = lambda a: pl.BlockSpec(a.shape, const)
    params = [p["mu"], p["w0"], p["w2"], p["a0"], p["a2"], p["g2"], p["k_k"], p["k_a"], p["r_k"],
              p["lnx_g"], p["lnx_b"], p["bd"]]
    state_spec = pl.BlockSpec((nb, RWKV_HEADS, HEAD_DIM, HEAD_DIM), lambda bi, ci: (bi, 0, 0, 0))
    return pl.pallas_call(
        functools.partial(_rwkv_kernel, chunk=chunk),
        grid=(b // nb, t // chunk),
        in_specs=[pl.BlockSpec((nb, chunk, SHIFT_PAD), lambda bi, ci: (bi, ci, 0)),
                  pl.BlockSpec((nb, 1, SHIFT_PAD), lambda bi, ci: (bi, 0, 0)),
                  state_spec] + [full(a) for a in params],
        out_specs=[pl.BlockSpec((nb, chunk, RWKV_WIDTH), lambda bi, ci: (bi, ci, 0)), state_spec],
        out_shape=[jax.ShapeDtypeStruct((b, t, RWKV_WIDTH), BF16),
                   jax.ShapeDtypeStruct((b, RWKV_HEADS, HEAD_DIM, HEAD_DIM), F32)],
        scratch_shapes=[pltpu.VMEM((nb, RWKV_HEADS // 2, HEAD_DIM, LANES), F32),
                        pltpu.VMEM((nb, 1, SHIFT_PAD), F32)],
        compiler_params=pltpu.CompilerParams(dimension_semantics=("parallel", "arbitrary"),
                                             vmem_limit_bytes=VMEM_LIMIT),
        name="rwkv7",
    )(zs3, shift0, wkv0, *params)


def _attn_kernel(*refs, n_grp, tq, n_main, n_extra, qb, pos0, n_valid, topk):
    if n_extra:
        q_ref, qi_ref, misc_ref, k_ref, v_ref, ki_ref, kx_ref, vx_ref, kix_ref, cnt_ref, o_ref = refs
    else:
        q_ref, qi_ref, misc_ref, k_ref, v_ref, ki_ref, cnt_ref, o_ref, logit_scr = refs
    n_keys = n_main + (LANES if n_extra else 0)
    nl = n_keys // LANES
    rows = n_grp * tq

    def pad_rows(x):
        return jnp.concatenate([x, jnp.zeros((LANES - n_extra, x.shape[1]), x.dtype)], axis=0)

    def dot_keys(lhs, main, extra):
        out = lax.dot_general(lhs, main, NT, preferred_element_type=F32)
        if n_extra:
            out = jnp.concatenate([out, lax.dot_general(lhs, extra, NT, preferred_element_type=F32)], axis=1)
        return out

    def row_limit(r):
        return min(((pos0 + qb * tq + r % tq) // CHUNK + 1) * CHUNK, n_valid)

    pos = pos0 + qb * tq + lax.broadcasted_iota(jnp.int32, (tq, LANES), 0)
    limit = jnp.minimum((jnp.right_shift(pos, CHUNK_SHIFT) + 1) * CHUNK, n_valid)
    if n_grp > 1:
        limit = jnp.concatenate([limit] * n_grp, axis=0)
    key_id = lax.broadcasted_iota(jnp.int32, (rows, n_keys), 1)
    adm = key_id < jnp.tile(limit, (1, nl))

    def topk_bias():
        parts = []
        for g in range(n_grp):
            qi = qi_ref[g]
            ki = ki_ref[g]
            kix = pad_rows(kix_ref[g]) if n_extra else None
            misc = misc_ref[g]
            sc = None
            if tq * IDX_HEADS <= STACK_ROWS:
                d_all = dot_keys(jnp.concatenate([qi[:, h * LANES:(h + 1) * LANES] for h in range(IDX_HEADS)],
                                                 axis=0), ki, kix)
            for h in range(IDX_HEADS):
                if tq * IDX_HEADS <= STACK_ROWS:
                    d = d_all[h * tq:(h + 1) * tq]
                else:
                    d = dot_keys(qi[:, h * LANES:(h + 1) * LANES], ki, kix)
                term = misc[:, IDX_DIM + h:IDX_DIM + h + 1] * jnp.maximum(d, 0.0)
                sc = term if sc is None else sc + term
            parts.append(sc)
        score = parts[0] if n_grp == 1 else jnp.concatenate(parts, axis=0)
        sm = jnp.where(adm, score, -jnp.inf)

        cnt_rhs = cnt_ref[...]
        ones_sq = cnt_ref[:, LANES:]

        def count_ge(sm_c, th, tiles):
            part = None
            for j in range(tiles):
                ind = jnp.where(sm_c[:, j * LANES:(j + 1) * LANES] >= th, 1.0, 0.0)
                part = ind if part is None else part + ind
            return jnp.dot(part.astype(BF16), ones_sq, preferred_element_type=F32)

        row_max = jnp.max(sm, axis=-1, keepdims=True)
        row_min = jnp.min(jnp.where(adm, score, jnp.inf), axis=-1, keepdims=True)
        lo0 = jnp.broadcast_to(row_min, (rows, LANES))
        hi0 = jnp.broadcast_to(2.0 * row_max - row_min + 1.0, (rows, LANES))

        rc = rows // BISECT_CHAINS
        chains = [slice(i * rc, (i + 1) * rc) for i in range(BISECT_CHAINS)]
        tiles = [-(-max(row_limit(r) for r in range(ch.start, ch.stop)) // LANES) for ch in chains]

        def bisect(_, carry):
            out = []
            for ch, nt, (lo, hi) in zip(chains, tiles, carry):
                mid = 0.5 * (lo + hi)
                enough = count_ge(sm[ch], mid, nt) >= topk
                out.append((jnp.where(enough, mid, lo), jnp.where(enough, hi, mid)))
            return tuple(out)

        bounds = lax.fori_loop(0, BISECT_ITERS, bisect, tuple((lo0[ch], hi0[ch]) for ch in chains),
                               unroll=BISECT_UNROLL)
        lo = jnp.concatenate([b[0] for b in bounds], axis=0)
        hi = jnp.concatenate([b[1] for b in bounds], axis=0)

        need = topk - count_ge(sm, hi, nl)
        take_all = limit <= topk
        carry = jnp.zeros((rows, LANES), F32)
        bias = []
        for j in range(nl):
            sm_j = sm[:, j * LANES:(j + 1) * LANES]
            above = sm_j >= hi
            at_least_lo = sm_j >= lo
            group = jnp.where(above, 0.0, jnp.where(at_least_lo, 1.0, 0.0))
            pc = jnp.dot(group.astype(BF16), cnt_rhs, preferred_element_type=F32)
            fill = carry + pc[:, :LANES] <= need
            carry = carry + pc[:, LANES:]
            b_group = jnp.where(at_least_lo, jnp.where(fill, 0.0, MASK_BIAS), MASK_BIAS)
            b_pick = jnp.where(take_all, 0.0, jnp.where(above, 0.0, b_group))
            bias.append(jnp.where(sm_j > -jnp.inf, b_pick, MASK_BIAS))
        return jnp.concatenate(bias, axis=1)

    def keys_values(g):
        km, vm = k_ref[g], v_ref[g]
        if km.dtype != BF16:
            km = _spread_heads(km, 0.0).astype(BF16)
            vm = _spread_heads(vm, 1.0).astype(BF16)
        kx, vx = (pad_rows(kx_ref[g]), pad_rows(vx_ref[g])) if n_extra else (None, None)
        return km, vm, kx, vx

    def qk_logits(g, gi, km, kx):
        q = q_ref[g]
        gs = slice(gi * LANES, (gi + 1) * LANES)
        qs = jnp.concatenate([q[:, (gi * GQA_GROUP + j) * LANES:(gi * GQA_GROUP + j + 1) * LANES]
                              for j in range(GQA_GROUP)], axis=0)
        return dot_keys(qs, km[:, gs], kx[:, gs] if n_extra else None)

    early = not n_extra
    if early:
        kv = [keys_values(g) for g in range(n_grp)]
        for g in range(n_grp):
            for gi in range(N_KV_HEADS):
                logit_scr[g * N_KV_HEADS + gi] = qk_logits(g, gi, kv[g][0], kv[g][2])

    if max(row_limit(r) for r in range(rows)) <= topk:
        bias = jnp.where(adm, 0.0, MASK_BIAS)
    else:
        bias = topk_bias()

    for g in range(n_grp):
        km, vm, kx, vx = kv[g] if early else keys_values(g)
        bias_g = jnp.concatenate([bias[g * tq:(g + 1) * tq]] * GQA_GROUP, axis=0)
        slabs = []
        for gi in range(N_KV_HEADS):
            gs = slice(gi * LANES, (gi + 1) * LANES)
            s = (logit_scr[g * N_KV_HEADS + gi] if early else qk_logits(g, gi, km, kx)) + bias_g
            m = jnp.max(s, axis=-1, keepdims=True)
            pexp = jnp.exp((s - m).astype(BF16))
            o = jnp.dot(pexp[:, :n_main], vm[:, gs], preferred_element_type=F32)
            if n_extra:
                o = o + jnp.dot(pexp[:, n_main:], vx[:, gs], preferred_element_type=F32)
            o = o / pltpu.roll(o, HEAD_DIM, 1)
            slabs.extend(o[j * tq:(j + 1) * tq] for j in range(GQA_GROUP))
        o_ref[g] = _pack_heads(slabs).astype(o_ref.dtype)


def _attention(q3, qi3, misc3, main, extra, cnt_rhs, *, n_grp, tq, n_main, qb, pos0, n_valid, topk):
    b = q3.shape[0]
    n_extra = extra[0].shape[1] if extra else 0
    qmap = lambda bi: (bi, qb, 0)
    kmap = lambda bi: (bi, 0, 0)
    key_specs = [pl.BlockSpec((n_grp, n_main, a.shape[2]), kmap) for a in main]
    if extra:
        key_specs += [pl.BlockSpec((n_grp, n_extra, a.shape[2]), kmap) for a in extra]
    return pl.pallas_call(
        functools.partial(_attn_kernel, n_grp=n_grp, tq=tq, n_main=n_main, n_extra=n_extra, qb=qb, pos0=pos0,
                          n_valid=n_valid, topk=topk),
        grid=(b // n_grp,),
        in_specs=[pl.BlockSpec((n_grp, tq, N_HEADS * LANES), qmap),
                  pl.BlockSpec((n_grp, tq, IDX_HEADS * LANES), qmap),
                  pl.BlockSpec((n_grp, tq, LANES), qmap)] + key_specs
        + [pl.BlockSpec(cnt_rhs.shape, lambda bi: (0, 0))],
        out_specs=pl.BlockSpec((n_grp, tq, ATT_WIDTH), lambda bi: (bi, 0, 0)),
        out_shape=jax.ShapeDtypeStruct((b, tq, ATT_WIDTH), BF16),
        scratch_shapes=[] if extra else [pltpu.VMEM((n_grp * N_KV_HEADS, GQA_GROUP * tq, n_main), F32)],
        compiler_params=pltpu.CompilerParams(dimension_semantics=("parallel",), vmem_limit_bytes=VMEM_LIMIT),
        name=f"dsa_attn_{n_main + n_extra}",
    )(q3, qi3, misc3, *main, *(extra or ()), cnt_rhs)


def _outffn_kernel(x_ref, oa_ref, ob_ref, wo_a_ref, wo_b_ref, g_ref, w1_ref, w2_ref, y_ref, *, ff_chunk):
    mix = (jnp.dot(oa_ref[...], wo_a_ref[...], preferred_element_type=F32)
           + jnp.dot(ob_ref[...], wo_b_ref[...], preferred_element_type=F32))
    h = x_ref[...] + mix
    ms = jnp.mean(h * h, axis=-1, keepdims=True)
    hn = (h * lax.rsqrt(ms + NORM_EPS) * g_ref[...]).astype(BF16)
    acc = h
    for j in range(D_FF // ff_chunk):
        act = jnp.dot(hn, w1_ref[:, j * ff_chunk:(j + 1) * ff_chunk], preferred_element_type=F32)
        act = jnp.square(jnp.maximum(act, 0.0)).astype(BF16)
        acc = acc + jnp.dot(act, w2_ref[j * ff_chunk:(j + 1) * ff_chunk, :], preferred_element_type=F32)
    y_ref[...] = acc


def _outffn(x2d, o_att, o_rwkv, p, tm):
    n = x2d.shape[0]
    row = lambda i: (i, 0)
    const = lambda i: (0, 0)
    params = [p["wo_a"], p["wo_b"], p["norm_ffn"], p["w_ff1"], p["w_ff2"]]
    return pl.pallas_call(
        functools.partial(_outffn_kernel, ff_chunk=1024),
        grid=(n // tm,),
        in_specs=[pl.BlockSpec((tm, D_MODEL), row), pl.BlockSpec((tm, ATT_WIDTH), row),
                  pl.BlockSpec((tm, RWKV_WIDTH), row)]
        + [pl.BlockSpec(a.shape, const, pipeline_mode=pl.Buffered(1)) for a in params],
        out_specs=pl.BlockSpec((tm, D_MODEL), row),
        out_shape=jax.ShapeDtypeStruct((n, D_MODEL), F32),
        compiler_params=pltpu.CompilerParams(dimension_semantics=("parallel",), vmem_limit_bytes=VMEM_LIMIT),
        name="outproj_ffn",
    )(x2d, o_att, o_rwkv, *params)


def _block_diag_ones():
    i = np.arange(ATT_WIDTH)
    return jnp.asarray((i[:, None] // HEAD_DIM) == (i[None, :] // HEAD_DIM), BF16)


def _count_rhs():
    i = np.arange(LANES)
    tri = (i[:, None] <= i[None, :]).astype(np.float32)
    return jnp.asarray(np.concatenate([tri, np.ones((LANES, LANES), np.float32)], axis=1), BF16)


def _prep_params(wl):
    (norm_mix, w_in, q_gain, k_gain, kidx_ln_g, kidx_ln_b, mu_shift, w0, w2, a0, a2, g2, k_k, k_a, r_k,
     ln_x_g, ln_x_b, w_out, norm_ffn, w_ff1, w_ff2) = wl
    row = lambda a: a.reshape(1, -1).astype(F32)
    o_ki = QKV_WIDTH
    o_wi = o_ki + IDX_DIM
    o_zs = o_wi + IDX_HEADS
    pad_cols = lambda a, wdt: jnp.pad(a, ((0, 0), (0, wdt - a.shape[1])))
    lora_rows = lambda a, off: jnp.pad(a, ((off, LORA_PAD - off - a.shape[0]), (0, 0))).astype(BF16)
    return {
        "norm_mix": row(norm_mix),
        "wa": w_in[:, :QKV_WIDTH].astype(BF16),
        "wb": pad_cols(w_in[:, o_ki:o_zs], LANES).astype(BF16),
        "wc": pad_cols(w_in[:, o_zs:], SHIFT_PAD).astype(BF16),
        "q_gain": row(jnp.tile(q_gain, N_HEADS)),
        "k_gain": row(jnp.tile(k_gain, N_KV_HEADS)),
        "ln_g": pad_cols(row(kidx_ln_g), LANES),
        "ln_b": pad_cols(row(kidx_ln_b), LANES),
        "bd": _block_diag_ones(),
        "mu": pad_cols(row(mu_shift), SHIFT_PAD),
        "w0": row(w0), "a0": row(a0),
        "w2": lora_rows(w2, 0), "a2": lora_rows(a2, W_LORA), "g2": lora_rows(g2, W_LORA + A_LORA),
        "k_k": row(k_k), "k_a": row(k_a), "r_k": row(r_k), "lnx_g": row(ln_x_g), "lnx_b": row(ln_x_b),
        "wo_a": w_out[:ATT_WIDTH].astype(BF16), "wo_b": w_out[ATT_WIDTH:].astype(BF16),
        "norm_ffn": row(norm_ffn),
        "w_ff1": w_ff1.astype(BF16), "w_ff2": w_ff2.astype(BF16),
    }


def _layer(x, pos0, k_past, v_past, ki_past, wkv0, shift0, p, cnt_rhs, *, chunk, n_grp, tq):
    b, t, _ = x.shape
    n = b * t
    past = k_past.shape[1]
    x2d = x.reshape(n, D_MODEL)
    tm = min(ROW_TILE, n)
    q, k, v, qi, misc, zs, kbf, vbf, kibf = _inproj(x2d, t, pos0, p, tm)

    shift0p = jnp.pad(shift0, ((0, 0), (0, 0), (0, SHIFT_PAD - SHIFT_WIDTH)))
    zs3 = zs.reshape(b, t, SHIFT_PAD)
    o_rwkv, wkv_new = _rwkv(zs3, shift0p, wkv0, p, chunk, min(b, RWKV_BATCH))

    n_valid = past + t
    topk = min(TOPK_MAX, n_valid // 4)
    three = lambda a: a.reshape(b, t, a.shape[-1])
    k_new, v_new, ki_new = three(kbf), three(vbf), three(kibf)
    q3, qi3, misc3 = three(q), three(qi), three(misc)
    if past:
        ki_main = jnp.pad(ki_past.astype(BF16), ((0, 0), (0, 0), (0, LANES - IDX_DIM)))
        main = (k_past.reshape(b, past, KV_WIDTH), v_past.reshape(b, past, KV_WIDTH), ki_main)
        o_att = _attention(q3, qi3, misc3, main, (k_new, v_new, ki_new), cnt_rhs, n_grp=n_grp, tq=tq,
                           n_main=past, qb=0, pos0=pos0, n_valid=n_valid, topk=topk)
    else:
        o_att = jnp.concatenate(
            [_attention(q3, qi3, misc3, (k_new, v_new, ki_new), None, cnt_rhs, n_grp=n_grp, tq=tq,
                        n_main=(j + 1) * tq, qb=j, pos0=pos0, n_valid=n_valid, topk=topk)
             for j in range(t // tq)], axis=1)

    y = _outffn(x2d, o_att.reshape(n, ATT_WIDTH), o_rwkv.reshape(n, RWKV_WIDTH), p, tm)
    return (y.reshape(b, t, D_MODEL), k.reshape(b, t, N_KV_HEADS, HEAD_DIM),
            v.reshape(b, t, N_KV_HEADS, HEAD_DIM), misc3[:, :, :IDX_DIM], wkv_new,
            zs3[:, t - 1:, :SHIFT_WIDTH])


def kernel(x_prompt, x_sample, cache_k, cache_v, cache_kidx, state_wkv, state_shift, norm_mix, w_in, q_gain, k_gain, kidx_ln_g, kidx_ln_b, mu_shift, w0, w2, a0, a2, g2, k_k, k_a, r_k, ln_x_g, ln_x_b, w_out, norm_ffn, w_ff1, w_ff2):
    depth = norm_mix.shape[0]
    bp, tp = x_prompt.shape[0], x_prompt.shape[1]
    bs, ts = x_sample.shape[0], x_sample.shape[1]
    past_len = cache_k.shape[2]
    cnt_rhs = _count_rhs()
    no_kv = jnp.zeros((bp, 0, KV_WIDTH), F32)
    no_ki = jnp.zeros((bp, 0, IDX_DIM), F32)
    wkv_zero = jnp.zeros((bp, RWKV_HEADS, HEAD_DIM, HEAD_DIM), F32)
    shift_zero = jnp.zeros((bp, 1, SHIFT_WIDTH), F32)

    y_p, y_s = x_prompt, x_sample
    outs_p, outs_s = [], []
    for l in range(depth):
        wl = (norm_mix[l], w_in[l], q_gain[l], k_gain[l], kidx_ln_g[l], kidx_ln_b[l], mu_shift[l],
              w0[l], w2[l], a0[l], a2[l], g2[l], k_k[l], k_a[l], r_k[l], ln_x_g[l], ln_x_b[l],
              w_out[l], norm_ffn[l], w_ff1[l], w_ff2[l])
        p = _prep_params(wl)
        y_p, *rest_p = _layer(y_p, 0, no_kv, no_kv, no_ki, wkv_zero, shift_zero, p, cnt_rhs,
                              chunk=64, n_grp=1, tq=256)
        y_s, *rest_s = _layer(y_s, past_len, cache_k[l], cache_v[l], cache_kidx[l], state_wkv[l],
                              state_shift[l], p, cnt_rhs, chunk=ts, n_grp=min(bs, 256 // ts), tq=ts)
        outs_p.append(rest_p)
        outs_s.append(rest_s)
    stack = lambda outs, i: jnp.stack([o[i] for o in outs])
    return (y_p, y_s,
            *(stack(outs_p, i) for i in range(5)),
            *(stack(outs_s, i) for i in range(5)))
```

```python
import functools

import numpy as np
import jax
import jax.numpy as jnp
from jax import lax
from jax.experimental import pallas as pl
from jax.experimental.pallas import tpu as pltpu

F32 = jnp.float32
BF16 = jnp.bfloat16

D_MODEL = 1024
CHUNK = 64
HEAD_DIM = 64
ATT_WIDTH = D_MODEL // 2
N_HEADS = ATT_WIDTH // HEAD_DIM
N_KV_HEADS = N_HEADS // 2
GQA_GROUP = N_HEADS // N_KV_HEADS
KV_WIDTH = N_KV_HEADS * HEAD_DIM
ROT_DIM = HEAD_DIM // 4
ROPE_THETA = 500000.0
IDX_HEADS = 8
IDX_DIM = 64
IDX_WIDTH = IDX_HEADS * IDX_DIM
TOPK_MAX = 256
RWKV_WIDTH = D_MODEL - ATT_WIDTH
RWKV_HEADS = RWKV_WIDTH // HEAD_DIM
W_LORA = 32
A_LORA = 32
G_LORA = 96
LORA_WIDTH = W_LORA + A_LORA + G_LORA
SHIFT_WIDTH = 3 * RWKV_WIDTH + LORA_WIDTH
D_FF = 4 * D_MODEL
NORM_EPS = 1e-6
GN_EPS = 64e-5
L2_EPS = 1e-12

LANES = 128
SHIFT_PAD = 3 * RWKV_WIDTH + 2 * LANES
LORA_PAD = SHIFT_PAD - 3 * RWKV_WIDTH
QKV_WIDTH = ATT_WIDTH + 2 * KV_WIDTH + IDX_WIDTH
INV_BLOCK = 16
INV_SHIFT = 4
CHUNK_SHIFT = 6
MASK_BIAS = -1e30
ROW_TILE = 512
RWKV_BATCH = 8
STACK_ROWS = 512
BISECT_ITERS = 32
BISECT_CHAINS = 4
BISECT_UNROLL = 16
VMEM_LIMIT = 56 * 1024 * 1024

NN = (((1,), (0,)), ((), ()))
NT = (((1,), (1,)), ((), ()))
TN = (((0,), (0,)), ((), ()))


def _dot(a, b, dims=NN):
    return lax.dot_general(a.astype(BF16), b.astype(BF16), dims, preferred_element_type=F32)


def _group_sum(x, bd):
    w = x.shape[-1]
    return _dot(x, bd[:w, :w])


def _rope(x, cos, sa, sb):
    w = x.shape[-1]
    reps = w // LANES
    if reps > 1:
        cos, sa, sb = (jnp.tile(t, (1, reps)) for t in (cos, sa, sb))
    half = ROT_DIM // 2
    return x * cos + pltpu.roll(x, w - half, 1) * sa + pltpu.roll(x, half, 1) * sb


def _spread_heads(x, fill):
    low = lax.broadcasted_iota(jnp.int32, (x.shape[0], LANES), 1) < HEAD_DIM
    cols = []
    for c in range(x.shape[1] // LANES):
        pair = x[:, c * LANES:(c + 1) * LANES]
        cols.append(jnp.where(low, pair, fill))
        cols.append(jnp.where(low, pltpu.roll(pair, HEAD_DIM, 1), fill))
    return jnp.concatenate(cols, axis=1)


def _pack_heads(slabs):
    low = lax.broadcasted_iota(jnp.int32, slabs[0].shape, 1) < HEAD_DIM
    cols = [jnp.where(low, slabs[i], pltpu.roll(slabs[i + 1], HEAD_DIM, 1)) for i in range(0, len(slabs), 2)]
    return jnp.concatenate(cols, axis=1)


def _inproj_kernel(x_ref, g_ref, wa_ref, wb_ref, wc_ref, qg_ref, kg_ref, lng_ref, lnb_ref, bd_ref,
                   cos_ref, sa_ref, sb_ref,
                   q_ref, k_ref, v_ref, qi_ref, misc_ref, zs_ref, kbf_ref, vbf_ref, kibf_ref):
    x = x_ref[...]
    ms = jnp.mean(x * x, axis=-1, keepdims=True)
    xn = (x * lax.rsqrt(ms + NORM_EPS) * g_ref[...]).astype(BF16)
    za = jnp.dot(xn, wa_ref[...], preferred_element_type=F32)
    zb = jnp.dot(xn, wb_ref[...], preferred_element_type=F32)
    zs_ref[...] = jnp.dot(xn, wc_ref[...], preferred_element_type=F32)

    cos, sa, sb = cos_ref[...], sa_ref[...], sb_ref[...]
    bd = bd_ref[...]

    zq = za[:, :ATT_WIDTH]
    qn = zq * lax.rsqrt(_group_sum(zq * zq, bd) * (1.0 / HEAD_DIM) + NORM_EPS) * qg_ref[...]
    q_ref[...] = _spread_heads(_rope(qn, cos, sa, sb) * (HEAD_DIM ** -0.5), 0.0).astype(BF16)

    zk = za[:, ATT_WIDTH:ATT_WIDTH + KV_WIDTH]
    kn = zk * lax.rsqrt(_group_sum(zk * zk, bd) * (1.0 / HEAD_DIM) + NORM_EPS) * kg_ref[...]
    k = _rope(kn, cos, sa, sb)
    k_ref[...] = k
    kbf_ref[...] = _spread_heads(k, 0.0).astype(BF16)

    v = za[:, ATT_WIDTH + KV_WIDTH:ATT_WIDTH + 2 * KV_WIDTH]
    v_ref[...] = v
    vbf_ref[...] = _spread_heads(v, 1.0).astype(BF16)

    zqi = za[:, ATT_WIDTH + 2 * KV_WIDTH:]
    qi_ref[...] = _spread_heads(_rope(zqi, cos, sa, sb) * (IDX_DIM ** -0.5), 0.0).astype(BF16)

    lane = lax.broadcasted_iota(jnp.int32, zb.shape, 1)
    is_ki = lane < IDX_DIM
    mu = jnp.sum(jnp.where(is_ki, zb, 0.0), axis=-1, keepdims=True) * (1.0 / IDX_DIM)
    d = jnp.where(is_ki, zb - mu, 0.0)
    var = jnp.sum(d * d, axis=-1, keepdims=True) * (1.0 / IDX_DIM)
    kin = d * lax.rsqrt(var + NORM_EPS) * lng_ref[...] + lnb_ref[...]
    ki = _rope(kin, cos, sa, sb)
    misc_ref[...] = jnp.where(is_ki, ki, zb * (IDX_HEADS ** -0.5))
    kibf_ref[...] = jnp.where(is_ki, ki, 0.0).astype(BF16)


def _rope_tables(pos):
    half = ROT_DIM // 2
    inv = ROPE_THETA ** (-jnp.arange(0, ROT_DIM, 2, dtype=F32) / ROT_DIM)
    ang = pos.astype(F32)[:, None] * inv[None, :]
    cos, sin = jnp.cos(ang), jnp.sin(ang)
    n = pos.shape[0]
    ones = jnp.ones((n, HEAD_DIM - ROT_DIM), F32)
    zeros_h = jnp.zeros((n, half), F32)
    zeros_r = jnp.zeros((n, HEAD_DIM - ROT_DIM), F32)
    c64 = jnp.concatenate([cos, cos, ones], axis=1)
    sa64 = jnp.concatenate([-sin, zeros_h, zeros_r], axis=1)
    sb64 = jnp.concatenate([zeros_h, sin, zeros_r], axis=1)
    rep = LANES // HEAD_DIM
    return tuple(jnp.tile(t, (1, rep)) for t in (c64, sa64, sb64))


def _inproj(x2d, seq_len, pos0, p, tm):
    n = x2d.shape[0]
    tab_rows = max(seq_len, tm)
    pos = pos0 + (jnp.arange(tab_rows, dtype=jnp.int32) % seq_len)
    cos, sa, sb = _rope_tables(pos)
    nblk_t = tab_rows // tm

    def row(i):
        return (i, 0)

    def const(i):
        return (0, 0)

    def tab(i):
        return (i % nblk_t, 0)

    full = lambda a: pl.BlockSpec(a.shape, const)
    outs = [
        (N_HEADS * LANES, BF16), (KV_WIDTH, F32), (KV_WIDTH, F32), (IDX_HEADS * LANES, BF16), (LANES, F32),
        (SHIFT_PAD, F32), (N_KV_HEADS * LANES, BF16), (N_KV_HEADS * LANES, BF16), (LANES, BF16),
    ]
    params = [p["norm_mix"], p["wa"], p["wb"], p["wc"], p["q_gain"], p["k_gain"], p["ln_g"], p["ln_b"], p["bd"]]
    return pl.pallas_call(
        _inproj_kernel,
        grid=(n // tm,),
        in_specs=[pl.BlockSpec((tm, D_MODEL), row)] + [full(a) for a in params]
        + [pl.BlockSpec((tm, LANES), tab)] * 3,
        out_specs=[pl.BlockSpec((tm, w), row) for w, _ in outs],
        out_shape=[jax.ShapeDtypeStruct((n, w), dt) for w, dt in outs],
        compiler_params=pltpu.CompilerParams(dimension_semantics=("parallel",), vmem_limit_bytes=VMEM_LIMIT),
        name="inproj",
    )(x2d, *params, cos, sa, sb)


def _left_head(shape):
    return lax.broadcasted_iota(jnp.int32, shape, 1) < shape[1] // 2


def _bdiag(x):
    x = x.astype(BF16)
    zero = jnp.zeros_like(x)
    left = _left_head(x.shape)
    return jnp.concatenate([jnp.where(left, x, zero), jnp.where(left, zero, x)], axis=0)


def _pdot(x, y, dims=NN):
    return lax.dot_general(x.astype(BF16), _bdiag(y), dims, preferred_element_type=F32)


def _solve_unit_lower(lows, rhss, eye, blk):
    c = lows[0].shape[0]
    d = [jnp.where(blk, l, 0.0) for l in lows]
    e = [l - dh for l, dh in zip(lows, d)]
    dk = [_pdot(dh, dh) for dh in d]
    p = [_pdot(eye - dh, eye + d2) for dh, d2 in zip(d, dk)]
    span = 4
    while span < INV_BLOCK:
        dk = [_pdot(m, m) for m in dk]
        p = [_pdot(ph, eye + m) for ph, m in zip(p, dk)]
        span *= 2
    x = [_pdot(ph, r) for ph, r in zip(p, rhss)]
    nb = c // INV_BLOCK
    if nb > 1:
        nk = [_pdot(ph, eh) for ph, eh in zip(p, e)]
        x = [_pdot(eye - m, xh) for m, xh in zip(nk, x)]
        order = 2
        while order < nb:
            nk = [_pdot(m, m) for m in nk]
            x = [_pdot(eye + m, xh) for m, xh in zip(nk, x)]
            order *= 2
    return x


def _group_sum2(x, bd):
    hi = x.astype(BF16)
    lo = (x - hi.astype(F32)).astype(BF16)
    return (jnp.dot(hi, bd, preferred_element_type=F32) + jnp.dot(lo, bd, preferred_element_type=F32))


def _rwkv_kernel(zs_ref, shift0_ref, wkv0_ref, mu_ref, w0_ref, w2_ref, a0_ref, a2_ref, g2_ref,
                 kk_ref, ka_ref, rk_ref, lng_ref, lnb_ref, bd_ref,
                 o_ref, sout_ref, s_scr, prev_scr, *, chunk):
    c = chunk
    nb = zs_ref.shape[0]
    ci = pl.program_id(1)
    w = RWKV_WIDTH
    pairs = RWKV_HEADS // 2

    @pl.when(ci == 0)
    def _():
        for j in range(nb):
            for p in range(pairs):
                s_scr[j, p] = jnp.concatenate([wkv0_ref[j, 2 * p], wkv0_ref[j, 2 * p + 1]], axis=1)
        prev_scr[...] = shift0_ref[...]

    row = lax.broadcasted_iota(jnp.int32, (c, 2 * c), 0)
    lane = lax.broadcasted_iota(jnp.int32, (c, 2 * c), 1)
    col = jnp.bitwise_and(lane, c - 1)
    tri_incl = row >= col
    tri_strict = row > col
    eye = jnp.where(row == col, 1.0, 0.0)
    blk = jnp.right_shift(row, INV_SHIFT) == jnp.right_shift(col, INV_SHIFT)
    trow = lax.broadcasted_iota(jnp.int32, (c, c), 0)
    tcol = lax.broadcasted_iota(jnp.int32, (c, c), 1)
    tri = jnp.where(trow >= tcol, 1.0, 0.0).astype(BF16)
    mu = mu_ref[...]
    bd = bd_ref[...]
    sls = [slice(p * LANES, (p + 1) * LANES) for p in range(pairs)]

    rs, ks, vs, tails = [], [], [], []
    for j in range(nb):
        z = zs_ref[j]
        prev_row = prev_scr[j]

        def mixed(lo, hi):
            zp = z[:, lo:hi]
            rolled = pltpu.roll(zp, 1, 0)
            first = lax.broadcasted_iota(jnp.int32, zp.shape, 0) == 0
            prv = jnp.where(first, prev_row[:, lo:hi], rolled)
            return zp + mu[:, lo:hi] * (prv - zp)

        rs.append(mixed(0, w))
        ks.append(mixed(w, 2 * w))
        vs.append(mixed(2 * w, 3 * w))
        tails.append(mixed(3 * w, SHIFT_PAD))
        prev_scr[j] = z[c - 1:c, :]
    stack = lambda xs: xs[0] if nb == 1 else jnp.concatenate(xs, axis=0)
    r, k, v, tail = stack(rs), stack(ks), stack(vs), stack(tails)

    lw = w0_ref[...] + _dot(jnp.tanh(tail), w2_ref[...])
    nlw = -lw
    softplus = jnp.maximum(nlw, 0.0) + jnp.log(1.0 + jnp.exp(-jnp.abs(nlw)))
    logd = -jnp.exp(-softplus - 0.5)
    a = 1.0 / (1.0 + jnp.exp(-(a0_ref[...] + _dot(tail, a2_ref[...]))))
    gate = _dot(1.0 / (1.0 + jnp.exp(-tail)), g2_ref[...])

    kk = k * kk_ref[...]
    kk = kk / jnp.maximum(jnp.sqrt(_group_sum(kk * kk, bd)), L2_EPS)
    k = k * (1.0 + (a - 1.0) * ka_ref[...])
    b = kk * a

    hi = logd.astype(BF16)
    rem = logd - hi.astype(F32)
    mid = rem.astype(BF16)
    lo = (rem - mid.astype(F32)).astype(BF16)
    lg = stack([jnp.dot(tri, hi[rj], preferred_element_type=F32) + jnp.dot(tri, mid[rj], preferred_element_type=F32)
                + jnp.dot(tri, lo[rj], preferred_element_type=F32)
                for rj in (slice(j * c, (j + 1) * c) for j in range(nb))])
    g = jnp.exp(lg)
    ginv = jnp.exp(-lg)
    a_m = jnp.exp(lg - logd) * kk
    b_m = b * ginv
    k_m = k * ginv
    p_m = g * r
    bonus = _group_sum2(r * k * rk_ref[...], bd) * v

    ar, km, bm, vh, g_last = [], [], [], [], []
    for j in range(nb):
        rj = slice(j * c, (j + 1) * c)
        ar += [jnp.concatenate([a_m[rj, sl], p_m[rj, sl]], axis=0).astype(BF16) for sl in sls]
        km += [k_m[rj, sl].astype(BF16) for sl in sls]
        bm += [b_m[rj, sl].astype(BF16) for sl in sls]
        vh += [v[rj, sl].astype(BF16) for sl in sls]
        g_last += [g[(j + 1) * c - 1:(j + 1) * c, sl] for sl in sls]

    chains = range(nb * pairs)
    s_h = [s_scr[j, p] for j in range(nb) for p in range(pairs)]
    w2c = 2 * c
    gs = [lax.dot_general(ar[i], jnp.concatenate([_bdiag(km[i]), _bdiag(bm[i]), _bdiag(s_h[i])], axis=0), NT,
                          preferred_element_type=F32) for i in chains]
    g_k = [m[:, :w2c] for m in gs]
    g_b = [m[:, w2c:2 * w2c] for m in gs]
    a_s = [m[:, 2 * w2c:] for m in gs]
    rhs = [a_s[i][:c] + _pdot(jnp.where(tri_strict, g_k[i][:c], 0.0), vh[i]) for i in chains]
    u = _solve_unit_lower([jnp.where(tri_strict, g_b[i][:c], 0.0) for i in chains], rhs, eye, blk)
    ys = []
    for i in chains:
        j, p = divmod(i, pairs)
        n_pk = jnp.where(tri_incl, g_k[i][c:], 0.0)
        n_pb = jnp.where(tri_incl, g_b[i][c:], 0.0)
        ys.append(a_s[i][c:] + _dot(jnp.concatenate([n_pk, -n_pb], axis=1),
                                     jnp.concatenate([_bdiag(vh[i]), _bdiag(u[i])], axis=0)))
        full = _dot(jnp.concatenate([vh[i], -u[i].astype(BF16)], axis=0),
                    jnp.concatenate([km[i], bm[i]], axis=0), TN)
        diag = jnp.where(_left_head((HEAD_DIM, LANES)), full[:HEAD_DIM], full[HEAD_DIM:])
        s_scr[j, p] = (s_h[i] + diag) * g_last[i]

    y = stack([jnp.concatenate(ys[j * pairs:(j + 1) * pairs], axis=1) for j in range(nb)])
    yc = y - _group_sum2(y, bd) * (1.0 / HEAD_DIM)
    var = _group_sum2(yc * yc, bd) * (1.0 / HEAD_DIM)
    yn = yc * lax.rsqrt(var + GN_EPS) * lng_ref[...] + lnb_ref[...]
    o = ((yn + bonus) * gate).astype(o_ref.dtype)
    for j in range(nb):
        o_ref[j] = o[j * c:(j + 1) * c]

    @pl.when(ci == pl.num_programs(1) - 1)
    def _():
        for j in range(nb):
            for p in range(pairs):
                sp = s_scr[j, p]
                sout_ref[j, 2 * p] = sp[:, :HEAD_DIM]
                sout_ref[j, 2 * p + 1] = sp[:, HEAD_DIM:]


def _rwkv(zs3, shift0, wkv0, p, chunk, nb):
    b, t, _ = zs3.shape
    const = lambda bi, ci: (0, 0)
    full = lambda a: pl.BlockSpec(a.shape, const)
    params = [p["mu"], p["w0"], p["w2"], p["a0"], p["a2"], p["g2"], p["k_k"], p["k_a"], p["r_k"],
              p["lnx_g"], p["lnx_b"], p["bd"]]
    state_spec = pl.BlockSpec((nb, RWKV_HEADS, HEAD_DIM, HEAD_DIM), lambda bi, ci: (bi, 0, 0, 0))
    return pl.pallas_call(
        functools.partial(_rwkv_kernel, chunk=chunk),
        grid=(b // nb, t // chunk),
        in_specs=[pl.BlockSpec((nb, chunk, SHIFT_PAD), lambda bi, ci: (bi, ci, 0)),
                  pl.BlockSpec((nb, 1, SHIFT_PAD), lambda bi, ci: (bi, 0, 0)),
                  state_spec] + [full(a) for a in params],
        out_specs=[pl.BlockSpec((nb, chunk, RWKV_WIDTH), lambda bi, ci: (bi, ci, 0)), state_spec],
        out_shape=[jax.ShapeDtypeStruct((b, t, RWKV_WIDTH), BF16),
                   jax.ShapeDtypeStruct((b, RWKV_HEADS, HEAD_DIM, HEAD_DIM), F32)],
        scratch_shapes=[pltpu.VMEM((nb, RWKV_HEADS // 2, HEAD_DIM, LANES), F32),
                        pltpu.VMEM((nb, 1, SHIFT_PAD), F32)],
        compiler_params=pltpu.CompilerParams(dimension_semantics=("parallel", "arbitrary"),
                                             vmem_limit_bytes=VMEM_LIMIT),
        name="rwkv7",
    )(zs3, shift0, wkv0, *params)


def _attn_kernel(*refs, n_grp, tq, n_main, n_extra, qb, pos0, n_valid, topk):
    if n_extra:
        q_ref, qi_ref, misc_ref, k_ref, v_ref, ki_ref, kx_ref, vx_ref, kix_ref, cnt_ref, o_ref = refs
    else:
        q_ref, qi_ref, misc_ref, k_ref, v_ref, ki_ref, cnt_ref, o_ref, logit_scr = refs
    n_keys = n_main + (LANES if n_extra else 0)
    nl = n_keys // LANES
    rows = n_grp * tq

    def pad_rows(x):
        return jnp.concatenate([x, jnp.zeros((LANES - n_extra, x.shape[1]), x.dtype)], axis=0)

    def dot_keys(lhs, main, extra):
        out = lax.dot_general(lhs, main, NT, preferred_element_type=F32)
        if n_extra:
            out = jnp.concatenate([out, lax.dot_general(lhs, extra, NT, preferred_element_type=F32)], axis=1)
        return out

    def row_limit(r):
        return min(((pos0 + qb * tq + r % tq) // CHUNK + 1) * CHUNK, n_valid)

    pos = pos0 + qb * tq + lax.broadcasted_iota(jnp.int32, (tq, LANES), 0)
    limit = jnp.minimum((jnp.right_shift(pos, CHUNK_SHIFT) + 1) * CHUNK, n_valid)
    if n_grp > 1:
        limit = jnp.concatenate([limit] * n_grp, axis=0)
    key_id = lax.broadcasted_iota(jnp.int32, (rows, n_keys), 1)
    adm = key_id < jnp.tile(limit, (1, nl))

    def topk_bias():
        parts = []
        for g in range(n_grp):
            qi = qi_ref[g]
            ki = ki_ref[g]
            kix = pad_rows(kix_ref[g]) if n_extra else None
            misc = misc_ref[g]
            sc = None
            if tq * IDX_HEADS <= STACK_ROWS:
                d_all = dot_keys(jnp.concatenate([qi[:, h * LANES:(h + 1) * LANES] for h in range(IDX_HEADS)],
                                                 axis=0), ki, kix)
            for h in range(IDX_HEADS):
                if tq * IDX_HEADS <= STACK_ROWS:
                    d = d_all[h * tq:(h + 1) * tq]
                else:
                    d = dot_keys(qi[:, h * LANES:(h + 1) * LANES], ki, kix)
                term = misc[:, IDX_DIM + h:IDX_DIM + h + 1] * jnp.maximum(d, 0.0)
                sc = term if sc is None else sc + term
            parts.append(sc)
        score = parts[0] if n_grp == 1 else jnp.concatenate(parts, axis=0)
        sm = jnp.where(adm, score, -jnp.inf)

        cnt_rhs = cnt_ref[...]
        ones_sq = cnt_ref[:, LANES:]

        def count_ge(sm_c, th, tiles):
            part = None
            for j in range(tiles):
                ind = jnp.where(sm_c[:, j * LANES:(j + 1) * LANES] >= th, 1.0, 0.0)
                part = ind if part is None else part + ind
            return jnp.dot(part.astype(BF16), ones_sq, preferred_element_type=F32)

        row_max = jnp.max(sm, axis=-1, keepdims=True)
        row_min = jnp.min(jnp.where(adm, score, jnp.inf), axis=-1, keepdims=True)
        lo0 = jnp.broadcast_to(row_min, (rows, LANES))
        hi0 = jnp.broadcast_to(2.0 * row_max - row_min + 1.0, (rows, LANES))

        rc = rows // BISECT_CHAINS
        chains = [slice(i * rc, (i + 1) * rc) for i in range(BISECT_CHAINS)]
        tiles = [-(-max(row_limit(r) for r in range(ch.start, ch.stop)) // LANES) for ch in chains]

        def bisect(_, carry):
            out = []
            for ch, nt, (lo, hi) in zip(chains, tiles, carry):
                mid = 0.5 * (lo + hi)
                enough = count_ge(sm[ch], mid, nt) >= topk
                out.append((jnp.where(enough, mid, lo), jnp.where(enough, hi, mid)))
            return tuple(out)

        bounds = lax.fori_loop(0, BISECT_ITERS, bisect, tuple((lo0[ch], hi0[ch]) for ch in chains),
                               unroll=BISECT_UNROLL)
        lo = jnp.concatenate([b[0] for b in bounds], axis=0)
        hi = jnp.concatenate([b[1] for b in bounds], axis=0)

        need = topk - count_ge(sm, hi, nl)
        take_all = limit <= topk
        carry = jnp.zeros((rows, LANES), F32)
        bias = []
        for j in range(nl):
            sm_j = sm[:, j * LANES:(j + 1) * LANES]
            above = sm_j >= hi
            at_least_lo = sm_j >= lo
            group = jnp.where(above, 0.0, jnp.where(at_least_lo, 1.0, 0.0))
            pc = jnp.dot(group.astype(BF16), cnt_rhs, preferred_element_type=F32)
            fill = carry + pc[:, :LANES] <= need
            carry = carry + pc[:, LANES:]
            b_group = jnp.where(at_least_lo, jnp.where(fill, 0.0, MASK_BIAS), MASK_BIAS)
            b_pick = jnp.where(take_all, 0.0, jnp.where(above, 0.0, b_group))
            bias.append(jnp.where(sm_j > -jnp.inf, b_pick, MASK_BIAS))
        return jnp.concatenate(bias, axis=1)

    def keys_values(g):
        km, vm = k_ref[g], v_ref[g]
        if km.dtype != BF16:
            km = _spread_heads(km, 0.0).astype(BF16)
            vm = _spread_heads(vm, 1.0).astype(BF16)
        kx, vx = (pad_rows(kx_ref[g]), pad_rows(vx_ref[g])) if n_extra else (None, None)
        return km, vm, kx, vx

    def qk_logits(g, gi, km, kx):
        q = q_ref[g]
        gs = slice(gi * LANES, (gi + 1) * LANES)
        qs = jnp.concatenate([q[:, (gi * GQA_GROUP + j) * LANES:(gi * GQA_GROUP + j + 1) * LANES]
                              for j in range(GQA_GROUP)], axis=0)
        return dot_keys(qs, km[:, gs], kx[:, gs] if n_extra else None)

    early = not n_extra
    if early:
        kv = [keys_values(g) for g in range(n_grp)]
        for g in range(n_grp):
            for gi in range(N_KV_HEADS):
                logit_scr[g * N_KV_HEADS + gi] = qk_logits(g, gi, kv[g][0], kv[g][2])

    if max(row_limit(r) for r in range(rows)) <= topk:
        bias = jnp.where(adm, 0.0, MASK_BIAS)
    else:
        bias = topk_bias()

    for g in range(n_grp):
        km, vm, kx, vx = kv[g] if early else keys_values(g)
        bias_g = jnp.concatenate([bias[g * tq:(g + 1) * tq]] * GQA_GROUP, axis=0)
        slabs = []
        for gi in range(N_KV_HEADS):
            gs = slice(gi * LANES, (gi + 1) * LANES)
            s = (logit_scr[g * N_KV_HEADS + gi] if early else qk_logits(g, gi, km, kx)) + bias_g
            m = jnp.max(s, axis=-1, keepdims=True)
            pexp = jnp.exp((s - m).astype(BF16))
            o = jnp.dot(pexp[:, :n_main], vm[:, gs], preferred_element_type=F32)
            if n_extra:
                o = o + jnp.dot(pexp[:, n_main:], vx[:, gs], preferred_element_type=F32)
            o = o / pltpu.roll(o, HEAD_DIM, 1)
            slabs.extend(o[j * tq:(j + 1) * tq] for j in range(GQA_GROUP))
        o_ref[g] = _pack_heads(slabs).astype(o_ref.dtype)


def _attention(q3, qi3, misc3, main, extra, cnt_rhs, *, n_grp, tq, n_main, qb, pos0, n_valid, topk):
    b = q3.shape[0]
    n_extra = extra[0].shape[1] if extra else 0
    qmap = lambda bi: (bi, qb, 0)
    kmap = lambda bi: (bi, 0, 0)
    key_specs = [pl.BlockSpec((n_grp, n_main, a.shape[2]), kmap) for a in main]
    if extra:
        key_specs += [pl.BlockSpec((n_grp, n_extra, a.shape[2]), kmap) for a in extra]
    return pl.pallas_call(
        functools.partial(_attn_kernel, n_grp=n_grp, tq=tq, n_main=n_main, n_extra=n_extra, qb=qb, pos0=pos0,
                          n_valid=n_valid, topk=topk),
        grid=(b // n_grp,),
        in_specs=[pl.BlockSpec((n_grp, tq, N_HEADS * LANES), qmap),
                  pl.BlockSpec((n_grp, tq, IDX_HEADS * LANES), qmap),
                  pl.BlockSpec((n_grp, tq, LANES), qmap)] + key_specs
        + [pl.BlockSpec(cnt_rhs.shape, lambda bi: (0, 0))],
        out_specs=pl.BlockSpec((n_grp, tq, ATT_WIDTH), lambda bi: (bi, 0, 0)),
        out_shape=jax.ShapeDtypeStruct((b, tq, ATT_WIDTH), BF16),
        scratch_shapes=[] if extra else [pltpu.VMEM((n_grp * N_KV_HEADS, GQA_GROUP * tq, n_main), F32)],
        compiler_params=pltpu.CompilerParams(dimension_semantics=("parallel",), vmem_limit_bytes=VMEM_LIMIT),
        name=f"dsa_attn_{n_main + n_extra}",
    )(q3, qi3, misc3, *main, *(extra or ()), cnt_rhs)


def _outffn_kernel(x_ref, oa_ref, ob_ref, wo_a_ref, wo_b_ref, g_ref, w1_ref, w2_ref, y_ref, *, ff_chunk):
    mix = (jnp.dot(oa_ref[...], wo_a_ref[...], preferred_element_type=F32)
           + jnp.dot(ob_ref[...], wo_b_ref[...], preferred_element_type=F32))
    h = x_ref[...] + mix
    ms = jnp.mean(h * h, axis=-1, keepdims=True)
    hn = (h * lax.rsqrt(ms + NORM_EPS) * g_ref[...]).astype(BF16)
    acc = h
    for j in range(D_FF // ff_chunk):
        act = jnp.dot(hn, w1_ref[:, j * ff_chunk:(j + 1) * ff_chunk], preferred_element_type=F32)
        act = jnp.square(jnp.maximum(act, 0.0)).astype(BF16)
        acc = acc + jnp.dot(act, w2_ref[j * ff_chunk:(j + 1) * ff_chunk, :], preferred_element_type=F32)
    y_ref[...] = acc


def _outffn(x2d, o_att, o_rwkv, p, tm):
    n = x2d.shape[0]
    row = lambda i: (i, 0)
    const = lambda i: (0, 0)
    params = [p["wo_a"], p["wo_b"], p["norm_ffn"], p["w_ff1"], p["w_ff2"]]
    return pl.pallas_call(
        functools.partial(_outffn_kernel, ff_chunk=1024),
        grid=(n // tm,),
        in_specs=[pl.BlockSpec((tm, D_MODEL), row), pl.BlockSpec((tm, ATT_WIDTH), row),
                  pl.BlockSpec((tm, RWKV_WIDTH), row)]
        + [pl.BlockSpec(a.shape, const, pipeline_mode=pl.Buffered(1)) for a in params],
        out_specs=pl.BlockSpec((tm, D_MODEL), row),
        out_shape=jax.ShapeDtypeStruct((n, D_MODEL), F32),
        compiler_params=pltpu.CompilerParams(dimension_semantics=("parallel",), vmem_limit_bytes=VMEM_LIMIT),
        name="outproj_ffn",
    )(x2d, o_att, o_rwkv, *params)


def _block_diag_ones():
    i = np.arange(ATT_WIDTH)
    return jnp.asarray((i[:, None] // HEAD_DIM) == (i[None, :] // HEAD_DIM), BF16)


def _count_rhs():
    i = np.arange(LANES)
    tri = (i[:, None] <= i[None, :]).astype(np.float32)
    return jnp.asarray(np.concatenate([tri, np.ones((LANES, LANES), np.float32)], axis=1), BF16)


def _prep_params(wl):
    (norm_mix, w_in, q_gain, k_gain, kidx_ln_g, kidx_ln_b, mu_shift, w0, w2, a0, a2, g2, k_k, k_a, r_k,
     ln_x_g, ln_x_b, w_out, norm_ffn, w_ff1, w_ff2) = wl
    row = lambda a: a.reshape(1, -1).astype(F32)
    o_ki = QKV_WIDTH
    o_wi = o_ki + IDX_DIM
    o_zs = o_wi + IDX_HEADS
    pad_cols = lambda a, wdt: jnp.pad(a, ((0, 0), (0, wdt - a.shape[1])))
    lora_rows = lambda a, off: jnp.pad(a, ((off, LORA_PAD - off - a.shape[0]), (0, 0))).astype(BF16)
    return {
        "norm_mix": row(norm_mix),
        "wa": w_in[:, :QKV_WIDTH].astype(BF16),
        "wb": pad_cols(w_in[:, o_ki:o_zs], LANES).astype(BF16),
        "wc": pad_cols(w_in[:, o_zs:], SHIFT_PAD).astype(BF16),
        "q_gain": row(jnp.tile(q_gain, N_HEADS)),
        "k_gain": row(jnp.tile(k_gain, N_KV_HEADS)),
        "ln_g": pad_cols(row(kidx_ln_g), LANES),
        "ln_b": pad_cols(row(kidx_ln_b), LANES),
        "bd": _block_diag_ones(),
        "mu": pad_cols(row(mu_shift), SHIFT_PAD),
        "w0": row(w0), "a0": row(a0),
        "w2": lora_rows(w2, 0), "a2": lora_rows(a2, W_LORA), "g2": lora_rows(g2, W_LORA + A_LORA),
        "k_k": row(k_k), "k_a": row(k_a), "r_k": row(r_k), "lnx_g": row(ln_x_g), "lnx_b": row(ln_x_b),
        "wo_a": w_out[:ATT_WIDTH].astype(BF16), "wo_b": w_out[ATT_WIDTH:].astype(BF16),
        "norm_ffn": row(norm_ffn),
        "w_ff1": w_ff1.astype(BF16), "w_ff2": w_ff2.astype(BF16),
    }


def _layer(x, pos0, k_past, v_past, ki_past, wkv0, shift0, p, cnt_rhs, *, chunk, n_grp, tq):
    b, t, _ = x.shape
    n = b * t
    past = k_past.shape[1]
    x2d = x.reshape(n, D_MODEL)
    tm = min(ROW_TILE, n)
    q, k, v, qi, misc, zs, kbf, vbf, kibf = _inproj(x2d, t, pos0, p, tm)

    shift0p = jnp.pad(shift0, ((0, 0), (0, 0), (0, SHIFT_PAD - SHIFT_WIDTH)))
    zs3 = zs.reshape(b, t, SHIFT_PAD)
    o_rwkv, wkv_new = _rwkv(zs3, shift0p, wkv0, p, chunk, min(b, RWKV_BATCH))

    n_valid = past + t
    topk = min(TOPK_MAX, n_valid // 4)
    three = lambda a: a.reshape(b, t, a.shape[-1])
    k_new, v_new, ki_new = three(kbf), three(vbf), three(kibf)
    q3, qi3, misc3 = three(q), three(qi), three(misc)
    if past:
        ki_main = jnp.pad(ki_past.astype(BF16), ((0, 0), (0, 0), (0, LANES - IDX_DIM)))
        main = (k_past.reshape(b, past, KV_WIDTH), v_past.reshape(b, past, KV_WIDTH), ki_main)
        o_att = _attention(q3, qi3, misc3, main, (k_new, v_new, ki_new), cnt_rhs, n_grp=n_grp, tq=tq,
                           n_main=past, qb=0, pos0=pos0, n_valid=n_valid, topk=topk)
    else:
        o_att = jnp.concatenate(
            [_attention(q3, qi3, misc3, (k_new, v_new, ki_new), None, cnt_rhs, n_grp=n_grp, tq=tq,
                        n_main=(j + 1) * tq, qb=j, pos0=pos0, n_valid=n_valid, topk=topk)
             for j in range(t // tq)], axis=1)

    y = _outffn(x2d, o_att.reshape(n, ATT_WIDTH), o_rwkv.reshape(n, RWKV_WIDTH), p, tm)
    return (y.reshape(b, t, D_MODEL), k.reshape(b, t, N_KV_HEADS, HEAD_DIM),
            v.reshape(b, t, N_KV_HEADS, HEAD_DIM), misc3[:, :, :IDX_DIM], wkv_new,
            zs3[:, t - 1:, :SHIFT_WIDTH])


def kernel(x_prompt, x_sample, cache_k, cache_v, cache_kidx, state_wkv, state_shift, norm_mix, w_in, q_gain, k_gain, kidx_ln_g, kidx_ln_b, mu_shift, w0, w2, a0, a2, g2, k_k, k_a, r_k, ln_x_g, ln_x_b, w_out, norm_ffn, w_ff1, w_ff2):
    depth = norm_mix.shape[0]
    bp, tp = x_prompt.shape[0], x_prompt.shape[1]
    bs, ts = x_sample.shape[0], x_sample.shape[1]
    past_len = cache_k.shape[2]
    cnt_rhs = _count_rhs()
    no_kv = jnp.zeros((bp, 0, KV_WIDTH), F32)
    no_ki = jnp.zeros((bp, 0, IDX_DIM), F32)
    wkv_zero = jnp.zeros((bp, RWKV_HEADS, HEAD_DIM, HEAD_DIM), F32)
    shift_zero = jnp.zeros((bp, 1, SHIFT_WIDTH), F32)

    y_p, y_s = x_prompt, x_sample
    outs_p, outs_s = [], []
    for l in range(depth):
        wl = (norm_mix[l], w_in[l], q_gain[l], k_gain[l], kidx_ln_g[l], kidx_ln_b[l], mu_shift[l],
              w0[l], w2[l], a0[l], a2[l], g2[l], k_k[l], k_a[l], r_k[l], ln_x_g[l], ln_x_b[l],
              w_out[l], norm_ffn[l], w_ff1[l], w_ff2[l])
        p = _prep_params(wl)
        y_p, *rest_p = _layer(y_p, 0, no_kv, no_kv, no_ki, wkv_zero, shift_zero, p, cnt_rhs,
                              chunk=64, n_grp=1, tq=256)
        y_s, *rest_s = _layer(y_s, past_len, cache_k[l], cache_v[l], cache_kidx[l], state_wkv[l],
                              state_shift[l], p, cnt_rhs, chunk=ts, n_grp=min(bs, 256 // ts), tq=ts)
        outs_p.append(rest_p)
        outs_s.append(rest_s)
    stack = lambda outs, i: jnp.stack([o[i] for o in outs])
    return (y_p, y_s,
            *(stack(outs_p, i) for i in range(5)),
            *(stack(outs_s, i) for i in range(5)))
```

```python
import functools

import numpy as np
import jax
import jax.numpy as jnp
from jax import lax
from jax.experimental import pallas as pl
from jax.experimental.pallas import tpu as pltpu

F32 = jnp.float32
BF16 = jnp.bfloat16

D_MODEL = 1024
CHUNK = 64
HEAD_DIM = 64
ATT_WIDTH = D_MODEL // 2
N_HEADS = ATT_WIDTH // HEAD_DIM
N_KV_HEADS = N_HEADS // 2
GQA_GROUP = N_HEADS // N_KV_HEADS
KV_WIDTH = N_KV_HEADS * HEAD_DIM
ROT_DIM = HEAD_DIM // 4
ROPE_THETA = 500000.0
IDX_HEADS = 8
IDX_DIM = 64
IDX_WIDTH = IDX_HEADS * IDX_DIM
TOPK_MAX = 256
RWKV_WIDTH = D_MODEL - ATT_WIDTH
RWKV_HEADS = RWKV_WIDTH // HEAD_DIM
W_LORA = 32
A_LORA = 32
G_LORA = 96
LORA_WIDTH = W_LORA + A_LORA + G_LORA
SHIFT_WIDTH = 3 * RWKV_WIDTH + LORA_WIDTH
D_FF = 4 * D_MODEL
NORM_EPS = 1e-6
GN_EPS = 64e-5
L2_EPS = 1e-12

LANES = 128
SHIFT_PAD = 3 * RWKV_WIDTH + 2 * LANES
LORA_PAD = SHIFT_PAD - 3 * RWKV_WIDTH
QKV_WIDTH = ATT_WIDTH + 2 * KV_WIDTH + IDX_WIDTH
INV_BLOCK = 16
INV_SHIFT = 4
CHUNK_SHIFT = 6
MASK_BIAS = -1e30
ROW_TILE = 512
RWKV_BATCH = 8
STACK_ROWS = 512
BISECT_ITERS = 32
BISECT_CHAINS = 4
BISECT_UNROLL = 16
VMEM_LIMIT = 56 * 1024 * 1024

NN = (((1,), (0,)), ((), ()))
NT = (((1,), (1,)), ((), ()))
TN = (((0,), (0,)), ((), ()))


def _dot(a, b, dims=NN):
    return lax.dot_general(a.astype(BF16), b.astype(BF16), dims, preferred_element_type=F32)


def _group_sum(x, bd):
    w = x.shape[-1]
    return _dot(x, bd[:w, :w])


def _rope(x, cos, sa, sb):
    w = x.shape[-1]
    reps = w // LANES
    if reps > 1:
        cos, sa, sb = (jnp.tile(t, (1, reps)) for t in (cos, sa, sb))
    half = ROT_DIM // 2
    return x * cos + pltpu.roll(x, w - half, 1) * sa + pltpu.roll(x, half, 1) * sb


def _spread_heads(x, fill):
    low = lax.broadcasted_iota(jnp.int32, (x.shape[0], LANES), 1) < HEAD_DIM
    cols = []
    for c in range(x.shape[1] // LANES):
        pair = x[:, c * LANES:(c + 1) * LANES]
        cols.append(jnp.where(low, pair, fill))
        cols.append(jnp.where(low, pltpu.roll(pair, HEAD_DIM, 1), fill))
    return jnp.concatenate(cols, axis=1)


def _pack_heads(slabs):
    low = lax.broadcasted_iota(jnp.int32, slabs[0].shape, 1) < HEAD_DIM
    cols = [jnp.where(low, slabs[i], pltpu.roll(slabs[i + 1], HEAD_DIM, 1)) for i in range(0, len(slabs), 2)]
    return jnp.concatenate(cols, axis=1)


def _inproj_kernel(x_ref, g_ref, wa_ref, wb_ref, wc_ref, qg_ref, kg_ref, lng_ref, lnb_ref, bd_ref,
                   cos_ref, sa_ref, sb_ref,
                   q_ref, k_ref, v_ref, qi_ref, misc_ref, zs_ref, kbf_ref, vbf_ref, kibf_ref):
    x = x_ref[...]
    ms = jnp.mean(x * x, axis=-1, keepdims=True)
    xn = (x * lax.rsqrt(ms + NORM_EPS) * g_ref[...]).astype(BF16)
    za = jnp.dot(xn, wa_ref[...], preferred_element_type=F32)
    zb = jnp.dot(xn, wb_ref[...], preferred_element_type=F32)
    zs_ref[...] = jnp.dot(xn, wc_ref[...], preferred_element_type=F32)

    cos, sa, sb = cos_ref[...], sa_ref[...], sb_ref[...]
    bd = bd_ref[...]

    zq = za[:, :ATT_WIDTH]
    qn = zq * lax.rsqrt(_group_sum(zq * zq, bd) * (1.0 / HEAD_DIM) + NORM_EPS) * qg_ref[...]
    q_ref[...] = _spread_heads(_rope(qn, cos, sa, sb) * (HEAD_DIM ** -0.5), 0.0).astype(BF16)

    zk = za[:, ATT_WIDTH:ATT_WIDTH + KV_WIDTH]
    kn = zk * lax.rsqrt(_group_sum(zk * zk, bd) * (1.0 / HEAD_DIM) + NORM_EPS) * kg_ref[...]
    k = _rope(kn, cos, sa, sb)
    k_ref[...] = k
    kbf_ref[...] = _spread_heads(k, 0.0).astype(BF16)

    v = za[:, ATT_WIDTH + KV_WIDTH:ATT_WIDTH + 2 * KV_WIDTH]
    v_ref[...] = v
    vbf_ref[...] = _spread_heads(v, 1.0).astype(BF16)

    zqi = za[:, ATT_WIDTH + 2 * KV_WIDTH:]
    qi_ref[...] = _spread_heads(_rope(zqi, cos, sa, sb) * (IDX_DIM ** -0.5), 0.0).astype(BF16)

    lane = lax.broadcasted_iota(jnp.int32, zb.shape, 1)
    is_ki = lane < IDX_DIM
    mu = jnp.sum(jnp.where(is_ki, zb, 0.0), axis=-1, keepdims=True) * (1.0 / IDX_DIM)
    d = jnp.where(is_ki, zb - mu, 0.0)
    var = jnp.sum(d * d, axis=-1, keepdims=True) * (1.0 / IDX_DIM)
    kin = d * lax.rsqrt(var + NORM_EPS) * lng_ref[...] + lnb_ref[...]
    ki = _rope(kin, cos, sa, sb)
    misc_ref[...] = jnp.where(is_ki, ki, zb * (IDX_HEADS ** -0.5))
    kibf_ref[...] = jnp.where(is_ki, ki, 0.0).astype(BF16)


def _rope_tables(pos):
    half = ROT_DIM // 2
    inv = ROPE_THETA ** (-jnp.arange(0, ROT_DIM, 2, dtype=F32) / ROT_DIM)
    ang = pos.astype(F32)[:, None] * inv[None, :]
    cos, sin = jnp.cos(ang), jnp.sin(ang)
    n = pos.shape[0]
    ones = jnp.ones((n, HEAD_DIM - ROT_DIM), F32)
    zeros_h = jnp.zeros((n, half), F32)
    zeros_r = jnp.zeros((n, HEAD_DIM - ROT_DIM), F32)
    c64 = jnp.concatenate([cos, cos, ones], axis=1)
    sa64 = jnp.concatenate([-sin, zeros_h, zeros_r], axis=1)
    sb64 = jnp.concatenate([zeros_h, sin, zeros_r], axis=1)
    rep = LANES // HEAD_DIM
    return tuple(jnp.tile(t, (1, rep)) for t in (c64, sa64, sb64))


def _inproj(x2d, seq_len, pos0, p, tm):
    n = x2d.shape[0]
    tab_rows = max(seq_len, tm)
    pos = pos0 + (jnp.arange(tab_rows, dtype=jnp.int32) % seq_len)
    cos, sa, sb = _rope_tables(pos)
    nblk_t = tab_rows // tm

    def row(i):
        return (i, 0)

    def const(i):
        return (0, 0)

    def tab(i):
        return (i % nblk_t, 0)

    full = lambda a: pl.BlockSpec(a.shape, const)
    outs = [
        (N_HEADS * LANES, BF16), (KV_WIDTH, F32), (KV_WIDTH, F32), (IDX_HEADS * LANES, BF16), (LANES, F32),
        (SHIFT_PAD, F32), (N_KV_HEADS * LANES, BF16), (N_KV_HEADS * LANES, BF16), (LANES, BF16),
    ]
    params = [p["norm_mix"], p["wa"], p["wb"], p["wc"], p["q_gain"], p["k_gain"], p["ln_g"], p["ln_b"], p["bd"]]
    return pl.pallas_call(
        _inproj_kernel,
        grid=(n // tm,),
        in_specs=[pl.BlockSpec((tm, D_MODEL), row)] + [full(a) for a in params]
        + [pl.BlockSpec((tm, LANES), tab)] * 3,
        out_specs=[pl.BlockSpec((tm, w), row) for w, _ in outs],
        out_shape=[jax.ShapeDtypeStruct((n, w), dt) for w, dt in outs],
        compiler_params=pltpu.CompilerParams(dimension_semantics=("parallel",), vmem_limit_bytes=VMEM_LIMIT),
        name="inproj",
    )(x2d, *params, cos, sa, sb)


def _left_head(shape):
    return lax.broadcasted_iota(jnp.int32, shape, 1) < shape[1] // 2


def _bdiag(x):
    x = x.astype(BF16)
    zero = jnp.zeros_like(x)
    left = _left_head(x.shape)
    return jnp.concatenate([jnp.where(left, x, zero), jnp.where(left, zero, x)], axis=0)


def _pdot(x, y, dims=NN):
    return lax.dot_general(x.astype(BF16), _bdiag(y), dims, preferred_element_type=F32)


def _solve_unit_lower(lows, rhss, eye, blk):
    c = lows[0].shape[0]
    d = [jnp.where(blk, l, 0.0) for l in lows]
    e = [l - dh for l, dh in zip(lows, d)]
    dk = [_pdot(dh, dh) for dh in d]
    p = [_pdot(eye - dh, eye + d2) for dh, d2 in zip(d, dk)]
    span = 4
    while span < INV_BLOCK:
        dk = [_pdot(m, m) for m in dk]
        p = [_pdot(ph, eye + m) for ph, m in zip(p, dk)]
        span *= 2
    x = [_pdot(ph, r) for ph, r in zip(p, rhss)]
    nb = c // INV_BLOCK
    if nb > 1:
        nk = [_pdot(ph, eh) for ph, eh in zip(p, e)]
        x = [_pdot(eye - m, xh) for m, xh in zip(nk, x)]
        order = 2
        while order < nb:
            nk = [_pdot(m, m) for m in nk]
            x = [_pdot(eye + m, xh) for m, xh in zip(nk, x)]
            order *= 2
    return x


def _group_sum2(x, bd):
    hi = x.astype(BF16)
    lo = (x - hi.astype(F32)).astype(BF16)
    return (jnp.dot(hi, bd, preferred_element_type=F32) + jnp.dot(lo, bd, preferred_element_type=F32))


def _rwkv_kernel(zs_ref, shift0_ref, wkv0_ref, mu_ref, w0_ref, w2_ref, a0_ref, a2_ref, g2_ref,
                 kk_ref, ka_ref, rk_ref, lng_ref, lnb_ref, bd_ref,
                 o_ref, sout_ref, s_scr, prev_scr, *, chunk):
    c = chunk
    nb = zs_ref.shape[0]
    ci = pl.program_id(1)
    w = RWKV_WIDTH
    pairs = RWKV_HEADS // 2

    @pl.when(ci == 0)
    def _():
        for j in range(nb):
            for p in range(pairs):
                s_scr[j, p] = jnp.concatenate([wkv0_ref[j, 2 * p], wkv0_ref[j, 2 * p + 1]], axis=1)
        prev_scr[...] = shift0_ref[...]

    row = lax.broadcasted_iota(jnp.int32, (c, 2 * c), 0)
    lane = lax.broadcasted_iota(jnp.int32, (c, 2 * c), 1)
    col = jnp.bitwise_and(lane, c - 1)
    tri_incl = row >= col
    tri_strict = row > col
    eye = jnp.where(row == col, 1.0, 0.0)
    blk = jnp.right_shift(row, INV_SHIFT) == jnp.right_shift(col, INV_SHIFT)
    trow = lax.broadcasted_iota(jnp.int32, (c, c), 0)
    tcol = lax.broadcasted_iota(jnp.int32, (c, c), 1)
    tri = jnp.where(trow >= tcol, 1.0, 0.0).astype(BF16)
    mu = mu_ref[...]
    bd = bd_ref[...]
    sls = [slice(p * LANES, (p + 1) * LANES) for p in range(pairs)]

    rs, ks, vs, tails = [], [], [], []
    for j in range(nb):
        z = zs_ref[j]
        prev_row = prev_scr[j]

        def mixed(lo, hi):
            zp = z[:, lo:hi]
            rolled = pltpu.roll(zp, 1, 0)
            first = lax.broadcasted_iota(jnp.int32, zp.shape, 0) == 0
            prv = jnp.where(first, prev_row[:, lo:hi], rolled)
            return zp + mu[:, lo:hi] * (prv - zp)

        rs.append(mixed(0, w))
        ks.append(mixed(w, 2 * w))
        vs.append(mixed(2 * w, 3 * w))
        tails.append(mixed(3 * w, SHIFT_PAD))
        prev_scr[j] = z[c - 1:c, :]
    stack = lambda xs: xs[0] if nb == 1 else jnp.concatenate(xs, axis=0)
    r, k, v, tail = stack(rs), stack(ks), stack(vs), stack(tails)

    lw = w0_ref[...] + _dot(jnp.tanh(tail), w2_ref[...])
    nlw = -lw
    softplus = jnp.maximum(nlw, 0.0) + jnp.log(1.0 + jnp.exp(-jnp.abs(nlw)))
    logd = -jnp.exp(-softplus - 0.5)
    a = 1.0 / (1.0 + jnp.exp(-(a0_ref[...] + _dot(tail, a2_ref[...]))))
    gate = _dot(1.0 / (1.0 + jnp.exp(-tail)), g2_ref[...])

    kk = k * kk_ref[...]
    kk = kk / jnp.maximum(jnp.sqrt(_group_sum(kk * kk, bd)), L2_EPS)
    k = k * (1.0 + (a - 1.0) * ka_ref[...])
    b = kk * a

    hi = logd.astype(BF16)
    rem = logd - hi.astype(F32)
    mid = rem.astype(BF16)
    lo = (rem - mid.astype(F32)).astype(BF16)
    lg = stack([jnp.dot(tri, hi[rj], preferred_element_type=F32) + jnp.dot(tri, mid[rj], preferred_element_type=F32)
                + jnp.dot(tri, lo[rj], preferred_element_type=F32)
                for rj in (slice(j * c, (j + 1) * c) for j in range(nb))])
    g = jnp.exp(lg)
    ginv = jnp.exp(-lg)
    a_m = jnp.exp(lg - logd) * kk
    b_m = b * ginv
    k_m = k * ginv
    p_m = g * r
    bonus = _group_sum2(r * k * rk_ref[...], bd) * v

    ar, km, bm, vh, g_last = [], [], [], [], []
    for j in range(nb):
        rj = slice(j * c, (j + 1) * c)
        ar += [jnp.concatenate([a_m[rj, sl], p_m[rj, sl]], axis=0).astype(BF16) for sl in sls]
        km += [k_m[rj, sl].astype(BF16) for sl in sls]
        bm += [b_m[rj, sl].astype(BF16) for sl in sls]
        vh += [v[rj, sl].astype(BF16) for sl in sls]
        g_last += [g[(j + 1) * c - 1:(j + 1) * c, sl] for sl in sls]

    chains = range(nb * pairs)
    s_h = [s_scr[j, p] for j in range(nb) for p in range(pairs)]
    w2c = 2 * c
    gs = [lax.dot_general(ar[i], jnp.concatenate([_bdiag(km[i]), _bdiag(bm[i]), _bdiag(s_h[i])], axis=0), NT,
                          preferred_element_type=F32) for i in chains]
    g_k = [m[:, :w2c] for m in gs]
    g_b = [m[:, w2c:2 * w2c] for m in gs]
    a_s = [m[:, 2 * w2c:] for m in gs]
    rhs = [a_s[i][:c] + _pdot(jnp.where(tri_strict, g_k[i][:c], 0.0), vh[i]) for i in chains]
    u = _solve_unit_lower([jnp.where(tri_strict, g_b[i][:c], 0.0) for i in chains], rhs, eye, blk)
    ys = []
    for i in chains:
        j, p = divmod(i, pairs)
        n_pk = jnp.where(tri_incl, g_k[i][c:], 0.0)
        n_pb = jnp.where(tri_incl, g_b[i][c:], 0.0)
        ys.append(a_s[i][c:] + _dot(jnp.concatenate([n_pk, -n_pb], axis=1),
                                     jnp.concatenate([_bdiag(vh[i]), _bdiag(u[i])], axis=0)))
        full = _dot(jnp.concatenate([vh[i], -u[i].astype(BF16)], axis=0),
                    jnp.concatenate([km[i], bm[i]], axis=0), TN)
        diag = jnp.where(_left_head((HEAD_DIM, LANES)), full[:HEAD_DIM], full[HEAD_DIM:])
        s_scr[j, p] = (s_h[i] + diag) * g_last[i]

    y = stack([jnp.concatenate(ys[j * pairs:(j + 1) * pairs], axis=1) for j in range(nb)])
    yc = y - _group_sum2(y, bd) * (1.0 / HEAD_DIM)
    var = _group_sum2(yc * yc, bd) * (1.0 / HEAD_DIM)
    yn = yc * lax.rsqrt(var + GN_EPS) * lng_ref[...] + lnb_ref[...]
    o = ((yn + bonus) * gate).astype(o_ref.dtype)
    for j in range(nb):
        o_ref[j] = o[j * c:(j + 1) * c]

    @pl.when(ci == pl.num_programs(1) - 1)
    def _():
        for j in range(nb):
            for p in range(pairs):
                sp = s_scr[j, p]
                sout_ref[j, 2 * p] = sp[:, :HEAD_DIM]
                sout_ref[j, 2 * p + 1] = sp[:, HEAD_DIM:]


def _rwkv(zs3, shift0, wkv0, p, chunk, nb):
    b, t, _ = zs3.shape
    const = lambda bi, ci: (0, 0)
    full = lambda a: pl.BlockSpec(a.shape, const)
    params = [p["mu"], p["w0"], p["w2"], p["a0"], p["a2"], p["g2"], p["k_k"], p["k_a"], p["r_k"],
              p["lnx_g"], p["lnx_b"], p["bd"]]
    state_spec = pl.BlockSpec((nb, RWKV_HEADS, HEAD_DIM, HEAD_DIM), lambda bi, ci: (bi, 0, 0, 0))
    return pl.pallas_call(
        functools.partial(_rwkv_kernel, chunk=chunk),
        grid=(b // nb, t // chunk),
        in_specs=[pl.BlockSpec((nb, chunk, SHIFT_PAD), lambda bi, ci: (bi, ci, 0)),
                  pl.BlockSpec((nb, 1, SHIFT_PAD), lambda bi, ci: (bi, 0, 0)),
                  state_spec] + [full(a) for a in params],
        out_specs=[pl.BlockSpec((nb, chunk, RWKV_WIDTH), lambda bi, ci: (bi, ci, 0)), state_spec],
        out_shape=[jax.ShapeDtypeStruct((b, t, RWKV_WIDTH), BF16),
                   jax.ShapeDtypeStruct((b, RWKV_HEADS, HEAD_DIM, HEAD_DIM), F32)],
        scratch_shapes=[pltpu.VMEM((nb, RWKV_HEADS // 2, HEAD_DIM, LANES), F32),
                        pltpu.VMEM((nb, 1, SHIFT_PAD), F32)],
        compiler_params=pltpu.CompilerParams(dimension_semantics=("parallel", "arbitrary"),
                                             vmem_limit_bytes=VMEM_LIMIT),
        name="rwkv7",
    )(zs3, shift0, wkv0, *params)


def _attn_kernel(*refs, n_grp, tq, n_main, n_extra, qb, pos0, n_valid, topk):
    if n_extra:
        q_ref, qi_ref, misc_ref, k_ref, v_ref, ki_ref, kx_ref, vx_ref, kix_ref, cnt_ref, o_ref = refs
    else:
        q_ref, qi_ref, misc_ref, k_ref, v_ref, ki_ref, cnt_ref, o_ref, logit_scr = refs
    n_keys = n_main + (LANES if n_extra else 0)
    nl = n_keys // LANES
    rows = n_grp * tq

    def pad_rows(x):
        return jnp.concatenate([x, jnp.zeros((LANES - n_extra, x.shape[1]), x.dtype)], axis=0)

    def dot_keys(lhs, main, extra):
        out = lax.dot_general(lhs, main, NT, preferred_element_type=F32)
        if n_extra:
            out = jnp.concatenate([out, lax.dot_general(lhs, extra, NT, preferred_element_type=F32)], axis=1)
        return out

    def row_limit(r):
        return min(((pos0 + qb * tq + r % tq) // CHUNK + 1) * CHUNK, n_valid)

    pos = pos0 + qb * tq + lax.broadcasted_iota(jnp.int32, (tq, LANES), 0)
    limit = jnp.minimum((jnp.right_shift(pos, CHUNK_SHIFT) + 1) * CHUNK, n_valid)
    if n_grp > 1:
        limit = jnp.concatenate([limit] * n_grp, axis=0)
    key_id = lax.broadcasted_iota(jnp.int32, (rows, n_keys), 1)
    adm = key_id < jnp.tile(limit, (1, nl))

    def topk_bias():
        parts = []
        for g in range(n_grp):
            qi = qi_ref[g]
            ki = ki_ref[g, :n_main]
            kix = pad_rows(kix_ref[g]) if n_extra else None
            misc = misc_ref[g]
            sc = None
            if tq * IDX_HEADS <= STACK_ROWS:
                d_all = dot_keys(jnp.concatenate([qi[:, h * LANES:(h + 1) * LANES] for h in range(IDX_HEADS)],
                                                 axis=0), ki, kix)
            for h in range(IDX_HEADS):
                if tq * IDX_HEADS <= STACK_ROWS:
                    d = d_all[h * tq:(h + 1) * tq]
                else:
                    d = dot_keys(qi[:, h * LANES:(h + 1) * LANES], ki, kix)
                term = misc[:, IDX_DIM + h:IDX_DIM + h + 1] * jnp.maximum(d, 0.0)
                sc = term if sc is None else sc + term
            parts.append(sc)
        score = parts[0] if n_grp == 1 else jnp.concatenate(parts, axis=0)
        sm = jnp.where(adm, score, -jnp.inf)

        cnt_rhs = cnt_ref[...]
        ones_sq = cnt_ref[:, LANES:]

        def count_ge(sm_c, th, tiles):
            part = None
            for j in range(tiles):
                ind = jnp.where(sm_c[:, j * LANES:(j + 1) * LANES] >= th, 1.0, 0.0)
                part = ind if part is None else part + ind
            return jnp.dot(part.astype(BF16), ones_sq, preferred_element_type=F32)

        row_max = jnp.max(sm, axis=-1, keepdims=True)
        row_min = jnp.min(jnp.where(adm, score, jnp.inf), axis=-1, keepdims=True)
        lo0 = jnp.broadcast_to(row_min, (rows, LANES))
        hi0 = jnp.broadcast_to(2.0 * row_max - row_min + 1.0, (rows, LANES))

        rc = rows // BISECT_CHAINS
        chains = [slice(i * rc, (i + 1) * rc) for i in range(BISECT_CHAINS)]
        tiles = [-(-max(row_limit(r) for r in range(ch.start, ch.stop)) // LANES) for ch in chains]

        def bisect(_, carry):
            out = []
            for ch, nt, (lo, hi) in zip(chains, tiles, carry):
                mid = 0.5 * (lo + hi)
                enough = count_ge(sm[ch], mid, nt) >= topk
                out.append((jnp.where(enough, mid, lo), jnp.where(enough, hi, mid)))
            return tuple(out)

        bounds = lax.fori_loop(0, BISECT_ITERS, bisect, tuple((lo0[ch], hi0[ch]) for ch in chains),
                               unroll=BISECT_UNROLL)
        lo = jnp.concatenate([b[0] for b in bounds], axis=0)
        hi = jnp.concatenate([b[1] for b in bounds], axis=0)

        need = topk - count_ge(sm, hi, nl)
        take_all = limit <= topk
        carry = jnp.zeros((rows, LANES), F32)
        bias = []
        for j in range(nl):
            sm_j = sm[:, j * LANES:(j + 1) * LANES]
            above = sm_j >= hi
            at_least_lo = sm_j >= lo
            group = jnp.where(above, 0.0, jnp.where(at_least_lo, 1.0, 0.0))
            pc = jnp.dot(group.astype(BF16), cnt_rhs, preferred_element_type=F32)
            fill = carry + pc[:, :LANES] <= need
            carry = carry + pc[:, LANES:]
            b_group = jnp.where(at_least_lo, jnp.where(fill, 0.0, MASK_BIAS), MASK_BIAS)
            b_pick = jnp.where(take_all, 0.0, jnp.where(above, 0.0, b_group))
            bias.append(jnp.where(sm_j > -jnp.inf, b_pick, MASK_BIAS))
        return jnp.concatenate(bias, axis=1)

    def keys_values(g):
        km, vm = k_ref[g, :n_main], v_ref[g, :n_main]
        if km.dtype != BF16:
            km = _spread_heads(km, 0.0).astype(BF16)
            vm = _spread_heads(vm, 1.0).astype(BF16)
        kx, vx = (pad_rows(kx_ref[g]), pad_rows(vx_ref[g])) if n_extra else (None, None)
        return km, vm, kx, vx

    def qk_logits(g, gi, km, kx):
        q = q_ref[g]
        gs = slice(gi * LANES, (gi + 1) * LANES)
        qs = jnp.concatenate([q[:, (gi * GQA_GROUP + j) * LANES:(gi * GQA_GROUP + j + 1) * LANES]
                              for j in range(GQA_GROUP)], axis=0)
        return dot_keys(qs, km[:, gs], kx[:, gs] if n_extra else None)

    early = not n_extra
    if early:
        kv = [keys_values(g) for g in range(n_grp)]
        for g in range(n_grp):
            for gi in range(N_KV_HEADS):
                logit_scr[g * N_KV_HEADS + gi, :, :n_main] = qk_logits(g, gi, kv[g][0], kv[g][2])

    if max(row_limit(r) for r in range(rows)) <= topk:
        bias = jnp.where(adm, 0.0, MASK_BIAS)
    else:
        bias = topk_bias()

    for g in range(n_grp):
        km, vm, kx, vx = kv[g] if early else keys_values(g)
        bias_g = jnp.concatenate([bias[g * tq:(g + 1) * tq]] * GQA_GROUP, axis=0)
        slabs = []
        for gi in range(N_KV_HEADS):
            gs = slice(gi * LANES, (gi + 1) * LANES)
            s = (logit_scr[g * N_KV_HEADS + gi, :, :n_main] if early else qk_logits(g, gi, km, kx)) + bias_g
            m = jnp.max(s, axis=-1, keepdims=True)
            pexp = jnp.exp((s - m).astype(BF16))
            o = jnp.dot(pexp[:, :n_main], vm[:, gs], preferred_element_type=F32)
            if n_extra:
                o = o + jnp.dot(pexp[:, n_main:], vx[:, gs], preferred_element_type=F32)
            o = o / pltpu.roll(o, HEAD_DIM, 1)
            slabs.extend(o[j * tq:(j + 1) * tq] for j in range(GQA_GROUP))
        o_ref[g] = _pack_heads(slabs).astype(o_ref.dtype)


def _attn_blocks_kernel(*refs, n_blocks, tq, **kw):
    for j in range(n_blocks):
        @pl.when(pl.program_id(1) == j)
        def _(j=j):
            _attn_kernel(*refs, tq=tq, n_main=(j + 1) * tq, n_extra=0, qb=j, **kw)


def _attention(q3, qi3, misc3, main, extra, cnt_rhs, *, n_grp, tq, pos0, n_valid, topk):
    b, t = q3.shape[0], q3.shape[1]
    n_rows = main[0].shape[1]
    kmap = lambda bi, qi_: (bi, 0, 0)
    qmap = lambda bi, qi_: (bi, qi_, 0)
    key_specs = [pl.BlockSpec((n_grp, n_rows, a.shape[2]), kmap) for a in main]
    if extra:
        n_extra = extra[0].shape[1]
        key_specs += [pl.BlockSpec((n_grp, n_extra, a.shape[2]), kmap) for a in extra]
        body = functools.partial(_attn_kernel, n_grp=n_grp, tq=tq, n_main=n_rows, n_extra=n_extra, qb=0, pos0=pos0,
                                 n_valid=n_valid, topk=topk)
        scratch = []
    else:
        body = functools.partial(_attn_blocks_kernel, n_blocks=t // tq, n_grp=n_grp, tq=tq, pos0=pos0,
                                 n_valid=n_valid, topk=topk)
        scratch = [pltpu.VMEM((n_grp * N_KV_HEADS, GQA_GROUP * tq, n_rows), F32)]
    return pl.pallas_call(
        body,
        grid=(b // n_grp, t // tq),
        in_specs=[pl.BlockSpec((n_grp, tq, N_HEADS * LANES), qmap),
                  pl.BlockSpec((n_grp, tq, IDX_HEADS * LANES), qmap),
                  pl.BlockSpec((n_grp, tq, LANES), qmap)] + key_specs
        + [pl.BlockSpec(cnt_rhs.shape, lambda bi, qi_: (0, 0))],
        out_specs=pl.BlockSpec((n_grp, tq, ATT_WIDTH), qmap),
        out_shape=jax.ShapeDtypeStruct((b, t, ATT_WIDTH), BF16),
        scratch_shapes=scratch,
        compiler_params=pltpu.CompilerParams(dimension_semantics=("parallel", "arbitrary"),
                                             vmem_limit_bytes=VMEM_LIMIT),
        name="dsa_attn_cached" if extra else "dsa_attn",
    )(q3, qi3, misc3, *main, *(extra or ()), cnt_rhs)


def _outffn_kernel(x_ref, oa_ref, ob_ref, wo_a_ref, wo_b_ref, g_ref, w1_ref, w2_ref, y_ref, *, ff_chunk):
    mix = (jnp.dot(oa_ref[...], wo_a_ref[...], preferred_element_type=F32)
           + jnp.dot(ob_ref[...], wo_b_ref[...], preferred_element_type=F32))
    h = x_ref[...] + mix
    ms = jnp.mean(h * h, axis=-1, keepdims=True)
    hn = (h * lax.rsqrt(ms + NORM_EPS) * g_ref[...]).astype(BF16)
    acc = h
    for j in range(D_FF // ff_chunk):
        act = jnp.dot(hn, w1_ref[:, j * ff_chunk:(j + 1) * ff_chunk], preferred_element_type=F32)
        act = jnp.square(jnp.maximum(act, 0.0)).astype(BF16)
        acc = acc + jnp.dot(act, w2_ref[j * ff_chunk:(j + 1) * ff_chunk, :], preferred_element_type=F32)
    y_ref[...] = acc


def _outffn(x2d, o_att, o_rwkv, p, tm):
    n = x2d.shape[0]
    row = lambda i: (i, 0)
    const = lambda i: (0, 0)
    params = [p["wo_a"], p["wo_b"], p["norm_ffn"], p["w_ff1"], p["w_ff2"]]
    return pl.pallas_call(
        functools.partial(_outffn_kernel, ff_chunk=1024),
        grid=(n // tm,),
        in_specs=[pl.BlockSpec((tm, D_MODEL), row), pl.BlockSpec((tm, ATT_WIDTH), row),
                  pl.BlockSpec((tm, RWKV_WIDTH), row)]
        + [pl.BlockSpec(a.shape, const, pipeline_mode=pl.Buffered(1)) for a in params],
        out_specs=pl.BlockSpec((tm, D_MODEL), row),
        out_shape=jax.ShapeDtypeStruct((n, D_MODEL), F32),
        compiler_params=pltpu.CompilerParams(dimension_semantics=("parallel",), vmem_limit_bytes=VMEM_LIMIT),
        name="outproj_ffn",
    )(x2d, o_att, o_rwkv, *params)


def _block_diag_ones():
    i = np.arange(ATT_WIDTH)
    return jnp.asarray((i[:, None] // HEAD_DIM) == (i[None, :] // HEAD_DIM), BF16)


def _count_rhs():
    i = np.arange(LANES)
    tri = (i[:, None] <= i[None, :]).astype(np.float32)
    return jnp.asarray(np.concatenate([tri, np.ones((LANES, LANES), np.float32)], axis=1), BF16)


def _prep_params(wl):
    (norm_mix, w_in, q_gain, k_gain, kidx_ln_g, kidx_ln_b, mu_shift, w0, w2, a0, a2, g2, k_k, k_a, r_k,
     ln_x_g, ln_x_b, w_out, norm_ffn, w_ff1, w_ff2) = wl
    row = lambda a: a.reshape(1, -1).astype(F32)
    o_ki = QKV_WIDTH
    o_wi = o_ki + IDX_DIM
    o_zs = o_wi + IDX_HEADS
    pad_cols = lambda a, wdt: jnp.pad(a, ((0, 0), (0, wdt - a.shape[1])))
    lora_rows = lambda a, off: jnp.pad(a, ((off, LORA_PAD - off - a.shape[0]), (0, 0))).astype(BF16)
    return {
        "norm_mix": row(norm_mix),
        "wa": w_in[:, :QKV_WIDTH].astype(BF16),
        "wb": pad_cols(w_in[:, o_ki:o_zs], LANES).astype(BF16),
        "wc": pad_cols(w_in[:, o_zs:], SHIFT_PAD).astype(BF16),
        "q_gain": row(jnp.tile(q_gain, N_HEADS)),
        "k_gain": row(jnp.tile(k_gain, N_KV_HEADS)),
        "ln_g": pad_cols(row(kidx_ln_g), LANES),
        "ln_b": pad_cols(row(kidx_ln_b), LANES),
        "bd": _block_diag_ones(),
        "mu": pad_cols(row(mu_shift), SHIFT_PAD),
        "w0": row(w0), "a0": row(a0),
        "w2": lora_rows(w2, 0), "a2": lora_rows(a2, W_LORA), "g2": lora_rows(g2, W_LORA + A_LORA),
        "k_k": row(k_k), "k_a": row(k_a), "r_k": row(r_k), "lnx_g": row(ln_x_g), "lnx_b": row(ln_x_b),
        "wo_a": w_out[:ATT_WIDTH].astype(BF16), "wo_b": w_out[ATT_WIDTH:].astype(BF16),
        "norm_ffn": row(norm_ffn),
        "w_ff1": w_ff1.astype(BF16), "w_ff2": w_ff2.astype(BF16),
    }


def _layer(x, pos0, k_past, v_past, ki_past, wkv0, shift0, p, cnt_rhs, *, chunk, n_grp, tq):
    b, t, _ = x.shape
    n = b * t
    past = k_past.shape[1]
    x2d = x.reshape(n, D_MODEL)
    tm = min(ROW_TILE, n)
    q, k, v, qi, misc, zs, kbf, vbf, kibf = _inproj(x2d, t, pos0, p, tm)

    shift0p = jnp.pad(shift0, ((0, 0), (0, 0), (0, SHIFT_PAD - SHIFT_WIDTH)))
    zs3 = zs.reshape(b, t, SHIFT_PAD)
    o_rwkv, wkv_new = _rwkv(zs3, shift0p, wkv0, p, chunk, min(b, RWKV_BATCH))

    n_valid = past + t
    topk = min(TOPK_MAX, n_valid // 4)
    three = lambda a: a.reshape(b, t, a.shape[-1])
    k_new, v_new, ki_new = three(kbf), three(vbf), three(kibf)
    q3, qi3, misc3 = three(q), three(qi), three(misc)
    if past:
        ki_main = jnp.pad(ki_past.astype(BF16), ((0, 0), (0, 0), (0, LANES - IDX_DIM)))
        main = (k_past.reshape(b, past, KV_WIDTH), v_past.reshape(b, past, KV_WIDTH), ki_main)
        o_att = _attention(q3, qi3, misc3, main, (k_new, v_new, ki_new), cnt_rhs, n_grp=n_grp, tq=tq,
                           pos0=pos0, n_valid=n_valid, topk=topk)
    else:
        o_att = _attention(q3, qi3, misc3, (k_new, v_new, ki_new), None, cnt_rhs, n_grp=n_grp, tq=tq,
                           pos0=pos0, n_valid=n_valid, topk=topk)

    y = _outffn(x2d, o_att.reshape(n, ATT_WIDTH), o_rwkv.reshape(n, RWKV_WIDTH), p, tm)
    return (y.reshape(b, t, D_MODEL), k.reshape(b, t, N_KV_HEADS, HEAD_DIM),
            v.reshape(b, t, N_KV_HEADS, HEAD_DIM), misc3[:, :, :IDX_DIM], wkv_new,
            zs3[:, t - 1:, :SHIFT_WIDTH])


def kernel(x_prompt, x_sample, cache_k, cache_v, cache_kidx, state_wkv, state_shift, norm_mix, w_in, q_gain, k_gain, kidx_ln_g, kidx_ln_b, mu_shift, w0, w2, a0, a2, g2, k_k, k_a, r_k, ln_x_g, ln_x_b, w_out, norm_ffn, w_ff1, w_ff2):
    depth = norm_mix.shape[0]
    bp, tp = x_prompt.shape[0], x_prompt.shape[1]
    bs, ts = x_sample.shape[0], x_sample.shape[1]
    past_len = cache_k.shape[2]
    cnt_rhs = _count_rhs()
    no_kv = jnp.zeros((bp, 0, KV_WIDTH), F32)
    no_ki = jnp.zeros((bp, 0, IDX_DIM), F32)
    wkv_zero = jnp.zeros((bp, RWKV_HEADS, HEAD_DIM, HEAD_DIM), F32)
    shift_zero = jnp.zeros((bp, 1, SHIFT_WIDTH), F32)

    y_p, y_s = x_prompt, x_sample
    outs_p, outs_s = [], []
    for l in range(depth):
        wl = (norm_mix[l], w_in[l], q_gain[l], k_gain[l], kidx_ln_g[l], kidx_ln_b[l], mu_shift[l],
              w0[l], w2[l], a0[l], a2[l], g2[l], k_k[l], k_a[l], r_k[l], ln_x_g[l], ln_x_b[l],
              w_out[l], norm_ffn[l], w_ff1[l], w_ff2[l])
        p = _prep_params(wl)
        y_p, *rest_p = _layer(y_p, 0, no_kv, no_kv, no_ki, wkv_zero, shift_zero, p, cnt_rhs,
                              chunk=64, n_grp=1, tq=256)
        y_s, *rest_s = _layer(y_s, past_len, cache_k[l], cache_v[l], cache_kidx[l], state_wkv[l],
                              state_shift[l], p, cnt_rhs, chunk=ts, n_grp=min(bs, 256 // ts), tq=ts)
        outs_p.append(rest_p)
        outs_s.append(rest_s)
    stack = lambda outs, i: jnp.stack([o[i] for o in outs])
    return (y_p, y_s,
            *(stack(outs_p, i) for i in range(5)),
            *(stack(outs_s, i) for i in range(5)))
```

```python
import functools

import numpy as np
import jax
import jax.numpy as jnp
from jax import lax
from jax.experimental import pallas as pl
from jax.experimental.pallas import tpu as pltpu

F32 = jnp.float32
BF16 = jnp.bfloat16

D_MODEL = 1024
CHUNK = 64
HEAD_DIM = 64
ATT_WIDTH = D_MODEL // 2
N_HEADS = ATT_WIDTH // HEAD_DIM
N_KV_HEADS = N_HEADS // 2
GQA_GROUP = N_HEADS // N_KV_HEADS
KV_WIDTH = N_KV_HEADS * HEAD_DIM
ROT_DIM = HEAD_DIM // 4
ROPE_THETA = 500000.0
IDX_HEADS = 8
IDX_DIM = 64
IDX_WIDTH = IDX_HEADS * IDX_DIM
TOPK_MAX = 256
RWKV_WIDTH = D_MODEL - ATT_WIDTH
RWKV_HEADS = RWKV_WIDTH // HEAD_DIM
W_LORA = 32
A_LORA = 32
G_LORA = 96
LORA_WIDTH = W_LORA + A_LORA + G_LORA
SHIFT_WIDTH = 3 * RWKV_WIDTH + LORA_WIDTH
D_FF = 4 * D_MODEL
NORM_EPS = 1e-6
GN_EPS = 64e-5
L2_EPS = 1e-12

LANES = 128
SHIFT_PAD = 3 * RWKV_WIDTH + 2 * LANES
LORA_PAD = SHIFT_PAD - 3 * RWKV_WIDTH
QKV_WIDTH = ATT_WIDTH + 2 * KV_WIDTH + IDX_WIDTH
INV_BLOCK = 16
INV_SHIFT = 4
CHUNK_SHIFT = 6
MASK_BIAS = -1e30
ROW_TILE = 512
RWKV_BATCH = 8
STACK_ROWS = 512
BISECT_ITERS = 32
BISECT_CHAINS = 4
BISECT_UNROLL = 16
VMEM_LIMIT = 56 * 1024 * 1024

NN = (((1,), (0,)), ((), ()))
NT = (((1,), (1,)), ((), ()))
TN = (((0,), (0,)), ((), ()))


def _dot(a, b, dims=NN):
    return lax.dot_general(a.astype(BF16), b.astype(BF16), dims, preferred_element_type=F32)


def _group_sum(x, bd):
    w = x.shape[-1]
    return _dot(x, bd[:w, :w])


def _rope(x, cos, sa, sb):
    w = x.shape[-1]
    reps = w // LANES
    if reps > 1:
        cos, sa, sb = (jnp.tile(t, (1, reps)) for t in (cos, sa, sb))
    half = ROT_DIM // 2
    return x * cos + pltpu.roll(x, w - half, 1) * sa + pltpu.roll(x, half, 1) * sb


def _spread_heads(x, fill):
    low = lax.broadcasted_iota(jnp.int32, (x.shape[0], LANES), 1) < HEAD_DIM
    cols = []
    for c in range(x.shape[1] // LANES):
        pair = x[:, c * LANES:(c + 1) * LANES]
        cols.append(jnp.where(low, pair, fill))
        cols.append(jnp.where(low, pltpu.roll(pair, HEAD_DIM, 1), fill))
    return jnp.concatenate(cols, axis=1)


def _pack_heads(slabs):
    low = lax.broadcasted_iota(jnp.int32, slabs[0].shape, 1) < HEAD_DIM
    cols = [jnp.where(low, slabs[i], pltpu.roll(slabs[i + 1], HEAD_DIM, 1)) for i in range(0, len(slabs), 2)]
    return jnp.concatenate(cols, axis=1)


def _inproj_kernel(x_ref, g_ref, wa_ref, wb_ref, wc_ref, qg_ref, kg_ref, lng_ref, lnb_ref, bd_ref,
                   cos_ref, sa_ref, sb_ref,
                   q_ref, k_ref, v_ref, qi_ref, misc_ref, zs_ref, kbf_ref, vbf_ref, kibf_ref):
    x = x_ref[...]
    ms = jnp.mean(x * x, axis=-1, keepdims=True)
    xn = (x * lax.rsqrt(ms + NORM_EPS) * g_ref[...]).astype(BF16)
    za = jnp.dot(xn, wa_ref[...], preferred_element_type=F32)
    zb = jnp.dot(xn, wb_ref[...], preferred_element_type=F32)
    zs_ref[...] = jnp.dot(xn, wc_ref[...], preferred_element_type=F32)

    cos, sa, sb = cos_ref[...], sa_ref[...], sb_ref[...]
    bd = bd_ref[...]

    zq = za[:, :ATT_WIDTH]
    qn = zq * lax.rsqrt(_group_sum(zq * zq, bd) * (1.0 / HEAD_DIM) + NORM_EPS) * qg_ref[...]
    q_ref[...] = _spread_heads(_rope(qn, cos, sa, sb) * (HEAD_DIM ** -0.5), 0.0).astype(BF16)

    zk = za[:, ATT_WIDTH:ATT_WIDTH + KV_WIDTH]
    kn = zk * lax.rsqrt(_group_sum(zk * zk, bd) * (1.0 / HEAD_DIM) + NORM_EPS) * kg_ref[...]
    k = _rope(kn, cos, sa, sb)
    k_ref[...] = k
    kbf_ref[...] = _spread_heads(k, 0.0).astype(BF16)

    v = za[:, ATT_WIDTH + KV_WIDTH:ATT_WIDTH + 2 * KV_WIDTH]
    v_ref[...] = v
    vbf_ref[...] = _spread_heads(v, 1.0).astype(BF16)

    zqi = za[:, ATT_WIDTH + 2 * KV_WIDTH:]
    qi_ref[...] = _spread_heads(_rope(zqi, cos, sa, sb) * (IDX_DIM ** -0.5), 0.0).astype(BF16)

    lane = lax.broadcasted_iota(jnp.int32, zb.shape, 1)
    is_ki = lane < IDX_DIM
    mu = jnp.sum(jnp.where(is_ki, zb, 0.0), axis=-1, keepdims=True) * (1.0 / IDX_DIM)
    d = jnp.where(is_ki, zb - mu, 0.0)
    var = jnp.sum(d * d, axis=-1, keepdims=True) * (1.0 / IDX_DIM)
    kin = d * lax.rsqrt(var + NORM_EPS) * lng_ref[...] + lnb_ref[...]
    ki = _rope(kin, cos, sa, sb)
    misc_ref[...] = jnp.where(is_ki, ki, zb * (IDX_HEADS ** -0.5))
    kibf_ref[...] = jnp.where(is_ki, ki, 0.0).astype(BF16)


def _rope_tables(pos):
    half = ROT_DIM // 2
    inv = ROPE_THETA ** (-jnp.arange(0, ROT_DIM, 2, dtype=F32) / ROT_DIM)
    ang = pos.astype(F32)[:, None] * inv[None, :]
    cos, sin = jnp.cos(ang), jnp.sin(ang)
    n = pos.shape[0]
    ones = jnp.ones((n, HEAD_DIM - ROT_DIM), F32)
    zeros_h = jnp.zeros((n, half), F32)
    zeros_r = jnp.zeros((n, HEAD_DIM - ROT_DIM), F32)
    c64 = jnp.concatenate([cos, cos, ones], axis=1)
    sa64 = jnp.concatenate([-sin, zeros_h, zeros_r], axis=1)
    sb64 = jnp.concatenate([zeros_h, sin, zeros_r], axis=1)
    rep = LANES // HEAD_DIM
    return tuple(jnp.tile(t, (1, rep)) for t in (c64, sa64, sb64))


def _inproj(x2d, seq_len, pos0, p, tm):
    n = x2d.shape[0]
    tab_rows = max(seq_len, tm)
    pos = pos0 + (jnp.arange(tab_rows, dtype=jnp.int32) % seq_len)
    cos, sa, sb = _rope_tables(pos)
    nblk_t = tab_rows // tm

    def row(i):
        return (i, 0)

    def const(i):
        return (0, 0)

    def tab(i):
        return (i % nblk_t, 0)

    full = lambda a: pl.BlockSpec(a.shape, const)
    outs = [
        (N_HEADS * LANES, BF16), (KV_WIDTH, F32), (KV_WIDTH, F32), (IDX_HEADS * LANES, BF16), (LANES, F32),
        (SHIFT_PAD, F32), (N_KV_HEADS * LANES, BF16), (N_KV_HEADS * LANES, BF16), (LANES, BF16),
    ]
    params = [p["norm_mix"], p["wa"], p["wb"], p["wc"], p["q_gain"], p["k_gain"], p["ln_g"], p["ln_b"], p["bd"]]
    return pl.pallas_call(
        _inproj_kernel,
        grid=(n // tm,),
        in_specs=[pl.BlockSpec((tm, D_MODEL), row)] + [full(a) for a in params]
        + [pl.BlockSpec((tm, LANES), tab)] * 3,
        out_specs=[pl.BlockSpec((tm, w), row) for w, _ in outs],
        out_shape=[jax.ShapeDtypeStruct((n, w), dt) for w, dt in outs],
        compiler_params=pltpu.CompilerParams(dimension_semantics=("parallel",), vmem_limit_bytes=VMEM_LIMIT),
        name="inproj",
    )(x2d, *params, cos, sa, sb)


def _left_head(shape):
    return lax.broadcasted_iota(jnp.int32, shape, 1) < shape[1] // 2


def _bdiag(x):
    x = x.astype(BF16)
    zero = jnp.zeros_like(x)
    left = _left_head(x.shape)
    return jnp.concatenate([jnp.where(left, x, zero), jnp.where(left, zero, x)], axis=0)


def _pdot(x, y, dims=NN):
    return lax.dot_general(x.astype(BF16), _bdiag(y), dims, preferred_element_type=F32)


def _solve_unit_lower(lows, rhss, eye, blk):
    c = lows[0].shape[0]
    d = [jnp.where(blk, l, 0.0) for l in lows]
    e = [l - dh for l, dh in zip(lows, d)]
    dk = [_pdot(dh, dh) for dh in d]
    p = [_pdot(eye - dh, eye + d2) for dh, d2 in zip(d, dk)]
    span = 4
    while span < INV_BLOCK:
        dk = [_pdot(m, m) for m in dk]
        p = [_pdot(ph, eye + m) for ph, m in zip(p, dk)]
        span *= 2
    x = [_pdot(ph, r) for ph, r in zip(p, rhss)]
    nb = c // INV_BLOCK
    if nb > 1:
        nk = [_pdot(ph, eh) for ph, eh in zip(p, e)]
        x = [_pdot(eye - m, xh) for m, xh in zip(nk, x)]
        order = 2
        while order < nb:
            nk = [_pdot(m, m) for m in nk]
            x = [_pdot(eye + m, xh) for m, xh in zip(nk, x)]
            order *= 2
    return x


def _rwkv_kernel(zs_ref, shift0_ref, wkv0_ref, mu_ref, w0_ref, w2_ref, a0_ref, a2_ref, g2_ref,
                 kk_ref, ka_ref, rk_ref, lng_ref, lnb_ref, bd_ref,
                 o_ref, sout_ref, s_scr, prev_scr, *, chunk):
    c = chunk
    nb = zs_ref.shape[0]
    ci = pl.program_id(1)
    w = RWKV_WIDTH
    pairs = RWKV_HEADS // 2

    @pl.when(ci == 0)
    def _():
        for j in range(nb):
            for p in range(pairs):
                s_scr[j, p] = jnp.concatenate([wkv0_ref[j, 2 * p], wkv0_ref[j, 2 * p + 1]], axis=1)
        prev_scr[...] = shift0_ref[...]

    row = lax.broadcasted_iota(jnp.int32, (c, 2 * c), 0)
    lane = lax.broadcasted_iota(jnp.int32, (c, 2 * c), 1)
    col = jnp.bitwise_and(lane, c - 1)
    tri_incl = row >= col
    tri_strict = row > col
    eye = jnp.where(row == col, 1.0, 0.0)
    blk = jnp.right_shift(row, INV_SHIFT) == jnp.right_shift(col, INV_SHIFT)
    trow = lax.broadcasted_iota(jnp.int32, (c, c), 0)
    tcol = lax.broadcasted_iota(jnp.int32, (c, c), 1)
    tri = jnp.where(trow >= tcol, 1.0, 0.0).astype(BF16)
    mu = mu_ref[...]
    bd = bd_ref[...]
    sls = [slice(p * LANES, (p + 1) * LANES) for p in range(pairs)]

    rs, ks, vs, tails = [], [], [], []
    for j in range(nb):
        z = zs_ref[j]
        prev_row = prev_scr[j]

        def mixed(lo, hi):
            zp = z[:, lo:hi]
            rolled = pltpu.roll(zp, 1, 0)
            first = lax.broadcasted_iota(jnp.int32, zp.shape, 0) == 0
            prv = jnp.where(first, prev_row[:, lo:hi], rolled)
            return zp + mu[:, lo:hi] * (prv - zp)

        rs.append(mixed(0, w))
        ks.append(mixed(w, 2 * w))
        vs.append(mixed(2 * w, 3 * w))
        tails.append(mixed(3 * w, SHIFT_PAD))
        prev_scr[j] = z[c - 1:c, :]
    stack = lambda xs: xs[0] if nb == 1 else jnp.concatenate(xs, axis=0)
    r, k, v, tail = stack(rs), stack(ks), stack(vs), stack(tails)

    lw = w0_ref[...] + _dot(jnp.tanh(tail), w2_ref[...])
    nlw = -lw
    softplus = jnp.maximum(nlw, 0.0) + jnp.log(1.0 + jnp.exp(-jnp.abs(nlw)))
    logd = -jnp.exp(-softplus - 0.5)
    a = 1.0 / (1.0 + jnp.exp(-(a0_ref[...] + _dot(tail, a2_ref[...]))))
    gate = _dot(1.0 / (1.0 + jnp.exp(-tail)), g2_ref[...])

    kk = k * kk_ref[...]
    kk = kk / jnp.maximum(jnp.sqrt(_group_sum(kk * kk, bd)), L2_EPS)
    k = k * (1.0 + (a - 1.0) * ka_ref[...])
    b = kk * a

    hi = logd.astype(BF16)
    rem = logd - hi.astype(F32)
    mid = rem.astype(BF16)
    lo = (rem - mid.astype(F32)).astype(BF16)
    lg = stack([jnp.dot(tri, hi[rj], preferred_element_type=F32) + jnp.dot(tri, mid[rj], preferred_element_type=F32)
                + jnp.dot(tri, lo[rj], preferred_element_type=F32)
                for rj in (slice(j * c, (j + 1) * c) for j in range(nb))])
    g = jnp.exp(lg)
    ginv = jnp.exp(-lg)
    a_m = jnp.exp(lg - logd) * kk
    b_m = b * ginv
    k_m = k * ginv
    p_m = g * r
    bonus = _group_sum(r * k * rk_ref[...], bd) * v

    ar, km, bm, vh, g_last = [], [], [], [], []
    for j in range(nb):
        rj = slice(j * c, (j + 1) * c)
        ar += [jnp.concatenate([a_m[rj, sl], p_m[rj, sl]], axis=0).astype(BF16) for sl in sls]
        km += [k_m[rj, sl].astype(BF16) for sl in sls]
        bm += [b_m[rj, sl].astype(BF16) for sl in sls]
        vh += [v[rj, sl].astype(BF16) for sl in sls]
        g_last += [g[(j + 1) * c - 1:(j + 1) * c, sl] for sl in sls]

    chains = range(nb * pairs)
    s_h = [s_scr[j, p] for j in range(nb) for p in range(pairs)]
    w2c = 2 * c
    gs = [lax.dot_general(ar[i], jnp.concatenate([_bdiag(km[i]), _bdiag(bm[i]), _bdiag(s_h[i])], axis=0), NT,
                          preferred_element_type=F32) for i in chains]
    g_k = [m[:, :w2c] for m in gs]
    g_b = [m[:, w2c:2 * w2c] for m in gs]
    a_s = [m[:, 2 * w2c:] for m in gs]
    rhs = [a_s[i][:c] + _pdot(jnp.where(tri_strict, g_k[i][:c], 0.0), vh[i]) for i in chains]
    u = _solve_unit_lower([jnp.where(tri_strict, g_b[i][:c], 0.0) for i in chains], rhs, eye, blk)
    ys = []
    for i in chains:
        j, p = divmod(i, pairs)
        n_pk = jnp.where(tri_incl, g_k[i][c:], 0.0)
        n_pb = jnp.where(tri_incl, g_b[i][c:], 0.0)
        ys.append(a_s[i][c:] + _dot(jnp.concatenate([n_pk, -n_pb], axis=1),
                                     jnp.concatenate([_bdiag(vh[i]), _bdiag(u[i])], axis=0)))
        full = _dot(jnp.concatenate([vh[i], -u[i].astype(BF16)], axis=0),
                    jnp.concatenate([km[i], bm[i]], axis=0), TN)
        diag = jnp.where(_left_head((HEAD_DIM, LANES)), full[:HEAD_DIM], full[HEAD_DIM:])
        s_scr[j, p] = (s_h[i] + diag) * g_last[i]

    y = stack([jnp.concatenate(ys[j * pairs:(j + 1) * pairs], axis=1) for j in range(nb)])
    yc = y - _group_sum(y, bd) * (1.0 / HEAD_DIM)
    var = _group_sum(yc * yc, bd) * (1.0 / HEAD_DIM)
    yn = yc * lax.rsqrt(var + GN_EPS) * lng_ref[...] + lnb_ref[...]
    o = ((yn + bonus) * gate).astype(o_ref.dtype)
    for j in range(nb):
        o_ref[j] = o[j * c:(j + 1) * c]

    @pl.when(ci == pl.num_programs(1) - 1)
    def _():
        for j in range(nb):
            for p in range(pairs):
                sp = s_scr[j, p]
                sout_ref[j, 2 * p] = sp[:, :HEAD_DIM]
                sout_ref[j, 2 * p + 1] = sp[:, HEAD_DIM:]


def _rwkv(zs3, shift0, wkv0, p, chunk, nb):
    b, t, _ = zs3.shape
    const = lambda bi, ci: (0, 0)
    full = lambda a: pl.BlockSpec(a.shape, const)
    params = [p["mu"], p["w0"], p["w2"], p["a0"], p["a2"], p["g2"], p["k_k"], p["k_a"], p["r_k"],
              p["lnx_g"], p["lnx_b"], p["bd"]]
    state_spec = pl.BlockSpec((nb, RWKV_HEADS, HEAD_DIM, HEAD_DIM), lambda bi, ci: (bi, 0, 0, 0))
    return pl.pallas_call(
        functools.partial(_rwkv_kernel, chunk=chunk),
        grid=(b // nb, t // chunk),
        in_specs=[pl.BlockSpec((nb, chunk, SHIFT_PAD), lambda bi, ci: (bi, ci, 0)),
                  pl.BlockSpec((nb, 1, SHIFT_PAD), lambda bi, ci: (bi, 0, 0)),
                  state_spec] + [full(a) for a in params],
        out_specs=[pl.BlockSpec((nb, chunk, RWKV_WIDTH), lambda bi, ci: (bi, ci, 0)), state_spec],
        out_shape=[jax.ShapeDtypeStruct((b, t, RWKV_WIDTH), BF16),
                   jax.ShapeDtypeStruct((b, RWKV_HEADS, HEAD_DIM, HEAD_DIM), F32)],
        scratch_shapes=[pltpu.VMEM((nb, RWKV_HEADS // 2, HEAD_DIM, LANES), F32),
                        pltpu.VMEM((nb, 1, SHIFT_PAD), F32)],
        compiler_params=pltpu.CompilerParams(dimension_semantics=("parallel", "arbitrary"),
                                             vmem_limit_bytes=VMEM_LIMIT),
        name="rwkv7",
    )(zs3, shift0, wkv0, *params)


def _attn_kernel(*refs, n_grp, tq, n_main, n_extra, qb, pos0, n_valid, topk):
    if n_extra:
        q_ref, qi_ref, misc_ref, k_ref, v_ref, ki_ref, kx_ref, vx_ref, kix_ref, cnt_ref, o_ref = refs
    else:
        q_ref, qi_ref, misc_ref, k_ref, v_ref, ki_ref, cnt_ref, o_ref, logit_scr = refs
    n_keys = n_main + (LANES if n_extra else 0)
    nl = n_keys // LANES
    rows = n_grp * tq

    def pad_rows(x):
        return jnp.concatenate([x, jnp.zeros((LANES - n_extra, x.shape[1]), x.dtype)], axis=0)

    def dot_keys(lhs, main, extra):
        out = lax.dot_general(lhs, main, NT, preferred_element_type=F32)
        if n_extra:
            out = jnp.concatenate([out, lax.dot_general(lhs, extra, NT, preferred_element_type=F32)], axis=1)
        return out

    def row_limit(r):
        return min(((pos0 + qb * tq + r % tq) // CHUNK + 1) * CHUNK, n_valid)

    pos = pos0 + qb * tq + lax.broadcasted_iota(jnp.int32, (tq, LANES), 0)
    limit = jnp.minimum((jnp.right_shift(pos, CHUNK_SHIFT) + 1) * CHUNK, n_valid)
    if n_grp > 1:
        limit = jnp.concatenate([limit] * n_grp, axis=0)
    key_id = lax.broadcasted_iota(jnp.int32, (rows, n_keys), 1)
    adm = key_id < jnp.tile(limit, (1, nl))

    def topk_bias():
        parts = []
        for g in range(n_grp):
            qi = qi_ref[g]
            ki = ki_ref[g]
            kix = pad_rows(kix_ref[g]) if n_extra else None
            misc = misc_ref[g]
            sc = None
            if tq * IDX_HEADS <= STACK_ROWS:
                d_all = dot_keys(jnp.concatenate([qi[:, h * LANES:(h + 1) * LANES] for h in range(IDX_HEADS)],
                                                 axis=0), ki, kix)
            for h in range(IDX_HEADS):
                if tq * IDX_HEADS <= STACK_ROWS:
                    d = d_all[h * tq:(h + 1) * tq]
                else:
                    d = dot_keys(qi[:, h * LANES:(h + 1) * LANES], ki, kix)
                term = misc[:, IDX_DIM + h:IDX_DIM + h + 1] * jnp.maximum(d, 0.0)
                sc = term if sc is None else sc + term
            parts.append(sc)
        score = parts[0] if n_grp == 1 else jnp.concatenate(parts, axis=0)
        sm = jnp.where(adm, score, -jnp.inf)

        cnt_rhs = cnt_ref[...]
        ones_sq = cnt_ref[:, LANES:]

        def count_ge(sm_c, th, tiles):
            part = None
            for j in range(tiles):
                ind = jnp.where(sm_c[:, j * LANES:(j + 1) * LANES] >= th, 1.0, 0.0)
                part = ind if part is None else part + ind
            return jnp.dot(part.astype(BF16), ones_sq, preferred_element_type=F32)

        row_max = jnp.max(sm, axis=-1, keepdims=True)
        row_min = jnp.min(jnp.where(adm, score, jnp.inf), axis=-1, keepdims=True)
        lo0 = jnp.broadcast_to(row_min, (rows, LANES))
        hi0 = jnp.broadcast_to(2.0 * row_max - row_min + 1.0, (rows, LANES))

        rc = rows // BISECT_CHAINS
        chains = [slice(i * rc, (i + 1) * rc) for i in range(BISECT_CHAINS)]
        tiles = [-(-max(row_limit(r) for r in range(ch.start, ch.stop)) // LANES) for ch in chains]

        def bisect(_, carry):
            out = []
            for ch, nt, (lo, hi) in zip(chains, tiles, carry):
                mid = 0.5 * (lo + hi)
                enough = count_ge(sm[ch], mid, nt) >= topk
                out.append((jnp.where(enough, mid, lo), jnp.where(enough, hi, mid)))
            return tuple(out)

        bounds = lax.fori_loop(0, BISECT_ITERS, bisect, tuple((lo0[ch], hi0[ch]) for ch in chains),
                               unroll=BISECT_UNROLL)
        lo = jnp.concatenate([b[0] for b in bounds], axis=0)
        hi = jnp.concatenate([b[1] for b in bounds], axis=0)

        need = topk - count_ge(sm, hi, nl)
        take_all = limit <= topk
        carry = jnp.zeros((rows, LANES), F32)
        bias = []
        for j in range(nl):
            sm_j = sm[:, j * LANES:(j + 1) * LANES]
            above = sm_j >= hi
            at_least_lo = sm_j >= lo
            group = jnp.where(above, 0.0, jnp.where(at_least_lo, 1.0, 0.0))
            pc = jnp.dot(group.astype(BF16), cnt_rhs, preferred_element_type=F32)
            fill = carry + pc[:, :LANES] <= need
            carry = carry + pc[:, LANES:]
            b_group = jnp.where(at_least_lo, jnp.where(fill, 0.0, MASK_BIAS), MASK_BIAS)
            b_pick = jnp.where(take_all, 0.0, jnp.where(above, 0.0, b_group))
            bias.append(jnp.where(sm_j > -jnp.inf, b_pick, MASK_BIAS))
        return jnp.concatenate(bias, axis=1)

    def keys_values(g):
        km, vm = k_ref[g], v_ref[g]
        if km.dtype != BF16:
            km = _spread_heads(km, 0.0).astype(BF16)
            vm = _spread_heads(vm, 1.0).astype(BF16)
        kx, vx = (pad_rows(kx_ref[g]), pad_rows(vx_ref[g])) if n_extra else (None, None)
        return km, vm, kx, vx

    def qk_logits(g, gi, km, kx):
        q = q_ref[g]
        gs = slice(gi * LANES, (gi + 1) * LANES)
        qs = jnp.concatenate([q[:, (gi * GQA_GROUP + j) * LANES:(gi * GQA_GROUP + j + 1) * LANES]
                              for j in range(GQA_GROUP)], axis=0)
        return dot_keys(qs, km[:, gs], kx[:, gs] if n_extra else None)

    early = not n_extra
    if early:
        kv = [keys_values(g) for g in range(n_grp)]
        for g in range(n_grp):
            for gi in range(N_KV_HEADS):
                logit_scr[g * N_KV_HEADS + gi] = qk_logits(g, gi, kv[g][0], kv[g][2])

    if max(row_limit(r) for r in range(rows)) <= topk:
        bias = jnp.where(adm, 0.0, MASK_BIAS)
    else:
        bias = topk_bias()

    for g in range(n_grp):
        km, vm, kx, vx = kv[g] if early else keys_values(g)
        bias_g = jnp.concatenate([bias[g * tq:(g + 1) * tq]] * GQA_GROUP, axis=0)
        slabs = []
        for gi in range(N_KV_HEADS):
            gs = slice(gi * LANES, (gi + 1) * LANES)
            s = (logit_scr[g * N_KV_HEADS + gi] if early else qk_logits(g, gi, km, kx)) + bias_g
            m = jnp.max(s, axis=-1, keepdims=True)
            pexp = jnp.exp((s - m).astype(BF16))
            o = jnp.dot(pexp[:, :n_main], vm[:, gs], preferred_element_type=F32)
            if n_extra:
                o = o + jnp.dot(pexp[:, n_main:], vx[:, gs], preferred_element_type=F32)
            o = o / pltpu.roll(o, HEAD_DIM, 1)
            slabs.extend(o[j * tq:(j + 1) * tq] for j in range(GQA_GROUP))
        o_ref[g] = _pack_heads(slabs).astype(o_ref.dtype)


def _attention(q3, qi3, misc3, main, extra, cnt_rhs, *, n_grp, tq, n_main, qb, pos0, n_valid, topk):
    b = q3.shape[0]
    n_extra = extra[0].shape[1] if extra else 0
    qmap = lambda bi: (bi, qb, 0)
    kmap = lambda bi: (bi, 0, 0)
    key_specs = [pl.BlockSpec((n_grp, n_main, a.shape[2]), kmap) for a in main]
    if extra:
        key_specs += [pl.BlockSpec((n_grp, n_extra, a.shape[2]), kmap) for a in extra]
    return pl.pallas_call(
        functools.partial(_attn_kernel, n_grp=n_grp, tq=tq, n_main=n_main, n_extra=n_extra, qb=qb, pos0=pos0,
                          n_valid=n_valid, topk=topk),
        grid=(b // n_grp,),
        in_specs=[pl.BlockSpec((n_grp, tq, N_HEADS * LANES), qmap),
                  pl.BlockSpec((n_grp, tq, IDX_HEADS * LANES), qmap),
                  pl.BlockSpec((n_grp, tq, LANES), qmap)] + key_specs
        + [pl.BlockSpec(cnt_rhs.shape, lambda bi: (0, 0))],
        out_specs=pl.BlockSpec((n_grp, tq, ATT_WIDTH), lambda bi: (bi, 0, 0)),
        out_shape=jax.ShapeDtypeStruct((b, tq, ATT_WIDTH), BF16),
        scratch_shapes=[] if extra else [pltpu.VMEM((n_grp * N_KV_HEADS, GQA_GROUP * tq, n_main), F32)],
        compiler_params=pltpu.CompilerParams(dimension_semantics=("parallel",), vmem_limit_bytes=VMEM_LIMIT),
        name=f"dsa_attn_{n_main + n_extra}",
    )(q3, qi3, misc3, *main, *(extra or ()), cnt_rhs)


def _outffn_kernel(x_ref, oa_ref, ob_ref, wo_a_ref, wo_b_ref, g_ref, w1_ref, w2_ref, y_ref, *, ff_chunk):
    mix = (jnp.dot(oa_ref[...], wo_a_ref[...], preferred_element_type=F32)
           + jnp.dot(ob_ref[...], wo_b_ref[...], preferred_element_type=F32))
    h = x_ref[...] + mix
    ms = jnp.mean(h * h, axis=-1, keepdims=True)
    hn = (h * lax.rsqrt(ms + NORM_EPS) * g_ref[...]).astype(BF16)
    acc = h
    for j in range(D_FF // ff_chunk):
        act = jnp.dot(hn, w1_ref[:, j * ff_chunk:(j + 1) * ff_chunk], preferred_element_type=F32)
        act = jnp.square(jnp.maximum(act, 0.0)).astype(BF16)
        acc = acc + jnp.dot(act, w2_ref[j * ff_chunk:(j + 1) * ff_chunk, :], preferred_element_type=F32)
    y_ref[...] = acc


def _outffn(x2d, o_att, o_rwkv, p, tm):
    n = x2d.shape[0]
    row = lambda i: (i, 0)
    const = lambda i: (0, 0)
    params = [p["wo_a"], p["wo_b"], p["norm_ffn"], p["w_ff1"], p["w_ff2"]]
    return pl.pallas_call(
        functools.partial(_outffn_kernel, ff_chunk=1024),
        grid=(n // tm,),
        in_specs=[pl.BlockSpec((tm, D_MODEL), row), pl.BlockSpec((tm, ATT_WIDTH), row),
                  pl.BlockSpec((tm, RWKV_WIDTH), row)]
        + [pl.BlockSpec(a.shape, const, pipeline_mode=pl.Buffered(1)) for a in params],
        out_specs=pl.BlockSpec((tm, D_MODEL), row),
        out_shape=jax.ShapeDtypeStruct((n, D_MODEL), F32),
        compiler_params=pltpu.CompilerParams(dimension_semantics=("parallel",), vmem_limit_bytes=VMEM_LIMIT),
        name="outproj_ffn",
    )(x2d, o_att, o_rwkv, *params)


def _block_diag_ones():
    i = np.arange(ATT_WIDTH)
    return jnp.asarray((i[:, None] // HEAD_DIM) == (i[None, :] // HEAD_DIM), BF16)


def _count_rhs():
    i = np.arange(LANES)
    tri = (i[:, None] <= i[None, :]).astype(np.float32)
    return jnp.asarray(np.concatenate([tri, np.ones((LANES, LANES), np.float32)], axis=1), BF16)


def _prep_params(wl):
    (norm_mix, w_in, q_gain, k_gain, kidx_ln_g, kidx_ln_b, mu_shift, w0, w2, a0, a2, g2, k_k, k_a, r_k,
     ln_x_g, ln_x_b, w_out, norm_ffn, w_ff1, w_ff2) = wl
    row = lambda a: a.reshape(1, -1).astype(F32)
    o_ki = QKV_WIDTH
    o_wi = o_ki + IDX_DIM
    o_zs = o_wi + IDX_HEADS
    pad_cols = lambda a, wdt: jnp.pad(a, ((0, 0), (0, wdt - a.shape[1])))
    lora_rows = lambda a, off: jnp.pad(a, ((off, LORA_PAD - off - a.shape[0]), (0, 0))).astype(BF16)
    return {
        "norm_mix": row(norm_mix),
        "wa": w_in[:, :QKV_WIDTH].astype(BF16),
        "wb": pad_cols(w_in[:, o_ki:o_zs], LANES).astype(BF16),
        "wc": pad_cols(w_in[:, o_zs:], SHIFT_PAD).astype(BF16),
        "q_gain": row(jnp.tile(q_gain, N_HEADS)),
        "k_gain": row(jnp.tile(k_gain, N_KV_HEADS)),
        "ln_g": pad_cols(row(kidx_ln_g), LANES),
        "ln_b": pad_cols(row(kidx_ln_b), LANES),
        "bd": _block_diag_ones(),
        "mu": pad_cols(row(mu_shift), SHIFT_PAD),
        "w0": row(w0), "a0": row(a0),
        "w2": lora_rows(w2, 0), "a2": lora_rows(a2, W_LORA), "g2": lora_rows(g2, W_LORA + A_LORA),
        "k_k": row(k_k), "k_a": row(k_a), "r_k": row(r_k), "lnx_g": row(ln_x_g), "lnx_b": row(ln_x_b),
        "wo_a": w_out[:ATT_WIDTH].astype(BF16), "wo_b": w_out[ATT_WIDTH:].astype(BF16),
        "norm_ffn": row(norm_ffn),
        "w_ff1": w_ff1.astype(BF16), "w_ff2": w_ff2.astype(BF16),
    }


def _layer(x, pos0, k_past, v_past, ki_past, wkv0, shift0, p, cnt_rhs, *, chunk, n_grp, tq):
    b, t, _ = x.shape
    n = b * t
    past = k_past.shape[1]
    x2d = x.reshape(n, D_MODEL)
    tm = min(ROW_TILE, n)
    q, k, v, qi, misc, zs, kbf, vbf, kibf = _inproj(x2d, t, pos0, p, tm)

    shift0p = jnp.pad(shift0, ((0, 0), (0, 0), (0, SHIFT_PAD - SHIFT_WIDTH)))
    zs3 = zs.reshape(b, t, SHIFT_PAD)
    o_rwkv, wkv_new = _rwkv(zs3, shift0p, wkv0, p, chunk, min(b, RWKV_BATCH))

    n_valid = past + t
    topk = min(TOPK_MAX, n_valid // 4)
    three = lambda a: a.reshape(b, t, a.shape[-1])
    k_new, v_new, ki_new = three(kbf), three(vbf), three(kibf)
    q3, qi3, misc3 = three(q), three(qi), three(misc)
    if past:
        ki_main = jnp.pad(ki_past.astype(BF16), ((0, 0), (0, 0), (0, LANES - IDX_DIM)))
        main = (k_past.reshape(b, past, KV_WIDTH), v_past.reshape(b, past, KV_WIDTH), ki_main)
        o_att = _attention(q3, qi3, misc3, main, (k_new, v_new, ki_new), cnt_rhs, n_grp=n_grp, tq=tq,
                           n_main=past, qb=0, pos0=pos0, n_valid=n_valid, topk=topk)
    else:
        o_att = jnp.concatenate(
            [_attention(q3, qi3, misc3, (k_new, v_new, ki_new), None, cnt_rhs, n_grp=n_grp, tq=tq,
                        n_main=(j + 1) * tq, qb=j, pos0=pos0, n_valid=n_valid, topk=topk)
             for j in range(t // tq)], axis=1)

    y = _outffn(x2d, o_att.reshape(n, ATT_WIDTH), o_rwkv.reshape(n, RWKV_WIDTH), p, tm)
    return (y.reshape(b, t, D_MODEL), k.reshape(b, t, N_KV_HEADS, HEAD_DIM),
            v.reshape(b, t, N_KV_HEADS, HEAD_DIM), misc3[:, :, :IDX_DIM], wkv_new,
            zs3[:, t - 1:, :SHIFT_WIDTH])


def kernel(x_prompt, x_sample, cache_k, cache_v, cache_kidx, state_wkv, state_shift, norm_mix, w_in, q_gain, k_gain, kidx_ln_g, kidx_ln_b, mu_shift, w0, w2, a0, a2, g2, k_k, k_a, r_k, ln_x_g, ln_x_b, w_out, norm_ffn, w_ff1, w_ff2):
    depth = norm_mix.shape[0]
    bp, tp = x_prompt.shape[0], x_prompt.shape[1]
    bs, ts = x_sample.shape[0], x_sample.shape[1]
    past_len = cache_k.shape[2]
    cnt_rhs = _count_rhs()
    no_kv = jnp.zeros((bp, 0, KV_WIDTH), F32)
    no_ki = jnp.zeros((bp, 0, IDX_DIM), F32)
    wkv_zero = jnp.zeros((bp, RWKV_HEADS, HEAD_DIM, HEAD_DIM), F32)
    shift_zero = jnp.zeros((bp, 1, SHIFT_WIDTH), F32)

    y_p, y_s = x_prompt, x_sample
    outs_p, outs_s = [], []
    for l in range(depth):
        wl = (norm_mix[l], w_in[l], q_gain[l], k_gain[l], kidx_ln_g[l], kidx_ln_b[l], mu_shift[l],
              w0[l], w2[l], a0[l], a2[l], g2[l], k_k[l], k_a[l], r_k[l], ln_x_g[l], ln_x_b[l],
              w_out[l], norm_ffn[l], w_ff1[l], w_ff2[l])
        p = _prep_params(wl)
        y_p, *rest_p = _layer(y_p, 0, no_kv, no_kv, no_ki, wkv_zero, shift_zero, p, cnt_rhs,
                              chunk=64, n_grp=1, tq=256)
        y_s, *rest_s = _layer(y_s, past_len, cache_k[l], cache_v[l], cache_kidx[l], state_wkv[l],
                              state_shift[l], p, cnt_rhs, chunk=ts, n_grp=min(bs, 256 // ts), tq=ts)
        outs_p.append(rest_p)
        outs_s.append(rest_s)
    stack = lambda outs, i: jnp.stack([o[i] for o in outs])
    return (y_p, y_s,
            *(stack(outs_p, i) for i in range(5)),
            *(stack(outs_s, i) for i in range(5)))
```
